```python
import math
import jax, jax.numpy as jnp
from jax import lax
import numpy as np

D_MODEL = 1024
BATCH = 8
SEQ = 2048
DEPTH = 2
DEC_BATCH = 128
DEC_SEQ = 1
PAST_LEN = 16384
PAGE_SIZE = 128

N_META = 16
D_CONV = D_MODEL
CONV_W = 3
GLA_HEADS = 4
GLA_KDIM = D_MODEL // 2
GLA_VDIM = D_MODEL
GLA_DK = GLA_KDIM // GLA_HEADS
GLA_DV = GLA_VDIM // GLA_HEADS
GATE_RANK = 16
GATE_NORMALIZER = 16.0
CHUNK = 64
D_FF = -(-8 * D_MODEL // (3 * 256)) * 256
EPS = 1e-6
SPLIT_SIZES = (D_CONV, D_CONV, D_CONV, GLA_KDIM, GLA_KDIM, GLA_VDIM, GLA_VDIM, GATE_RANK, D_MODEL, D_MODEL)
D_IN = sum(SPLIT_SIZES)
SPLIT_IDX = tuple(int(s) for s in np.cumsum(SPLIT_SIZES)[:-1])

kernel_name = "hybrid_shortconv_gla_gated_merge_step"


def rmsnorm(x, g):
    xf = x.astype(jnp.float32)
    y = xf * lax.rsqrt(jnp.mean(xf * xf, axis=-1, keepdims=True) + EPS) * g.astype(jnp.float32)
    return y.astype(x.dtype)


def to_heads(a, d):
    b, t, _ = a.shape
    return a.reshape(b, t, -1, d).transpose(0, 2, 1, 3)


def gla_chunk(S, q, k, v, g):
    L = q.shape[2]
    b = jnp.cumsum(g, axis=2)
    causal = jnp.tril(jnp.ones((L, L), dtype=bool))[:, :, None]
    diff = b[:, :, :, None, :] - b[:, :, None, :, :]
    decay = jnp.where(causal, jnp.exp(jnp.where(causal, diff, 0.0)), 0.0)
    A = jnp.einsum('bhtd,bhsd,bhtsd->bhts', q, k, decay)
    o = jnp.einsum('bhts,bhsv->bhtv', A, v) + jnp.einsum('bhtd,bhdv->bhtv', q * jnp.exp(b), S)
    b_last = b[:, :, -1:, :]
    S_new = jnp.exp(b_last[:, :, 0, :])[..., None] * S + jnp.einsum('bhsd,bhsv->bhdv', k * jnp.exp(b_last - b), v)
    return S_new, o


def gla_scan(S0, q, k, v, g, chunk):
    bsz, h, L, _ = q.shape
    c = min(chunk, L)
    n = -(-L // c)
    pad = n * c - L

    def blocks(a):
        a = jnp.pad(a, ((0, 0), (0, 0), (0, pad), (0, 0)))
        return a.reshape(bsz, h, n, c, a.shape[-1]).transpose(2, 0, 1, 3, 4)

    S, o = lax.scan(lambda s, xs: gla_chunk(s, *xs), S0, (blocks(q), blocks(k), blocks(v), blocks(g)))
    o = o.transpose(1, 2, 0, 3, 4).reshape(bsz, h, n * c, -1)[:, :, :L]
    return S, o


def mixer(xn, conv_prev, S0, n_lead, w_in, conv_w, w_gk2, b_gk2, gla_gain, w_a_out, w_b_out, w_o):
    bsz, T, _ = xn.shape
    proj = xn @ w_in
    gB, gC, h, q, k, v, og, z, ga, gb = jnp.split(proj, SPLIT_IDX, axis=-1)
    u = gC * h
    up = jnp.concatenate([conv_prev.astype(u.dtype), u], axis=1)
    cw = conv_w.astype(u.dtype)
    yconv = cw[0] * up[:, :T] + cw[1] * up[:, 1:T + 1] + cw[2] * up[:, 2:T + 2]
    conv_new = up[:, T:]
    ya = (gB * yconv) @ w_a_out
    gk = jax.nn.log_sigmoid((z @ w_gk2 + b_gk2).astype(jnp.float32)) / GATE_NORMALIZER
    qh = to_heads(q, GLA_DK).astype(jnp.float32) * (GLA_DK ** -0.5)
    kh = to_heads(k, GLA_DK).astype(jnp.float32)
    vh = to_heads(v, GLA_DV).astype(jnp.float32)
    gh = to_heads(gk, GLA_DK)
    S0 = S0.astype(jnp.float32)
    if n_lead > 0:
        S, o_lead = gla_scan(S0, qh[:, :, :n_lead], kh[:, :, :n_lead], vh[:, :, :n_lead], gh[:, :, :n_lead], n_lead)
        S, o_rest = gla_scan(S, qh[:, :, n_lead:], kh[:, :, n_lead:], vh[:, :, n_lead:], gh[:, :, n_lead:], CHUNK)
        o = jnp.concatenate([o_lead, o_rest], axis=2)
    else:
        S, o = gla_scan(S0, qh, kh, vh, gh, CHUNK)
    o = rmsnorm(o, gla_gain)
    o = o.transpose(0, 2, 1, 3).reshape(bsz, T, GLA_VDIM).astype(xn.dtype)
    yb = (o * jax.nn.silu(og)) @ w_b_out
    m = jax.nn.sigmoid(ga) * ya + jax.nn.sigmoid(gb) * yb
    return m @ w_o, conv_new, S


def swiglu(xn, w_gu, w_down):
    g, u = jnp.split(xn @ w_gu, 2, axis=-1)
    return (jax.nn.silu(g) * u) @ w_down


def trunk(x, conv_states, gla_states, n_lead, w_in, conv_w, w_gk2, b_gk2, gla_gain,
          w_a_out, w_b_out, w_o, norm_mix, norm_ffn, w_gu, w_down, final_norm):
    conv_out, gla_out = [], []
    for l in range(DEPTH):
        mo, cs, gs = mixer(rmsnorm(x, norm_mix[l]), conv_states[l], gla_states[l], n_lead,
                           w_in[l], conv_w[l], w_gk2[l], b_gk2[l], gla_gain[l],
                           w_a_out[l], w_b_out[l], w_o[l])
        x = x + mo
        x = x + swiglu(rmsnorm(x, norm_ffn[l]), w_gu[l], w_down[l])
        conv_out.append(cs)
        gla_out.append(gs)
    return rmsnorm(x, final_norm), jnp.stack(conv_out), jnp.stack(gla_out)


def setup_inputs(seed: int = 0) -> dict:
    key = jax.random.key(seed)
    ks = jax.random.split(key, 20)
    nrm = lambda k, shape, s: jax.random.normal(k, shape, jnp.float32) * s
    return {
        "x_prompt": nrm(ks[0], (BATCH, SEQ, D_MODEL), 1.0),
        "x_sample": nrm(ks[1], (DEC_BATCH, DEC_SEQ, D_MODEL), 1.0),
        "state_conv": nrm(ks[2], (DEPTH, DEC_BATCH, CONV_W - 1, D_CONV), 1.0),
        "state_gla": nrm(ks[3], (DEPTH, DEC_BATCH, GLA_HEADS, GLA_DK, GLA_DV), 0.1),
        "meta_tokens": nrm(ks[4], (N_META, D_MODEL), 1.0),
        "w_in": nrm(ks[5], (DEPTH, D_MODEL, D_IN), D_MODEL ** -0.5),
        "conv_w": nrm(ks[6], (DEPTH, CONV_W, D_CONV), CONV_W ** -0.5),
        "w_gk2": nrm(ks[7], (DEPTH, GATE_RANK, GLA_KDIM), GATE_RANK ** -0.5),
        "b_gk2": nrm(ks[8], (DEPTH, GLA_KDIM), 0.1),
        "gla_gain": 1.0 + nrm(ks[9], (DEPTH, GLA_DV), 0.01),
        "w_a_out": nrm(ks[10], (DEPTH, D_CONV, D_MODEL), D_CONV ** -0.5),
        "w_b_out": nrm(ks[11], (DEPTH, GLA_VDIM, D_MODEL), GLA_VDIM ** -0.5),
        "w_o": nrm(ks[12], (DEPTH, D_MODEL, D_MODEL), D_MODEL ** -0.5),
        "norm_mix": 1.0 + nrm(ks[13], (DEPTH, D_MODEL), 0.01),
        "norm_ffn": 1.0 + nrm(ks[14], (DEPTH, D_MODEL), 0.01),
        "w_gu": nrm(ks[15], (DEPTH, D_MODEL, 2 * D_FF), D_MODEL ** -0.5),
        "w_down": nrm(ks[16], (DEPTH, D_FF, D_MODEL), D_FF ** -0.5),
        "final_norm": 1.0 + nrm(ks[17], (D_MODEL,), 0.01),
    }


def reference(x_prompt, x_sample, state_conv, state_gla, meta_tokens, w_in, conv_w, w_gk2, b_gk2,
              gla_gain, w_a_out, w_b_out, w_o, norm_mix, norm_ffn, w_gu, w_down, final_norm):
    weights = (w_in, conv_w, w_gk2, b_gk2, gla_gain, w_a_out, w_b_out, w_o,
               norm_mix, norm_ffn, w_gu, w_down, final_norm)
    bsz = x_prompt.shape[0]
    meta = jnp.broadcast_to(meta_tokens.astype(x_prompt.dtype)[None], (bsz, N_META, D_MODEL))
    xp = jnp.concatenate([meta, x_prompt], axis=1)
    conv0 = jnp.zeros((DEPTH, bsz, CONV_W - 1, D_CONV), x_prompt.dtype)
    gla0 = jnp.zeros((DEPTH, bsz, GLA_HEADS, GLA_DK, GLA_DV), jnp.float32)
    yp, p_conv, p_gla = trunk(xp, conv0, gla0, N_META, *weights)
    y_prompt = yp[:, N_META:]
    y_sample, s_conv, s_gla = trunk(x_sample, state_conv, state_gla, 0, *weights)
    return (y_prompt, y_sample, p_conv, p_gla.astype(x_prompt.dtype),
            s_conv.astype(state_conv.dtype), s_gla.astype(state_gla.dtype))
```

```python
import functools

import jax
import jax.numpy as jnp
from jax import lax
from jax.experimental import pallas as pl
from jax.experimental.pallas import tpu as pltpu

F32 = jnp.float32
BF16 = jnp.bfloat16

D_MODEL = 1024
N_META = 16
D_CONV = D_MODEL
CONV_W = 3
HEADS = 4
DK = 128
DV = 256
KDIM = HEADS * DK
VDIM = HEADS * DV
GATE_RANK = 16
GATE_NORMALIZER = 16.0
D_FF = 2816
EPS = 1e-6

CHUNK = 64
SUB = 8
NB = CHUNK // SUB
LANES = 128
VMEM_LIMIT = 60 * 1024 * 1024

C_GB, C_GC, C_H = 0, 1024, 2048
C_Q, C_K, C_V, C_OG = 3072, 3584, 4096, 5120
C_GA, C_GBT, C_Z = 6144, 7168, 8192
D_INP = 8320


def _rms(x, g):
    ms = jnp.mean(x * x, axis=-1, keepdims=True)
    return x * lax.rsqrt(ms + EPS) * g


def _log_sigmoid(x):
    return jnp.minimum(x, 0.0) - jnp.log1p(jnp.exp(-jnp.abs(x)))


def _sigmoid(x):
    return 1.0 / (1.0 + jnp.exp(-x))


def _dot(a, b):
    return jnp.dot(a, b, preferred_element_type=F32)


def _dot_nt(a, b):
    return lax.dot_general(a, b, (((1,), (1,)), ((), ())), preferred_element_type=F32)


def _dot_tn(a, b):
    return lax.dot_general(a, b, (((0,), (0,)), ((), ())), preferred_element_type=F32)


def _split3(x):
    x1 = x.astype(BF16)
    r1 = x - x1.astype(F32)
    x2 = r1.astype(BF16)
    r2 = r1 - x2.astype(F32)
    return x1, x2, r2.astype(BF16)


def _head_norm_gate(o, og, gain):
    outs = []
    for h in range(HEADS):
        oh = o[:, h * DV:(h + 1) * DV]
        outs.append(_rms(oh, gain))
    on = jnp.concatenate(outs, axis=1)
    return on * (og * _sigmoid(og))


def _mixer_seq_kernel(x_ref, cprev_ref, s0_ref, win_ref, wa_ref, wb_ref, wo_ref, wgk_ref,
                      nmix_ref, convw_ref, bgk_ref, gain_ref, lmat_ref, emat_ref,
                      xo_ref, cnew_ref, sout_ref,
                      ubuf, st_s, q_s, k_s, b_s, v_s, o_s):
    t = pl.program_id(1)
    nt = pl.num_programs(1)
    tm = x_ref.shape[1]

    @pl.when(t == 0)
    def _init():
        ubuf[0:SUB, :] = cprev_ref[0]
        for h in range(HEADS):
            st_s[h] = s0_ref[0, h].T

    x = x_ref[0]
    xn = _rms(x, nmix_ref[...]).astype(BF16)

    def proj(c0, n):
        return _dot(xn, win_ref[:, c0:c0 + n])

    u = proj(C_GC, D_CONV) * proj(C_H, D_CONV)
    ubuf[SUB:SUB + tm, :] = u
    cw = convw_ref[...]
    yconv = (cw[0:1] * ubuf[SUB - 2:SUB - 2 + tm, :]
             + cw[1:2] * ubuf[SUB - 1:SUB - 1 + tm, :]
             + cw[2:3] * u)
    ca = (proj(C_GB, D_CONV) * yconv).astype(BF16)
    ya = _dot(ca, wa_ref[...])
    tail = ubuf[tm:tm + SUB, :]
    ubuf[0:SUB, :] = tail
    cnew_ref[0] = tail

    q_s[...] = proj(C_Q, KDIM) * (DK ** -0.5)
    k_s[...] = proj(C_K, KDIM)
    v_s[...] = proj(C_V, VDIM).astype(BF16)
    z = proj(C_Z, LANES).astype(BF16)
    g = _log_sigmoid(_dot(z, wgk_ref[...]) + bgk_ref[...]) * (1.0 / GATE_NORMALIZER)
    g1, g2, g3 = _split3(g)
    lmat = lmat_ref[...]
    b_s[...] = _dot(lmat, g1) + _dot(lmat, g2) + _dot(lmat, g3)

    ti = lax.broadcasted_iota(jnp.int32, (CHUNK, CHUNK), 0)
    si = lax.broadcasted_iota(jnp.int32, (CHUNK, CHUNK), 1)
    tb, sb = ti >> 3, si >> 3
    inter_masks = [(sb == j) & (tb > j) for j in range(NB - 1)]
    diag_mask = (sb == tb) & ((si & 7) <= (ti & 7))
    emat = emat_ref[...]

    def chunk_body(c, carry):
        r0 = pl.multiple_of(c * CHUNK, CHUNK)
        rows = pl.ds(r0, CHUNK)
        for h in range(HEADS):
            kcols = slice(h * DK, (h + 1) * DK)
            vcols = slice(h * DV, (h + 1) * DV)
            qc = q_s[rows, kcols]
            kc = k_s[rows, kcols]
            bc = b_s[rows, kcols]
            vc = v_s[rows, vcols]
            st = st_s[h]
            q3 = qc.reshape(NB, SUB, DK)
            k3 = kc.reshape(NB, SUB, DK)
            b3 = bc.reshape(NB, SUB, DK)
            bend = b3[:, SUB - 1:SUB, :]
            blast = bc[CHUNK - 1:CHUNK, :]
            qhat = (qc * jnp.exp(bc)).astype(BF16)
            o = _dot_nt(qhat, st.astype(BF16))
            kt = (k3 * jnp.exp(bend - b3)).reshape(CHUNK, DK).astype(BF16)
            qs = [(qc * jnp.exp(jnp.minimum(bc - bend[j], 0.0))).astype(BF16)
                  for j in range(NB - 1)]
            rm = _dot_nt(jnp.concatenate(qs, axis=0), kt)
            a = jnp.zeros((CHUNK, CHUNK), F32)
            for j in range(NB - 1):
                a = a + jnp.where(inter_masks[j], rm[j * CHUNK:(j + 1) * CHUNK], 0.0)
            ps = []
            for j in range(SUB):
                kj = jnp.broadcast_to(k3[:, j:j + 1, :], (NB, SUB, DK))
                bj = jnp.broadcast_to(b3[:, j:j + 1, :], (NB, SUB, DK))
                pj = q3 * kj * jnp.exp(jnp.minimum(b3 - bj, 0.0))
                ps.append(pj.reshape(CHUNK, DK).astype(BF16))
            dm = _dot(jnp.concatenate(ps, axis=1), emat)
            a = a + jnp.where(diag_mask, dm, 0.0)
            o = o + _dot(a.astype(BF16), vc)
            o_s[rows, vcols] = o
            khat = (kc * jnp.exp(blast - bc)).astype(BF16)
            st_s[h] = st * jnp.exp(blast) + _dot_tn(vc, khat)
        return carry

    lax.fori_loop(0, tm // CHUNK, chunk_body, 0)

    yb_in = _head_norm_gate(o_s[...], proj(C_OG, VDIM), gain_ref[...]).astype(BF16)
    yb = _dot(yb_in, wb_ref[...])
    m = _sigmoid(proj(C_GA, D_MODEL)) * ya + _sigmoid(proj(C_GBT, D_MODEL)) * yb
    xo_ref[0] = x + _dot(m.astype(BF16), wo_ref[...])

    @pl.when(t == nt - 1)
    def _fin():
        for h in range(HEADS):
            sout_ref[0, h] = st_s[h].T


def _const_spec(shape):
    nd = len(shape)
    return pl.BlockSpec(shape, lambda *_: (0,) * nd, pipeline_mode=pl.Buffered(1))


def _mixer_seq(x, cprev8, s0, lw, tm):
    bsz, T, _ = x.shape
    assert T % tm == 0 and tm % CHUNK == 0
    ridx = jnp.arange(tm)
    lmat = ((ridx[:, None] // CHUNK == ridx[None, :] // CHUNK)
            & (ridx[None, :] <= ridx[:, None])).astype(BF16)
    emat = (jnp.arange(SUB * DK)[:, None] // DK == jnp.arange(CHUNK)[None, :] % SUB).astype(BF16)
    consts = [lw["w_in"], lw["w_a"], lw["w_b"], lw["w_o"], lw["w_gk"],
              lw["norm_mix"], lw["conv_w"], lw["b_gk"], lw["gain"], lmat, emat]
    in_specs = [
        pl.BlockSpec((1, tm, D_MODEL), lambda b, t: (b, t, 0)),
        pl.BlockSpec((1, SUB, D_CONV), lambda b, t: (0, 0, 0)),
        pl.BlockSpec((1, HEADS, DK, DV), lambda b, t: (0, 0, 0, 0)),
    ] + [_const_spec(c.shape) for c in consts]
    out_shape = (
        jax.ShapeDtypeStruct((bsz, T, D_MODEL), F32),
        jax.ShapeDtypeStruct((bsz, SUB, D_CONV), F32),
        jax.ShapeDtypeStruct((bsz, HEADS, DK, DV), F32),
    )
    out_specs = (
        pl.BlockSpec((1, tm, D_MODEL), lambda b, t: (b, t, 0)),
        pl.BlockSpec((1, SUB, D_CONV), lambda b, t: (b, 0, 0)),
        pl.BlockSpec((1, HEADS, DK, DV), lambda b, t: (b, 0, 0, 0)),
    )
    scratch = [
        pltpu.VMEM((tm + SUB, D_CONV), F32),
        pltpu.VMEM((HEADS, DV, DK), F32),
        pltpu.VMEM((tm, KDIM), F32),
        pltpu.VMEM((tm, KDIM), F32),
        pltpu.VMEM((tm, KDIM), F32),
        pltpu.VMEM((tm, VDIM), BF16),
        pltpu.VMEM((tm, VDIM), F32),
    ]
    return pl.pallas_call(
        _mixer_seq_kernel,
        grid=(bsz, T // tm),
        in_specs=in_specs,
        out_specs=out_specs,
        out_shape=out_shape,
        scratch_shapes=scratch,
        compiler_params=pltpu.CompilerParams(
            dimension_semantics=("arbitrary", "arbitrary"),
            vmem_limit_bytes=VMEM_LIMIT),
        name="mixer_seq",
    )(x, cprev8, s0, *consts)


def _mixer_step_kernel(x_ref, p0_ref, p1_ref, s_ref, win_ref, wqt_ref, wkt_ref, wzt_ref,
                       wa_ref, wb_ref, wo_ref, wgkt_ref,
                       nmix_ref, convw_ref, bgkc_ref, gain_ref,
                       xo_ref, u_ref, sout_ref,
                       qt_s, kt_s, at_s, v_s, o_s, ya_s):
    i = pl.program_id(0)
    n = pl.num_programs(0)
    nseq = x_ref.shape[0]

    def xn_bf16():
        return _rms(x_ref[...], nmix_ref[...]).astype(BF16)

    @pl.when(i == 0)
    def _dense_in():
        xn = xn_bf16()

        def proj(c0, w):
            return _dot(xn, win_ref[:, c0:c0 + w])

        u = proj(C_GC, D_CONV) * proj(C_H, D_CONV)
        cw = convw_ref[...]
        yconv = cw[0:1] * p0_ref[...] + cw[1:2] * p1_ref[...] + cw[2:3] * u
        u_ref[...] = u
        ca = (proj(C_GB, D_CONV) * yconv).astype(BF16)
        ya_s[...] = _dot(ca, wa_ref[...])
        v_s[...] = proj(C_V, VDIM)
        qt_s[...] = _dot_nt(wqt_ref[...], xn) * (DK ** -0.5)
        kt_s[...] = _dot_nt(wkt_ref[...], xn)
        zt = _dot_nt(wzt_ref[...], xn).astype(BF16)
        gt = _log_sigmoid(_dot(wgkt_ref[...], zt) + bgkc_ref[...]) * (1.0 / GATE_NORMALIZER)
        at_s[...] = jnp.exp(gt)

    shift = (nseq - i * SUB) % nseq
    qt = pltpu.roll(qt_s[...], shift, 1)
    kt = pltpu.roll(kt_s[...], shift, 1)
    at = pltpu.roll(at_s[...], shift, 1)
    for j in range(SUB):
        row = pl.ds(i * SUB + j, 1)
        for h in range(HEADS):
            rk = slice(h * DK, (h + 1) * DK)
            cv = slice(h * DV, (h + 1) * DV)
            acol = at[rk, j:j + 1]
            kcol = kt[rk, j:j + 1]
            qcol = qt[rk, j:j + 1]
            vrow = v_s[row, cv]
            sn = s_ref[j, h] * acol + kcol * vrow
            sout_ref[j, h] = sn
            o_s[row, cv] = jnp.sum(sn * qcol, axis=0, keepdims=True)

    @pl.when(i == n - 1)
    def _dense_out():
        xn = xn_bf16()

        def proj(c0, w):
            return _dot(xn, win_ref[:, c0:c0 + w])

        yb_in = _head_norm_gate(o_s[...], proj(C_OG, VDIM), gain_ref[...]).astype(BF16)
        yb = _dot(yb_in, wb_ref[...])
        m = _sigmoid(proj(C_GA, D_MODEL)) * ya_s[...] + _sigmoid(proj(C_GBT, D_MODEL)) * yb
        xo_ref[...] = x_ref[...] + _dot(m.astype(BF16), wo_ref[...])


def _mixer_step(x, p0, p1, s_all, l, lw):
    nseq = x.shape[0]
    assert nseq == LANES
    consts = [lw["w_in"], lw["w_qt"], lw["w_kt"], lw["w_zt"], lw["w_a"], lw["w_b"], lw["w_o"],
              lw["w_gkt"], lw["norm_mix"], lw["conv_w"], lw["b_gk_col"], lw["gain"]]
    in_specs = [
        _const_spec(x.shape),
        _const_spec(x.shape),
        _const_spec(x.shape),
        pl.BlockSpec((None, SUB, HEADS, DK, DV), lambda i: (l, i, 0, 0, 0)),
    ] + [_const_spec(c.shape) for c in consts]
    out_shape = (
        jax.ShapeDtypeStruct(x.shape, F32),
        jax.ShapeDtypeStruct(x.shape, F32),
        jax.ShapeDtypeStruct(s_all.shape[1:], F32),
    )
    out_specs = (
        pl.BlockSpec(x.shape, lambda i: (0, 0)),
        pl.BlockSpec(x.shape, lambda i: (0, 0)),
        pl.BlockSpec((SUB, HEADS, DK, DV), lambda i: (i, 0, 0, 0)),
    )
    scratch = [
        pltpu.VMEM((KDIM, nseq), F32),
        pltpu.VMEM((KDIM, nseq), F32),
        pltpu.VMEM((KDIM, nseq), F32),
        pltpu.VMEM((nseq, VDIM), F32),
        pltpu.VMEM((nseq, VDIM), F32),
        pltpu.VMEM((nseq, D_MODEL), F32),
    ]
    return pl.pallas_call(
        _mixer_step_kernel,
        grid=(nseq // SUB,),
        in_specs=in_specs,
        out_specs=out_specs,
        out_shape=out_shape,
        scratch_shapes=scratch,
        compiler_params=pltpu.CompilerParams(
            dimension_semantics=("arbitrary",),
            vmem_limit_bytes=VMEM_LIMIT),
        name="mixer_step",
    )(x, p0, p1, s_all, *consts)


def _ffn_kernel(x_ref, nffn_ref, wgu_ref, wdn_ref, fin_ref, o_ref, *, final):
    x = x_ref[...]
    xn = _rms(x, nffn_ref[...]).astype(BF16)
    g = _dot(xn, wgu_ref[:, 0:D_FF])
    u = _dot(xn, wgu_ref[:, D_FF:2 * D_FF])
    hid = (g * _sigmoid(g) * u).astype(BF16)
    y = x + _dot(hid, wdn_ref[...])
    if final:
        y = _rms(y, fin_ref[...])
    o_ref[...] = y


def _ffn(x, lw, final_norm, final, tm):
    n = x.shape[0]
    assert n % tm == 0
    consts = [lw["norm_ffn"], lw["w_gu"], lw["w_down"], final_norm]
    return pl.pallas_call(
        functools.partial(_ffn_kernel, final=final),
        grid=(n // tm,),
        in_specs=[pl.BlockSpec((tm, D_MODEL), lambda i: (i, 0))]
        + [_const_spec(c.shape) for c in consts],
        out_specs=pl.BlockSpec((tm, D_MODEL), lambda i: (i, 0)),
        out_shape=jax.ShapeDtypeStruct((n, D_MODEL), F32),
        compiler_params=pltpu.CompilerParams(
            dimension_semantics=("arbitrary",),
            vmem_limit_bytes=VMEM_LIMIT),
        name="ffn",
    )(x, *consts)


def _prep_layer(l, w_in, conv_w, w_gk2, b_gk2, gla_gain, w_a_out, w_b_out, w_o,
                norm_mix, norm_ffn, w_gu, w_down):
    wi = w_in[l]
    z_lo = 3 * D_CONV + 2 * KDIM + 2 * VDIM
    z_hi = z_lo + GATE_RANK
    w_z = jnp.pad(wi[:, z_lo:z_hi], ((0, 0), (0, LANES - GATE_RANK)))
    w_pack = jnp.concatenate([wi[:, :z_lo], wi[:, z_hi:], w_z], axis=1).astype(BF16)
    w_gk = jnp.pad(w_gk2[l], ((0, LANES - GATE_RANK), (0, 0))).astype(BF16)
    return {
        "w_in": w_pack,
        "w_qt": w_pack[:, C_Q:C_Q + KDIM].T,
        "w_kt": w_pack[:, C_K:C_K + KDIM].T,
        "w_zt": w_pack[:, C_Z:C_Z + LANES].T,
        "w_gk": w_gk,
        "w_gkt": w_gk.T,
        "b_gk": b_gk2[l].reshape(1, KDIM),
        "b_gk_col": b_gk2[l].reshape(KDIM, 1),
        "gain": gla_gain[l].reshape(1, DV),
        "conv_w": conv_w[l],
        "w_a": w_a_out[l].astype(BF16),
        "w_b": w_b_out[l].astype(BF16),
        "w_o": w_o[l].astype(BF16),
        "norm_mix": norm_mix[l].reshape(1, D_MODEL),
        "norm_ffn": norm_ffn[l].reshape(1, D_MODEL),
        "w_gu": w_gu[l].astype(BF16),
        "w_down": w_down[l].astype(BF16),
    }


def kernel(x_prompt, x_sample, state_conv, state_gla, meta_tokens, w_in, conv_w, w_gk2, b_gk2,
           gla_gain, w_a_out, w_b_out, w_o, norm_mix, norm_ffn, w_gu, w_down, final_norm):
    depth = w_in.shape[0]
    bsz, seq, _ = x_prompt.shape
    nsmp = x_sample.shape[0]
    fin = final_norm.reshape(1, D_MODEL)
    layers = [_prep_layer(l, w_in, conv_w, w_gk2, b_gk2, gla_gain, w_a_out, w_b_out, w_o,
                          norm_mix, norm_ffn, w_gu, w_down) for l in range(depth)]

    xm = jnp.pad(meta_tokens.astype(F32), ((CHUNK - N_META, 0), (0, 0)))[None]
    xp = x_prompt
    xs = x_sample.reshape(nsmp, D_MODEL)
    zero_conv = jnp.zeros((1, SUB, D_CONV), F32)
    zero_state = jnp.zeros((1, HEADS, DK, DV), F32)

    p_conv, p_gla, s_conv, s_gla = [], [], [], []
    for l, lw in enumerate(layers):
        last = l == depth - 1
        xm, m_conv, m_state = _mixer_seq(xm, zero_conv, zero_state, lw, CHUNK)
        xp, pc, ps = _mixer_seq(xp, m_conv, m_state, lw, 256)
        p1 = state_conv[l, :, 1, :]
        xs, us, ss = _mixer_step(xs, state_conv[l, :, 0, :], p1, state_gla, l, lw)
        sc = jnp.stack([p1, us], axis=1)
        if not last:
            xm = _ffn(xm[0], lw, fin, False, CHUNK)[None]
        xp = _ffn(xp.reshape(bsz * seq, D_MODEL), lw, fin, last, 512).reshape(bsz, seq, D_MODEL)
        xs = _ffn(xs, lw, fin, last, nsmp)
        p_conv.append(pc[:, SUB - (CONV_W - 1):, :])
        p_gla.append(ps)
        s_conv.append(sc)
        s_gla.append(ss)

    return (xp, xs.reshape(nsmp, 1, D_MODEL), jnp.stack(p_conv), jnp.stack(p_gla),
            jnp.stack(s_conv), jnp.stack(s_gla))
```

```python
import functools

import jax
import jax.numpy as jnp
from jax import lax
from jax.experimental import pallas as pl
from jax.experimental.pallas import tpu as pltpu

F32 = jnp.float32
BF16 = jnp.bfloat16

D_MODEL = 1024
N_META = 16
D_CONV = D_MODEL
CONV_W = 3
HEADS = 4
DK = 128
DV = 256
KDIM = HEADS * DK
VDIM = HEADS * DV
GATE_RANK = 16
GATE_NORMALIZER = 16.0
D_FF = 2816
EPS = 1e-6

CHUNK = 64
SUB = 8
NB = CHUNK // SUB
LANES = 128
COLB = 256
VMEM_LIMIT = 60 * 1024 * 1024

C_GB, C_GC, C_H = 0, 1024, 2048
C_Q, C_K, C_V, C_OG = 3072, 3584, 4096, 5120
D_MAIN = 6144
Z_LO = D_MAIN
Z_HI = Z_LO + GATE_RANK


def _rms(x, g):
    ms = jnp.mean(x * x, axis=-1, keepdims=True)
    return x * lax.rsqrt(ms + EPS) * g


def _log_sigmoid(x):
    return jnp.minimum(x, 0.0) - jnp.log1p(jnp.exp(-jnp.abs(x)))


def _sigmoid(x):
    return 1.0 / (1.0 + jnp.exp(-x))


def _dot(a, b):
    return jnp.dot(a, b, preferred_element_type=F32)


def _dot_nt(a, b):
    return lax.dot_general(a, b, (((1,), (1,)), ((), ())), preferred_element_type=F32)


def _dot_tn(a, b):
    return lax.dot_general(a, b, (((0,), (0,)), ((), ())), preferred_element_type=F32)


def _split3(x):
    x1 = x.astype(BF16)
    r1 = x - x1.astype(F32)
    x2 = r1.astype(BF16)
    r2 = r1 - x2.astype(F32)
    return x1, x2, r2.astype(BF16)


def _head_norm(o, gain):
    return jnp.concatenate(
        [_rms(o[:, h * DV:(h + 1) * DV], gain) for h in range(HEADS)], axis=1)


def _mixer_seq_kernel(x_ref, cprev_ref, s0_ref, wm_ref, wg_ref, wz_ref, wa_ref, wb_ref, wo_ref,
                      wgk_ref, nmix_ref, convw_ref, bgk_ref, gain_ref, lmat_ref, emat_ref,
                      xo_ref, cnew_ref, sout_ref,
                      xn_s, ubuf, st_s, q_s, k_s, b_s, v_s, o_s, ca_s, ya_s, og_s, ga_s, gb_s):
    t = pl.program_id(1)
    nt = pl.num_programs(1)
    tm = x_ref.shape[1]

    @pl.when(t == 0)
    def _init():
        ubuf[0:SUB, :] = cprev_ref[0]
        for h in range(HEADS):
            st_s[h] = s0_ref[0, h].T

    xn_s[...] = _rms(x_ref[0], nmix_ref[...]).astype(BF16)

    def proj(w_ref, c0, n):
        return _dot(xn_s[...], w_ref[:, c0:c0 + n])

    q_s[...] = proj(wm_ref, C_Q, KDIM) * (DK ** -0.5)
    k_s[...] = proj(wm_ref, C_K, KDIM)
    v_s[...] = proj(wm_ref, C_V, VDIM).astype(BF16)
    z = proj(wz_ref, 0, LANES).astype(BF16)
    g = _log_sigmoid(_dot(z, wgk_ref[...]) + bgk_ref[...]) * (1.0 / GATE_NORMALIZER)
    g1, g2, g3 = _split3(g)
    lmat = lmat_ref[...]
    b_s[...] = _dot(lmat, g1) + _dot(lmat, g2) + _dot(lmat, g3)

    def cols_of(cb):
        return slice(cb * COLB, (cb + 1) * COLB)

    def task_u(cb):
        u = proj(wm_ref, C_GC + cb * COLB, COLB) * proj(wm_ref, C_H + cb * COLB, COLB)
        ubuf[SUB:SUB + tm, cols_of(cb)] = u

    def task_conv(cb):
        cols = cols_of(cb)
        cw = convw_ref[:, cols]
        yconv = (cw[0:1] * ubuf[SUB - 2:SUB - 2 + tm, cols]
                 + cw[1:2] * ubuf[SUB - 1:SUB - 1 + tm, cols]
                 + cw[2:3] * ubuf[SUB:SUB + tm, cols])
        ca_s[:, cols] = (proj(wm_ref, C_GB + cb * COLB, COLB) * yconv).astype(BF16)
        tail = ubuf[tm:tm + SUB, cols]
        ubuf[0:SUB, cols] = tail
        cnew_ref[0, :, cols] = tail

    def task_ya(cb):
        ya_s[:, cols_of(cb)] = _dot(ca_s[...], wa_ref[:, cols_of(cb)])

    def task_og(cb):
        og = proj(wm_ref, C_OG + cb * COLB, COLB)
        og_s[:, cols_of(cb)] = og * _sigmoid(og)

    def task_ga(cb):
        ga_s[:, cols_of(cb)] = _sigmoid(proj(wg_ref, cb * COLB, COLB))

    def task_gb(cb):
        gb_s[:, cols_of(cb)] = _sigmoid(proj(wg_ref, D_MODEL + cb * COLB, COLB))

    ncb = D_MODEL // COLB
    tasks = []
    for cb in range(ncb):
        tasks += [functools.partial(task_u, cb), functools.partial(task_conv, cb)]
    for fn in (task_ya, task_og, task_ga, task_gb):
        tasks += [functools.partial(fn, cb) for cb in range(ncb)]

    ti = lax.broadcasted_iota(jnp.int32, (CHUNK, CHUNK), 0)
    si = lax.broadcasted_iota(jnp.int32, (CHUNK, CHUNK), 1)
    diag_mask = ((si >> 3) == (ti >> 3)) & ((si & 7) <= (ti & 7))
    lane_blk = lax.broadcasted_iota(jnp.int32, (SUB, CHUNK), 1) >> 3
    emat = emat_ref[...]
    stack_off = [sum(CHUNK - SUB * (i + 1) for i in range(j)) for j in range(NB)]

    def gla_diag(c, h):
        rows = slice(c * CHUNK, (c + 1) * CHUNK)
        kcols = slice(h * DK, (h + 1) * DK)
        q3 = q_s[rows, kcols].reshape(NB, SUB, DK)
        k3 = k_s[rows, kcols].reshape(NB, SUB, DK)
        b3 = b_s[rows, kcols].reshape(NB, SUB, DK)
        ps = []
        for j in range(SUB):
            kj = jnp.broadcast_to(k3[:, j:j + 1, :], (NB, SUB, DK))
            bj = jnp.broadcast_to(b3[:, j:j + 1, :], (NB, SUB, DK))
            pj = q3 * kj * jnp.exp(jnp.minimum(b3 - bj, 0.0))
            ps.append(pj.reshape(CHUNK, DK).astype(BF16))
        return jnp.concatenate(ps, axis=1)

    def gla_rest(c, h, dm):
        rows = slice(c * CHUNK, (c + 1) * CHUNK)
        kcols = slice(h * DK, (h + 1) * DK)
        vcols = slice(h * DV, (h + 1) * DV)
        qc = q_s[rows, kcols]
        kc = k_s[rows, kcols]
        bc = b_s[rows, kcols]
        vc = v_s[rows, vcols]
        st = st_s[h]
        k3 = kc.reshape(NB, SUB, DK)
        b3 = bc.reshape(NB, SUB, DK)
        bend = b3[:, SUB - 1:SUB, :]
        blast = bc[CHUNK - 1:CHUNK, :]
        qhat = (qc * jnp.exp(bc)).astype(BF16)
        o = _dot_nt(qhat, st.astype(BF16))
        kt = (k3 * jnp.exp(bend - b3)).reshape(CHUNK, DK).astype(BF16)
        qs = [qc[SUB * (j + 1):] * jnp.exp(bc[SUB * (j + 1):] - bend[j]) for j in range(NB - 1)]
        rm = _dot_nt(jnp.concatenate(qs, axis=0).astype(BF16), kt)
        arows = [jnp.zeros((SUB, CHUNK), F32)]
        for i in range(1, NB):
            acc = jnp.zeros((SUB, CHUNK), F32)
            for j in range(i):
                r0 = stack_off[j] + SUB * (i - j - 1)
                acc = jnp.where(lane_blk == j, rm[r0:r0 + SUB], acc)
            arows.append(acc)
        a = jnp.concatenate(arows, axis=0) + jnp.where(diag_mask, dm, 0.0)
        o_s[rows, vcols] = o + _dot(a.astype(BF16), vc)
        khat = (kc * jnp.exp(blast - bc)).astype(BF16)
        st_s[h] = st * jnp.exp(blast) + _dot_tn(vc, khat)

    nslots = (tm // CHUNK) * 2 * HEADS
    done = [0]

    def run_tasks(slot):
        upto = ((slot + 1) * len(tasks) + nslots - 1) // nslots
        while done[0] < upto:
            tasks[done[0]]()
            done[0] += 1

    slot = 0
    for c in range(tm // CHUNK):
        pcs = []
        for h in range(HEADS):
            pcs.append(gla_diag(c, h))
            run_tasks(slot)
            slot += 1
        dm_all = _dot(jnp.concatenate(pcs, axis=0), emat)
        for h in range(HEADS):
            gla_rest(c, h, dm_all[h * CHUNK:(h + 1) * CHUNK])
            run_tasks(slot)
            slot += 1

    yb_in = (_head_norm(o_s[...], gain_ref[...]) * og_s[...]).astype(BF16)
    yb = _dot(yb_in, wb_ref[...])
    m = ga_s[...] * ya_s[...] + gb_s[...] * yb
    xo_ref[0] = x_ref[0] + _dot(m.astype(BF16), wo_ref[...])

    @pl.when(t == nt - 1)
    def _fin():
        for h in range(HEADS):
            sout_ref[0, h] = st_s[h].T


def _const_spec(shape):
    nd = len(shape)
    return pl.BlockSpec(shape, lambda *_: (0,) * nd, pipeline_mode=pl.Buffered(1))


def _layer_spec(arr, l):
    nd = arr.ndim
    return pl.BlockSpec((None,) + arr.shape[1:], lambda *_: (l,) + (0,) * (nd - 1),
                        pipeline_mode=pl.Buffered(1))


def _mixer_seq(x, cprev8, s0, W, l, tm):
    bsz, T, _ = x.shape
    assert T % tm == 0 and tm % CHUNK == 0
    ridx = jnp.arange(tm)
    lmat = ((ridx[:, None] // CHUNK == ridx[None, :] // CHUNK)
            & (ridx[None, :] <= ridx[:, None])).astype(BF16)
    emat = (jnp.arange(SUB * DK)[:, None] // DK == jnp.arange(CHUNK)[None, :] % SUB).astype(BF16)
    params = [W["w_main"], W["w_gate"], W["w_z"], W["w_a"], W["w_b"], W["w_o"], W["w_gk"],
              W["norm_mix"], W["conv_w"], W["b_gk"], W["gain"]]
    in_specs = [
        pl.BlockSpec((1, tm, D_MODEL), lambda b, t: (b, t, 0)),
        pl.BlockSpec((1, SUB, D_CONV), lambda b, t: (0, 0, 0)),
        pl.BlockSpec((1, HEADS, DK, DV), lambda b, t: (0, 0, 0, 0)),
    ] + [_layer_spec(p, l) for p in params] + [_const_spec(lmat.shape), _const_spec(emat.shape)]
    out_shape = (
        jax.ShapeDtypeStruct((bsz, T, D_MODEL), F32),
        jax.ShapeDtypeStruct((bsz, SUB, D_CONV), F32),
        jax.ShapeDtypeStruct((bsz, HEADS, DK, DV), F32),
    )
    out_specs = (
        pl.BlockSpec((1, tm, D_MODEL), lambda b, t: (b, t, 0)),
        pl.BlockSpec((1, SUB, D_CONV), lambda b, t: (b, 0, 0)),
        pl.BlockSpec((1, HEADS, DK, DV), lambda b, t: (b, 0, 0, 0)),
    )
    scratch = [
        pltpu.VMEM((tm, D_MODEL), BF16),
        pltpu.VMEM((tm + SUB, D_CONV), F32),
        pltpu.VMEM((HEADS, DV, DK), F32),
        pltpu.VMEM((tm, KDIM), F32),
        pltpu.VMEM((tm, KDIM), F32),
        pltpu.VMEM((tm, KDIM), F32),
        pltpu.VMEM((tm, VDIM), BF16),
        pltpu.VMEM((tm, VDIM), F32),
        pltpu.VMEM((tm, D_CONV), BF16),
        pltpu.VMEM((tm, D_MODEL), F32),
        pltpu.VMEM((tm, VDIM), F32),
        pltpu.VMEM((tm, D_MODEL), F32),
        pltpu.VMEM((tm, D_MODEL), F32),
    ]
    return pl.pallas_call(
        _mixer_seq_kernel,
        grid=(bsz, T // tm),
        in_specs=in_specs,
        out_specs=out_specs,
        out_shape=out_shape,
        scratch_shapes=scratch,
        compiler_params=pltpu.CompilerParams(
            dimension_semantics=("arbitrary", "arbitrary"),
            vmem_limit_bytes=VMEM_LIMIT),
        name="mixer_seq",
    )(x, cprev8, s0, *params, lmat, emat)


def _mixer_step_kernel(x_ref, cp_ref, s_ref, wm_ref, wg_ref, wqt_ref, wkt_ref, wzt_ref,
                       wa_ref, wb_ref, wo_ref, wgkt_ref,
                       nmix_ref, convw_ref, bgkc_ref, gain_ref, *rest):
    xo_ref, cnew_ref, sout_ref, qt_s, kt_s, at_s, v_s, o_s, ya_s = rest[-9:]
    i = pl.program_id(0)
    n = pl.num_programs(0)
    nseq = x_ref.shape[0]

    def xn_bf16():
        return _rms(x_ref[...], nmix_ref[...]).astype(BF16)

    @pl.when(i == 0)
    def _dense_in():
        xn = xn_bf16()
        u = _dot(xn, wm_ref[:, C_GC:C_GC + D_CONV]) * _dot(xn, wm_ref[:, C_H:C_H + D_CONV])
        cw = convw_ref[...]
        p1 = cp_ref[:, D_CONV:2 * D_CONV]
        yconv = cw[0:1] * cp_ref[:, 0:D_CONV] + cw[1:2] * p1 + cw[2:3] * u
        cnew_ref[:, 0:D_CONV] = p1
        cnew_ref[:, D_CONV:2 * D_CONV] = u
        ca = (_dot(xn, wm_ref[:, C_GB:C_GB + D_CONV]) * yconv).astype(BF16)
        ya_s[...] = _dot(ca, wa_ref[...])
        v_s[...] = _dot(xn, wm_ref[:, C_V:C_V + VDIM])
        qt_s[...] = _dot_nt(wqt_ref[...], xn) * (DK ** -0.5)
        kt_s[...] = _dot_nt(wkt_ref[...], xn)
        zt = _dot_nt(wzt_ref[...], xn).astype(BF16)
        gt = _log_sigmoid(_dot(wgkt_ref[...], zt) + bgkc_ref[...]) * (1.0 / GATE_NORMALIZER)
        at_s[...] = jnp.exp(gt)

    shift = (nseq - i * SUB) % nseq
    qt = pltpu.roll(qt_s[...], shift, 1)
    kt = pltpu.roll(kt_s[...], shift, 1)
    at = pltpu.roll(at_s[...], shift, 1)
    for j in range(SUB):
        row = pl.ds(i * SUB + j, 1)
        for h in range(HEADS):
            rk = slice(h * DK, (h + 1) * DK)
            cv = slice(h * DV, (h + 1) * DV)
            sn = s_ref[j, h] * at[rk, j:j + 1] + kt[rk, j:j + 1] * v_s[row, cv]
            sout_ref[j, h] = sn
            o_s[row, cv] = jnp.sum(sn * qt[rk, j:j + 1], axis=0, keepdims=True)

    @pl.when(i == n - 1)
    def _dense_out():
        xn = xn_bf16()
        og = _dot(xn, wm_ref[:, C_OG:C_OG + VDIM])
        yb_in = (_head_norm(o_s[...], gain_ref[...]) * (og * _sigmoid(og))).astype(BF16)
        yb = _dot(yb_in, wb_ref[...])
        m = (_sigmoid(_dot(xn, wg_ref[:, 0:D_MODEL])) * ya_s[...]
             + _sigmoid(_dot(xn, wg_ref[:, D_MODEL:2 * D_MODEL])) * yb)
        xo_ref[...] = x_ref[...] + _dot(m.astype(BF16), wo_ref[...])


def _mixer_step(x, conv_all, s_all, s_out_prev, W, l):
    nseq = x.shape[0]
    assert nseq == LANES
    params = [W["w_main"], W["w_gate"], W["w_qt"], W["w_kt"], W["w_zt"], W["w_a"], W["w_b"],
              W["w_o"], W["w_gkt"], W["norm_mix"], W["conv_w"], W["b_gk_col"], W["gain"]]
    state_blk = (None, SUB, HEADS, DK, DV)
    in_specs = [
        _const_spec(x.shape),
        _layer_spec(conv_all, l),
        pl.BlockSpec(state_blk, lambda i: (l, i, 0, 0, 0)),
    ] + [_layer_spec(p, l) for p in params]
    args = [x, conv_all, s_all, *params]
    aliases = {}
    if s_out_prev is not None:
        in_specs.append(pl.BlockSpec(memory_space=pl.ANY))
        aliases = {len(args): 2}
        args.append(s_out_prev)
    out_shape = (
        jax.ShapeDtypeStruct(x.shape, F32),
        jax.ShapeDtypeStruct((nseq, 2 * D_CONV), F32),
        jax.ShapeDtypeStruct(s_all.shape, F32),
    )
    out_specs = (
        pl.BlockSpec(x.shape, lambda i: (0, 0)),
        pl.BlockSpec((nseq, 2 * D_CONV), lambda i: (0, 0)),
        pl.BlockSpec(state_blk, lambda i: (l, i, 0, 0, 0)),
    )
    scratch = [
        pltpu.VMEM((KDIM, nseq), F32),
        pltpu.VMEM((KDIM, nseq), F32),
        pltpu.VMEM((KDIM, nseq), F32),
        pltpu.VMEM((nseq, VDIM), F32),
        pltpu.VMEM((nseq, VDIM), F32),
        pltpu.VMEM((nseq, D_MODEL), F32),
    ]
    return pl.pallas_call(
        _mixer_step_kernel,
        grid=(nseq // SUB,),
        in_specs=in_specs,
        out_specs=out_specs,
        out_shape=out_shape,
        scratch_shapes=scratch,
        input_output_aliases=aliases,
        compiler_params=pltpu.CompilerParams(
            dimension_semantics=("arbitrary",),
            vmem_limit_bytes=VMEM_LIMIT),
        name="mixer_step",
    )(*args)


def _ffn_kernel(x_ref, nffn_ref, wgu_ref, wdn_ref, fin_ref, o_ref, *, final):
    x = x_ref[...]
    xn = _rms(x, nffn_ref[...]).astype(BF16)
    g = _dot(xn, wgu_ref[:, 0:D_FF])
    u = _dot(xn, wgu_ref[:, D_FF:2 * D_FF])
    hid = (g * _sigmoid(g) * u).astype(BF16)
    y = x + _dot(hid, wdn_ref[...])
    if final:
        y = _rms(y, fin_ref[...])
    o_ref[...] = y


def _ffn(x, W, l, final, tm):
    n = x.shape[0]
    assert n % tm == 0
    params = [W["norm_ffn"], W["w_gu"], W["w_down"]]
    fin = W["final_norm"]
    return pl.pallas_call(
        functools.partial(_ffn_kernel, final=final),
        grid=(n // tm,),
        in_specs=[pl.BlockSpec((tm, D_MODEL), lambda i: (i, 0))]
        + [_layer_spec(p, l) for p in params] + [_const_spec(fin.shape)],
        out_specs=pl.BlockSpec((tm, D_MODEL), lambda i: (i, 0)),
        out_shape=jax.ShapeDtypeStruct((n, D_MODEL), F32),
        compiler_params=pltpu.CompilerParams(
            dimension_semantics=("arbitrary",),
            vmem_limit_bytes=VMEM_LIMIT),
        name="ffn",
    )(x, *params, fin)


def _prep_weights(w_in, conv_w, w_gk2, b_gk2, gla_gain, w_a_out, w_b_out, w_o,
                  norm_mix, norm_ffn, w_gu, w_down, final_norm):
    depth = w_in.shape[0]
    w_z = jnp.pad(w_in[:, :, Z_LO:Z_HI], ((0, 0), (0, 0), (0, LANES - GATE_RANK))).astype(BF16)
    w_gk = jnp.pad(w_gk2, ((0, 0), (0, LANES - GATE_RANK), (0, 0))).astype(BF16)
    return {
        "w_main": w_in[:, :, :D_MAIN].astype(BF16),
        "w_gate": w_in[:, :, Z_HI:].astype(BF16),
        "w_z": w_z,
        "w_qt": jnp.swapaxes(w_in[:, :, C_Q:C_Q + KDIM], 1, 2).astype(BF16),
        "w_kt": jnp.swapaxes(w_in[:, :, C_K:C_K + KDIM], 1, 2).astype(BF16),
        "w_zt": jnp.swapaxes(w_z, 1, 2),
        "w_gk": w_gk,
        "w_gkt": jnp.swapaxes(w_gk, 1, 2),
        "b_gk": b_gk2.reshape(depth, 1, KDIM),
        "b_gk_col": b_gk2.reshape(depth, KDIM, 1),
        "gain": gla_gain.reshape(depth, 1, DV),
        "conv_w": conv_w,
        "w_a": w_a_out.astype(BF16),
        "w_b": w_b_out.astype(BF16),
        "w_o": w_o.astype(BF16),
        "norm_mix": norm_mix.reshape(depth, 1, D_MODEL),
        "norm_ffn": norm_ffn.reshape(depth, 1, D_MODEL),
        "w_gu": w_gu.astype(BF16),
        "w_down": w_down.astype(BF16),
        "final_norm": final_norm.reshape(1, D_MODEL),
    }


def kernel(x_prompt, x_sample, state_conv, state_gla, meta_tokens, w_in, conv_w, w_gk2, b_gk2,
           gla_gain, w_a_out, w_b_out, w_o, norm_mix, norm_ffn, w_gu, w_down, final_norm):
    depth = w_in.shape[0]
    bsz, seq, _ = x_prompt.shape
    nsmp = x_sample.shape[0]
    W = _prep_weights(w_in, conv_w, w_gk2, b_gk2, gla_gain, w_a_out, w_b_out, w_o,
                      norm_mix, norm_ffn, w_gu, w_down, final_norm)

    xm = jnp.pad(meta_tokens.astype(F32), ((CHUNK - N_META, 0), (0, 0)))[None]
    xp = x_prompt
    xs = x_sample.reshape(nsmp, D_MODEL)
    conv_all = state_conv.reshape(depth, nsmp, (CONV_W - 1) * D_CONV)
    zero_conv = jnp.zeros((1, SUB, D_CONV), F32)
    zero_state = jnp.zeros((1, HEADS, DK, DV), F32)

    p_conv, p_gla, s_conv = [], [], []
    s_gla = None
    for l in range(depth):
        last = l == depth - 1
        xm, m_conv, m_state = _mixer_seq(xm, zero_conv, zero_state, W, l, CHUNK)
        xp, pc, ps = _mixer_seq(xp, m_conv, m_state, W, l, 256)
        xs, sc, s_gla = _mixer_step(xs, conv_all, state_gla, s_gla, W, l)
        if not last:
            xm = _ffn(xm[0], W, l, False, CHUNK)[None]
        xp = _ffn(xp.reshape(bsz * seq, D_MODEL), W, l, last, 512).reshape(bsz, seq, D_MODEL)
        xs = _ffn(xs, W, l, last, nsmp)
        p_conv.append(pc[:, SUB - (CONV_W - 1):, :])
        p_gla.append(ps)
        s_conv.append(sc.reshape(nsmp, CONV_W - 1, D_CONV))

    return (xp, xs.reshape(nsmp, 1, D_MODEL), jnp.stack(p_conv), jnp.stack(p_gla),
            jnp.stack(s_conv), s_gla)
```

```python
import functools

import jax
import jax.numpy as jnp
from jax import lax
from jax.experimental import pallas as pl
from jax.experimental.pallas import tpu as pltpu

F32 = jnp.float32
BF16 = jnp.bfloat16

D_MODEL = 1024
N_META = 16
D_CONV = D_MODEL
CONV_W = 3
HEADS = 4
DK = 128
DV = 256
KDIM = HEADS * DK
VDIM = HEADS * DV
GATE_RANK = 16
GATE_NORMALIZER = 16.0
D_FF = 2816
EPS = 1e-6

CHUNK = 64
SUB = 8
NB = CHUNK // SUB
LANES = 128
COLB = 512
VMEM_LIMIT = 60 * 1024 * 1024

C_GB, C_GC, C_H = 0, 1024, 2048
C_Q, C_K, C_V, C_OG = 3072, 3584, 4096, 5120
D_MAIN = 6144
Z_LO = D_MAIN
Z_HI = Z_LO + GATE_RANK


def _rms(x, g):
    ms = jnp.mean(x * x, axis=-1, keepdims=True)
    return x * lax.rsqrt(ms + EPS) * g


def _log_sigmoid(x):
    return jnp.minimum(x, 0.0) - jnp.log1p(jnp.exp(-jnp.abs(x)))


def _sigmoid(x):
    return 1.0 / (1.0 + jnp.exp(-x))


def _dot(a, b):
    return jnp.dot(a, b, preferred_element_type=F32)


def _dot_nt(a, b):
    return lax.dot_general(a, b, (((1,), (1,)), ((), ())), preferred_element_type=F32)


def _dot_tn(a, b):
    return lax.dot_general(a, b, (((0,), (0,)), ((), ())), preferred_element_type=F32)


def _split3(x):
    x1 = x.astype(BF16)
    r1 = x - x1.astype(F32)
    x2 = r1.astype(BF16)
    r2 = r1 - x2.astype(F32)
    return x1, x2, r2.astype(BF16)


def _head_norm(o, gain):
    return jnp.concatenate(
        [_rms(o[:, h * DV:(h + 1) * DV], gain) for h in range(HEADS)], axis=1)


def _mixer_seq_kernel(x_ref, cprev_ref, s0_ref, wm_ref, wg_ref, wz_ref, wa_ref, wb_ref, wo_ref,
                      wgk_ref, nmix_ref, convw_ref, bgk_ref, gain_ref, lmat_ref, emat_ref,
                      xo_ref, cnew_ref, sout_ref,
                      xn_s, ubuf, st_s, q_s, k_s, b_s, v_s, o_s, ca_s, ya_s, og_s, ga_s, gb_s):
    t = pl.program_id(1)
    nt = pl.num_programs(1)
    tm = x_ref.shape[1]

    @pl.when(t == 0)
    def _init():
        ubuf[0:SUB, :] = cprev_ref[0]
        for h in range(HEADS):
            st_s[h] = s0_ref[0, h].T

    xn_s[...] = _rms(x_ref[0], nmix_ref[...]).astype(BF16)

    def proj(w_ref, c0, n):
        return _dot(xn_s[...], w_ref[:, c0:c0 + n])

    z = proj(wz_ref, 0, LANES).astype(BF16)
    g = _log_sigmoid(_dot(z, wgk_ref[...]) + bgk_ref[...]) * (1.0 / GATE_NORMALIZER)
    g1, g2, g3 = _split3(g)
    q_s[...] = proj(wm_ref, C_Q, KDIM) * (DK ** -0.5)
    k_s[...] = proj(wm_ref, C_K, KDIM)
    v_s[...] = proj(wm_ref, C_V, VDIM).astype(BF16)
    lmat = lmat_ref[...]
    b_s[...] = _dot(lmat, g1) + _dot(lmat, g2) + _dot(lmat, g3)

    def cols_of(cb):
        return slice(cb * COLB, (cb + 1) * COLB)

    def task_u(cb):
        u = proj(wm_ref, C_GC + cb * COLB, COLB) * proj(wm_ref, C_H + cb * COLB, COLB)
        ubuf[SUB:SUB + tm, cols_of(cb)] = u

    def task_conv(cb):
        cols = cols_of(cb)
        cw = convw_ref[:, cols]
        yconv = (cw[0:1] * ubuf[SUB - 2:SUB - 2 + tm, cols]
                 + cw[1:2] * ubuf[SUB - 1:SUB - 1 + tm, cols]
                 + cw[2:3] * ubuf[SUB:SUB + tm, cols])
        ca_s[:, cols] = (proj(wm_ref, C_GB + cb * COLB, COLB) * yconv).astype(BF16)
        tail = ubuf[tm:tm + SUB, cols]
        ubuf[0:SUB, cols] = tail
        cnew_ref[0, :, cols] = tail

    def task_ya(cb):
        ya_s[:, cols_of(cb)] = _dot(ca_s[...], wa_ref[:, cols_of(cb)])

    def task_og(cb):
        og = proj(wm_ref, C_OG + cb * COLB, COLB)
        og_s[:, cols_of(cb)] = og * _sigmoid(og)

    def task_ga(cb):
        ga_s[:, cols_of(cb)] = _sigmoid(proj(wg_ref, cb * COLB, COLB))

    def task_gb(cb):
        gb_s[:, cols_of(cb)] = _sigmoid(proj(wg_ref, D_MODEL + cb * COLB, COLB))

    ncb = D_MODEL // COLB
    tasks = []
    for cb in range(ncb):
        tasks += [functools.partial(task_u, cb), functools.partial(task_conv, cb)]
    for fn in (task_ya, task_og, task_ga, task_gb):
        tasks += [functools.partial(fn, cb) for cb in range(ncb)]

    ti = lax.broadcasted_iota(jnp.int32, (CHUNK, CHUNK), 0)
    si = lax.broadcasted_iota(jnp.int32, (CHUNK, CHUNK), 1)
    diag_mask = ((si >> 3) == (ti >> 3)) & ((si & 7) <= (ti & 7))
    lane_blk = lax.broadcasted_iota(jnp.int32, (SUB, CHUNK), 1) >> 3
    emat = emat_ref[...]
    stack_off = [sum(CHUNK - SUB * (i + 1) for i in range(j)) for j in range(NB)]

    def gla_diag(c, h):
        rows = slice(c * CHUNK, (c + 1) * CHUNK)
        kcols = slice(h * DK, (h + 1) * DK)
        q3 = q_s[rows, kcols].reshape(NB, SUB, DK)
        k3 = k_s[rows, kcols].reshape(NB, SUB, DK)
        b3 = b_s[rows, kcols].reshape(NB, SUB, DK)
        ps = []
        for j in range(SUB):
            kj = jnp.broadcast_to(k3[:, j:j + 1, :], (NB, SUB, DK))
            bj = jnp.broadcast_to(b3[:, j:j + 1, :], (NB, SUB, DK))
            pj = q3 * kj * jnp.exp(jnp.minimum(b3 - bj, 0.0))
            ps.append(pj.reshape(CHUNK, DK).astype(BF16))
        return jnp.concatenate(ps, axis=1)

    def gla_rest(c, h, dm):
        rows = slice(c * CHUNK, (c + 1) * CHUNK)
        kcols = slice(h * DK, (h + 1) * DK)
        vcols = slice(h * DV, (h + 1) * DV)
        qc = q_s[rows, kcols]
        kc = k_s[rows, kcols]
        bc = b_s[rows, kcols]
        vc = v_s[rows, vcols]
        st = st_s[h]
        k3 = kc.reshape(NB, SUB, DK)
        b3 = bc.reshape(NB, SUB, DK)
        bend = b3[:, SUB - 1:SUB, :]
        blast = bc[CHUNK - 1:CHUNK, :]
        qhat = (qc * jnp.exp(bc)).astype(BF16)
        o = _dot_nt(qhat, st.astype(BF16))
        kt = (k3 * jnp.exp(bend - b3)).reshape(CHUNK, DK).astype(BF16)
        qs = [qc[SUB * (j + 1):] * jnp.exp(bc[SUB * (j + 1):] - bend[j]) for j in range(NB - 1)]
        rm = _dot_nt(jnp.concatenate(qs, axis=0).astype(BF16), kt)
        arows = [jnp.zeros((SUB, CHUNK), F32)]
        for i in range(1, NB):
            acc = jnp.zeros((SUB, CHUNK), F32)
            for j in range(i):
                r0 = stack_off[j] + SUB * (i - j - 1)
                acc = jnp.where(lane_blk == j, rm[r0:r0 + SUB], acc)
            arows.append(acc)
        a = jnp.concatenate(arows, axis=0) + jnp.where(diag_mask, dm, 0.0)
        o_s[rows, vcols] = o + _dot(a.astype(BF16), vc)
        khat = (kc * jnp.exp(blast - bc)).astype(BF16)
        st_s[h] = st * jnp.exp(blast) + _dot_tn(vc, khat)

    nslots = (tm // CHUNK) * 2 * HEADS
    done = [0]

    def run_tasks(slot):
        upto = ((slot + 1) * len(tasks) + nslots - 1) // nslots
        while done[0] < upto:
            tasks[done[0]]()
            done[0] += 1

    slot = 0
    for c in range(tm // CHUNK):
        pcs = []
        for h in range(HEADS):
            pcs.append(gla_diag(c, h))
            run_tasks(slot)
            slot += 1
        dm_all = _dot(jnp.concatenate(pcs, axis=0), emat)
        for h in range(HEADS):
            gla_rest(c, h, dm_all[h * CHUNK:(h + 1) * CHUNK])
            run_tasks(slot)
            slot += 1

    yb_in = (_head_norm(o_s[...], gain_ref[...]) * og_s[...]).astype(BF16)
    yb = _dot(yb_in, wb_ref[...])
    m = ga_s[...] * ya_s[...] + gb_s[...] * yb
    xo_ref[0] = x_ref[0] + _dot(m.astype(BF16), wo_ref[...])

    @pl.when(t == nt - 1)
    def _fin():
        for h in range(HEADS):
            sout_ref[0, h] = st_s[h].T


def _const_spec(shape):
    nd = len(shape)
    return pl.BlockSpec(shape, lambda *_: (0,) * nd, pipeline_mode=pl.Buffered(1))


def _layer_spec(arr, l):
    nd = arr.ndim
    return pl.BlockSpec((None,) + arr.shape[1:], lambda *_: (l,) + (0,) * (nd - 1),
                        pipeline_mode=pl.Buffered(1))


def _mixer_seq(x, cprev8, s0, W, l, tm):
    bsz, T, _ = x.shape
    assert T % tm == 0 and tm % CHUNK == 0
    ridx = jnp.arange(tm)
    lmat = ((ridx[:, None] // CHUNK == ridx[None, :] // CHUNK)
            & (ridx[None, :] <= ridx[:, None])).astype(BF16)
    emat = (jnp.arange(SUB * DK)[:, None] // DK == jnp.arange(CHUNK)[None, :] % SUB).astype(BF16)
    params = [W["w_main"], W["w_gate"], W["w_z"], W["w_a"], W["w_b"], W["w_o"], W["w_gk"],
              W["norm_mix"], W["conv_w"], W["b_gk"], W["gain"]]
    in_specs = [
        pl.BlockSpec((1, tm, D_MODEL), lambda b, t: (b, t, 0)),
        pl.BlockSpec((1, SUB, D_CONV), lambda b, t: (0, 0, 0)),
        pl.BlockSpec((1, HEADS, DK, DV), lambda b, t: (0, 0, 0, 0)),
    ] + [_layer_spec(p, l) for p in params] + [_const_spec(lmat.shape), _const_spec(emat.shape)]
    out_shape = (
        jax.ShapeDtypeStruct((bsz, T, D_MODEL), F32),
        jax.ShapeDtypeStruct((bsz, SUB, D_CONV), F32),
        jax.ShapeDtypeStruct((bsz, HEADS, DK, DV), F32),
    )
    out_specs = (
        pl.BlockSpec((1, tm, D_MODEL), lambda b, t: (b, t, 0)),
        pl.BlockSpec((1, SUB, D_CONV), lambda b, t: (b, 0, 0)),
        pl.BlockSpec((1, HEADS, DK, DV), lambda b, t: (b, 0, 0, 0)),
    )
    scratch = [
        pltpu.VMEM((tm, D_MODEL), BF16),
        pltpu.VMEM((tm + SUB, D_CONV), F32),
        pltpu.VMEM((HEADS, DV, DK), F32),
        pltpu.VMEM((tm, KDIM), F32),
        pltpu.VMEM((tm, KDIM), F32),
        pltpu.VMEM((tm, KDIM), F32),
        pltpu.VMEM((tm, VDIM), BF16),
        pltpu.VMEM((tm, VDIM), F32),
        pltpu.VMEM((tm, D_CONV), BF16),
        pltpu.VMEM((tm, D_MODEL), F32),
        pltpu.VMEM((tm, VDIM), F32),
        pltpu.VMEM((tm, D_MODEL), F32),
        pltpu.VMEM((tm, D_MODEL), F32),
    ]
    return pl.pallas_call(
        _mixer_seq_kernel,
        grid=(bsz, T // tm),
        in_specs=in_specs,
        out_specs=out_specs,
        out_shape=out_shape,
        scratch_shapes=scratch,
        compiler_params=pltpu.CompilerParams(
            dimension_semantics=("arbitrary", "arbitrary"),
            vmem_limit_bytes=VMEM_LIMIT),
        name="mixer_seq",
    )(x, cprev8, s0, *params, lmat, emat)


def _mixer_step_kernel(x_ref, cp_ref, s_ref, wm_ref, wg_ref, wz_ref,
                       wa_ref, wb_ref, wo_ref, wgk_ref,
                       nmix_ref, convw_ref, bgk_ref, gain_ref, *rest):
    xo_ref, cnew_ref, sout_ref, qt_s, kt_s, at_s, v_s, o_s, ya_s = rest[-9:]
    i = pl.program_id(0)
    n = pl.num_programs(0)
    nseq = x_ref.shape[0]

    def xn_bf16():
        return _rms(x_ref[...], nmix_ref[...]).astype(BF16)

    @pl.when(i == 0)
    def _dense_in():
        xn = xn_bf16()
        u = _dot(xn, wm_ref[:, C_GC:C_GC + D_CONV]) * _dot(xn, wm_ref[:, C_H:C_H + D_CONV])
        cw = convw_ref[...]
        p1 = cp_ref[:, D_CONV:2 * D_CONV]
        yconv = cw[0:1] * cp_ref[:, 0:D_CONV] + cw[1:2] * p1 + cw[2:3] * u
        cnew_ref[:, 0:D_CONV] = p1
        cnew_ref[:, D_CONV:2 * D_CONV] = u
        ca = (_dot(xn, wm_ref[:, C_GB:C_GB + D_CONV]) * yconv).astype(BF16)
        ya_s[...] = _dot(ca, wa_ref[...])
        v_s[...] = _dot(xn, wm_ref[:, C_V:C_V + VDIM])
        qt_s[...] = (_dot(xn, wm_ref[:, C_Q:C_Q + KDIM]) * (DK ** -0.5)).T
        kt_s[...] = _dot(xn, wm_ref[:, C_K:C_K + KDIM]).T
        z = _dot(xn, wz_ref[...]).astype(BF16)
        g = _log_sigmoid(_dot(z, wgk_ref[...]) + bgk_ref[...]) * (1.0 / GATE_NORMALIZER)
        at_s[...] = jnp.exp(g).T

    shift = (nseq - i * SUB) % nseq
    qt = pltpu.roll(qt_s[...], shift, 1)
    kt = pltpu.roll(kt_s[...], shift, 1)
    at = pltpu.roll(at_s[...], shift, 1)
    for j in range(SUB):
        row = pl.ds(i * SUB + j, 1)
        for h in range(HEADS):
            rk = slice(h * DK, (h + 1) * DK)
            cv = slice(h * DV, (h + 1) * DV)
            sn = s_ref[j, h] * at[rk, j:j + 1] + kt[rk, j:j + 1] * v_s[row, cv]
            sout_ref[j, h] = sn
            o_s[row, cv] = jnp.sum(sn * qt[rk, j:j + 1], axis=0, keepdims=True)

    @pl.when(i == n - 1)
    def _dense_out():
        xn = xn_bf16()
        og = _dot(xn, wm_ref[:, C_OG:C_OG + VDIM])
        yb_in = (_head_norm(o_s[...], gain_ref[...]) * (og * _sigmoid(og))).astype(BF16)
        yb = _dot(yb_in, wb_ref[...])
        m = (_sigmoid(_dot(xn, wg_ref[:, 0:D_MODEL])) * ya_s[...]
             + _sigmoid(_dot(xn, wg_ref[:, D_MODEL:2 * D_MODEL])) * yb)
        xo_ref[...] = x_ref[...] + _dot(m.astype(BF16), wo_ref[...])


def _mixer_step(x, conv_all, s_all, s_out_prev, W, l):
    nseq = x.shape[0]
    assert nseq == LANES
    params = [W["w_main"], W["w_gate"], W["w_z"], W["w_a"], W["w_b"],
              W["w_o"], W["w_gk"], W["norm_mix"], W["conv_w"], W["b_gk"], W["gain"]]
    state_blk = (None, SUB, HEADS, DK, DV)
    in_specs = [
        _const_spec(x.shape),
        _layer_spec(conv_all, l),
        pl.BlockSpec(state_blk, lambda i: (l, i, 0, 0, 0)),
    ] + [_layer_spec(p, l) for p in params]
    args = [x, conv_all, s_all, *params]
    aliases = {}
    if s_out_prev is not None:
        in_specs.append(pl.BlockSpec(memory_space=pl.ANY))
        aliases = {len(args): 2}
        args.append(s_out_prev)
    out_shape = (
        jax.ShapeDtypeStruct(x.shape, F32),
        jax.ShapeDtypeStruct((nseq, 2 * D_CONV), F32),
        jax.ShapeDtypeStruct(s_all.shape, F32),
    )
    out_specs = (
        pl.BlockSpec(x.shape, lambda i: (0, 0)),
        pl.BlockSpec((nseq, 2 * D_CONV), lambda i: (0, 0)),
        pl.BlockSpec(state_blk, lambda i: (l, i, 0, 0, 0)),
    )
    scratch = [
        pltpu.VMEM((KDIM, nseq), F32),
        pltpu.VMEM((KDIM, nseq), F32),
        pltpu.VMEM((KDIM, nseq), F32),
        pltpu.VMEM((nseq, VDIM), F32),
        pltpu.VMEM((nseq, VDIM), F32),
        pltpu.VMEM((nseq, D_MODEL), F32),
    ]
    return pl.pallas_call(
        _mixer_step_kernel,
        grid=(nseq // SUB,),
        in_specs=in_specs,
        out_specs=out_specs,
        out_shape=out_shape,
        scratch_shapes=scratch,
        input_output_aliases=aliases,
        compiler_params=pltpu.CompilerParams(
            dimension_semantics=("arbitrary",),
            vmem_limit_bytes=VMEM_LIMIT),
        name="mixer_step",
    )(*args)


def _ffn_kernel(x_ref, nffn_ref, wgu_ref, wdn_ref, fin_ref, o_ref, *, final):
    x = x_ref[...]
    xn = _rms(x, nffn_ref[...]).astype(BF16)
    g = _dot(xn, wgu_ref[:, 0:D_FF])
    u = _dot(xn, wgu_ref[:, D_FF:2 * D_FF])
    hid = (g * _sigmoid(g) * u).astype(BF16)
    y = x + _dot(hid, wdn_ref[...])
    if final:
        y = _rms(y, fin_ref[...])
    o_ref[...] = y


def _ffn(x, W, l, final, tm):
    n = x.shape[0]
    assert n % tm == 0
    params = [W["norm_ffn"], W["w_gu"], W["w_down"]]
    fin = W["final_norm"]
    return pl.pallas_call(
        functools.partial(_ffn_kernel, final=final),
        grid=(n // tm,),
        in_specs=[pl.BlockSpec((tm, D_MODEL), lambda i: (i, 0))]
        + [_layer_spec(p, l) for p in params] + [_const_spec(fin.shape)],
        out_specs=pl.BlockSpec((tm, D_MODEL), lambda i: (i, 0)),
        out_shape=jax.ShapeDtypeStruct((n, D_MODEL), F32),
        compiler_params=pltpu.CompilerParams(
            dimension_semantics=("arbitrary",),
            vmem_limit_bytes=VMEM_LIMIT),
        name="ffn",
    )(x, *params, fin)


def _prep_weights(w_in, conv_w, w_gk2, b_gk2, gla_gain, w_a_out, w_b_out, w_o,
                  norm_mix, norm_ffn, w_gu, w_down, final_norm):
    depth = w_in.shape[0]
    w_z = jnp.pad(w_in[:, :, Z_LO:Z_HI], ((0, 0), (0, 0), (0, LANES - GATE_RANK))).astype(BF16)
    w_gk = jnp.pad(w_gk2, ((0, 0), (0, LANES - GATE_RANK), (0, 0))).astype(BF16)
    return {
        "w_main": w_in[:, :, :D_MAIN].astype(BF16),
        "w_gate": w_in[:, :, Z_HI:].astype(BF16),
        "w_z": w_z,
        "w_gk": w_gk,
        "b_gk": b_gk2.reshape(depth, 1, KDIM),
        "gain": gla_gain.reshape(depth, 1, DV),
        "conv_w": conv_w,
        "w_a": w_a_out.astype(BF16),
        "w_b": w_b_out.astype(BF16),
        "w_o": w_o.astype(BF16),
        "norm_mix": norm_mix.reshape(depth, 1, D_MODEL),
        "norm_ffn": norm_ffn.reshape(depth, 1, D_MODEL),
        "w_gu": w_gu.astype(BF16),
        "w_down": w_down.astype(BF16),
        "final_norm": final_norm.reshape(1, D_MODEL),
    }


def kernel(x_prompt, x_sample, state_conv, state_gla, meta_tokens, w_in, conv_w, w_gk2, b_gk2,
           gla_gain, w_a_out, w_b_out, w_o, norm_mix, norm_ffn, w_gu, w_down, final_norm):
    depth = w_in.shape[0]
    bsz, seq, _ = x_prompt.shape
    nsmp = x_sample.shape[0]
    W = _prep_weights(w_in, conv_w, w_gk2, b_gk2, gla_gain, w_a_out, w_b_out, w_o,
                      norm_mix, norm_ffn, w_gu, w_down, final_norm)

    xm = jnp.pad(meta_tokens.astype(F32), ((CHUNK - N_META, 0), (0, 0)))[None]
    xp = x_prompt
    xs = x_sample.reshape(nsmp, D_MODEL)
    conv_all = state_conv.reshape(depth, nsmp, (CONV_W - 1) * D_CONV)
    zero_conv = jnp.zeros((1, SUB, D_CONV), F32)
    zero_state = jnp.zeros((1, HEADS, DK, DV), F32)

    p_conv, p_gla, s_conv = [], [], []
    s_gla = None
    for l in range(depth):
        last = l == depth - 1
        xm, m_conv, m_state = _mixer_seq(xm, zero_conv, zero_state, W, l, CHUNK)
        xp, pc, ps = _mixer_seq(xp, m_conv, m_state, W, l, 256)
        xs, sc, s_gla = _mixer_step(xs, conv_all, state_gla, s_gla, W, l)
        if not last:
            xm = _ffn(xm[0], W, l, False, CHUNK)[None]
        xp = _ffn(xp.reshape(bsz * seq, D_MODEL), W, l, last, 512).reshape(bsz, seq, D_MODEL)
        xs = _ffn(xs, W, l, last, nsmp)
        p_conv.append(pc[:, SUB - (CONV_W - 1):, :])
        p_gla.append(ps)
        s_conv.append(sc.reshape(nsmp, CONV_W - 1, D_CONV))

    return (xp, xs.reshape(nsmp, 1, D_MODEL), jnp.stack(p_conv), jnp.stack(p_gla),
            jnp.stack(s_conv), s_gla)
```

```python
import functools

import jax
import jax.numpy as jnp
from jax import lax
from jax.experimental import pallas as pl
from jax.experimental.pallas import tpu as pltpu

F32 = jnp.float32
BF16 = jnp.bfloat16

D_MODEL = 1024
N_META = 16
D_CONV = D_MODEL
CONV_W = 3
HEADS = 4
DK = 128
DV = 256
KDIM = HEADS * DK
VDIM = HEADS * DV
GATE_RANK = 16
GATE_NORMALIZER = 16.0
D_FF = 2816
EPS = 1e-6
LOG2_E = 1.4426950408889634

CHUNK = 64
SUB = 8
NB = CHUNK // SUB
LANES = 128
COLB = 512
VMEM_LIMIT = 60 * 1024 * 1024

C_GB, C_GC, C_H = 0, 1024, 2048
C_Q, C_K, C_V, C_OG = 3072, 3584, 4096, 5120
D_MAIN = 6144
Z_LO = D_MAIN
Z_HI = Z_LO + GATE_RANK


def _rms(x, g):
    ms = jnp.mean(x * x, axis=-1, keepdims=True)
    return x * lax.rsqrt(ms + EPS) * g


def _log_sigmoid(x):
    return jnp.minimum(x, 0.0) - jnp.log(1.0 + jnp.exp(-jnp.abs(x)))


def _sigmoid(x):
    return 1.0 / (1.0 + jnp.exp(-x))


def _dot(a, b):
    return jnp.dot(a, b, preferred_element_type=F32)


def _dot_nt(a, b):
    return lax.dot_general(a, b, (((1,), (1,)), ((), ())), preferred_element_type=F32)


def _dot_tn(a, b):
    return lax.dot_general(a, b, (((0,), (0,)), ((), ())), preferred_element_type=F32)


def _split3(x):
    x1 = x.astype(BF16)
    r1 = x - x1.astype(F32)
    x2 = r1.astype(BF16)
    r2 = r1 - x2.astype(F32)
    return x1, x2, r2.astype(BF16)


def _head_norm(o, gain):
    return jnp.concatenate(
        [_rms(o[:, h * DV:(h + 1) * DV], gain) for h in range(HEADS)], axis=1)


def _mixer_seq_kernel(x_ref, cprev_ref, s0_ref, wm_ref, wg_ref, wz_ref, wa_ref, wb_ref, wo_ref,
                      wgk_ref, nmix_ref, convw_ref, bgk_ref, gain_ref, lmat_ref, emat_ref,
                      xo_ref, cnew_ref, sout_ref,
                      ubuf, st_s, q_s, k_s, b_s, v_s, o_s, ca_s, ya_s, og_s, ga_s, gb_s):
    t = pl.program_id(1)
    nt = pl.num_programs(1)
    tm = x_ref.shape[1]

    @pl.when(t == 0)
    def _init():
        ubuf[0:SUB, :] = cprev_ref[0]
        for h in range(HEADS):
            st_s[h] = s0_ref[0, h].T

    xn = _rms(x_ref[0], nmix_ref[...]).astype(BF16)

    def proj(w_ref, c0, n):
        return _dot(xn, w_ref[:, c0:c0 + n])

    z = proj(wz_ref, 0, LANES).astype(BF16)
    g = _log_sigmoid(_dot(z, wgk_ref[...]) + bgk_ref[...]) * (LOG2_E / GATE_NORMALIZER)
    g1, g2, g3 = _split3(g)
    q_s[...] = proj(wm_ref, C_Q, KDIM) * (DK ** -0.5)
    k_s[...] = proj(wm_ref, C_K, KDIM)
    v_s[...] = proj(wm_ref, C_V, VDIM).astype(BF16)
    b_s[...] = _dot(lmat_ref[...], jnp.concatenate([g1, g2, g3], axis=0))

    def cols_of(cb):
        return slice(cb * COLB, (cb + 1) * COLB)

    def task_u(cb):
        u = proj(wm_ref, C_GC + cb * COLB, COLB) * proj(wm_ref, C_H + cb * COLB, COLB)
        ubuf[SUB:SUB + tm, cols_of(cb)] = u

    def task_conv(cb):
        cols = cols_of(cb)
        cw = convw_ref[:, cols]
        yconv = (cw[0:1] * ubuf[SUB - 2:SUB - 2 + tm, cols]
                 + cw[1:2] * ubuf[SUB - 1:SUB - 1 + tm, cols]
                 + cw[2:3] * ubuf[SUB:SUB + tm, cols])
        ca_s[:, cols] = (proj(wm_ref, C_GB + cb * COLB, COLB) * yconv).astype(BF16)
        tail = ubuf[tm:tm + SUB, cols]
        ubuf[0:SUB, cols] = tail
        cnew_ref[0, :, cols] = tail

    def task_ya(cb):
        ya_s[:, cols_of(cb)] = _dot(ca_s[...], wa_ref[:, cols_of(cb)])

    def task_og(cb):
        og = proj(wm_ref, C_OG + cb * COLB, COLB)
        og_s[:, cols_of(cb)] = og * _sigmoid(og)

    def task_ga(cb):
        ga_s[:, cols_of(cb)] = _sigmoid(proj(wg_ref, cb * COLB, COLB))

    def task_gb(cb):
        gb_s[:, cols_of(cb)] = _sigmoid(proj(wg_ref, D_MODEL + cb * COLB, COLB))

    ncb = D_MODEL // COLB
    tasks = []
    for cb in range(ncb):
        tasks += [functools.partial(task_u, cb), functools.partial(task_conv, cb)]
    for fn in (task_ya, task_og, task_ga, task_gb):
        tasks += [functools.partial(fn, cb) for cb in range(ncb)]

    ti = lax.broadcasted_iota(jnp.int32, (CHUNK, CHUNK), 0)
    si = lax.broadcasted_iota(jnp.int32, (CHUNK, CHUNK), 1)
    diag_mask = ((si >> 3) == (ti >> 3)) & ((si & 7) <= (ti & 7))
    lane_blk = lax.broadcasted_iota(jnp.int32, (SUB, CHUNK), 1) >> 3
    emat = emat_ref[...]
    stack_off = [sum(CHUNK - SUB * (i + 1) for i in range(j)) for j in range(NB)]

    def gla_diag(c, h):
        r0 = c * CHUNK
        rows = slice(r0, r0 + CHUNK)
        kcols = slice(h * DK, (h + 1) * DK)
        qc = q_s[rows, kcols]
        bc = b_s[rows, kcols]

        def row_of_each_block(ref, j):
            return jnp.concatenate(
                [jnp.broadcast_to(ref[r0 + SUB * i + j:r0 + SUB * i + j + 1, kcols], (SUB, DK))
                 for i in range(NB)], axis=0)

        ps = []
        for j in range(SUB):
            kj = row_of_each_block(k_s, j)
            bj = row_of_each_block(b_s, j)
            ps.append((qc * kj * jnp.exp2(jnp.minimum(bc - bj, 0.0))).astype(BF16))
        return jnp.concatenate(ps, axis=1)

    def gla_rest(c, h, dm):
        r0 = c * CHUNK
        rows = slice(r0, r0 + CHUNK)
        kcols = slice(h * DK, (h + 1) * DK)
        vcols = slice(h * DV, (h + 1) * DV)
        qc = q_s[rows, kcols]
        kc = k_s[rows, kcols]
        bc = b_s[rows, kcols]
        vc = v_s[rows, vcols]
        st = st_s[h]
        bend = [b_s[r0 + SUB * j + SUB - 1:r0 + SUB * (j + 1), kcols] for j in range(NB)]
        bend_rows = jnp.concatenate([jnp.broadcast_to(e, (SUB, DK)) for e in bend], axis=0)
        blast = bend[NB - 1]
        qhat = (qc * jnp.exp2(bc)).astype(BF16)
        o = _dot_nt(qhat, st.astype(BF16))
        kt = (kc * jnp.exp2(bend_rows - bc)).astype(BF16)
        qs = [qc[SUB * (j + 1):] * jnp.exp2(bc[SUB * (j + 1):] - bend[j]) for j in range(NB - 1)]
        rm = _dot_nt(jnp.concatenate(qs, axis=0).astype(BF16), kt)
        arows = [jnp.zeros((SUB, CHUNK), F32)]
        for i in range(1, NB):
            acc = jnp.zeros((SUB, CHUNK), F32)
            for j in range(i):
                r0 = stack_off[j] + SUB * (i - j - 1)
                acc = jnp.where(lane_blk == j, rm[r0:r0 + SUB], acc)
            arows.append(acc)
        a = jnp.concatenate(arows, axis=0) + jnp.where(diag_mask, dm, 0.0)
        o_s[rows, vcols] = o + _dot(a.astype(BF16), vc)
        khat = (kc * jnp.exp2(blast - bc)).astype(BF16)
        st_s[h] = st * jnp.exp2(blast) + _dot_tn(vc, khat)

    nslots = (tm // CHUNK) * 2 * HEADS
    done = [0]

    def run_tasks(slot):
        upto = ((slot + 1) * len(tasks) + nslots - 1) // nslots
        while done[0] < upto:
            tasks[done[0]]()
            done[0] += 1

    slot = 0
    for c in range(tm // CHUNK):
        pcs = []
        for h in range(HEADS):
            pcs.append(gla_diag(c, h))
            run_tasks(slot)
            slot += 1
        dm_all = _dot(jnp.concatenate(pcs, axis=0), emat)
        for h in range(HEADS):
            gla_rest(c, h, dm_all[h * CHUNK:(h + 1) * CHUNK])
            run_tasks(slot)
            slot += 1

    yb_in = (_head_norm(o_s[...], gain_ref[...]) * og_s[...]).astype(BF16)
    yb = _dot(yb_in, wb_ref[...])
    m = ga_s[...] * ya_s[...] + gb_s[...] * yb
    xo_ref[0] = x_ref[0] + _dot(m.astype(BF16), wo_ref[...])

    @pl.when(t == nt - 1)
    def _fin():
        for h in range(HEADS):
            sout_ref[0, h] = st_s[h].T


def _const_spec(shape):
    nd = len(shape)
    return pl.BlockSpec(shape, lambda *_: (0,) * nd, pipeline_mode=pl.Buffered(1))


def _layer_spec(arr, l):
    nd = arr.ndim
    return pl.BlockSpec((None,) + arr.shape[1:], lambda *_: (l,) + (0,) * (nd - 1),
                        pipeline_mode=pl.Buffered(1))


def _mixer_seq(x, cprev8, s0, W, l, tm):
    bsz, T, _ = x.shape
    assert T % tm == 0 and tm % CHUNK == 0
    ridx = jnp.arange(tm)
    lmat = ((ridx[:, None] // CHUNK == ridx[None, :] // CHUNK)
            & (ridx[None, :] <= ridx[:, None])).astype(BF16)
    lmat = jnp.concatenate([lmat, lmat, lmat], axis=1)
    emat = (jnp.arange(SUB * DK)[:, None] // DK == jnp.arange(CHUNK)[None, :] % SUB).astype(BF16)
    params = [W["w_main"], W["w_gate"], W["w_z"], W["w_a"], W["w_b"], W["w_o"], W["w_gk"],
              W["norm_mix"], W["conv_w"], W["b_gk"], W["gain"]]
    in_specs = [
        pl.BlockSpec((1, tm, D_MODEL), lambda b, t: (b, t, 0)),
        pl.BlockSpec((1, SUB, D_CONV), lambda b, t: (0, 0, 0)),
        pl.BlockSpec((1, HEADS, DK, DV), lambda b, t: (0, 0, 0, 0)),
    ] + [_layer_spec(p, l) for p in params] + [_const_spec(lmat.shape), _const_spec(emat.shape)]
    out_shape = (
        jax.ShapeDtypeStruct((bsz, T, D_MODEL), F32),
        jax.ShapeDtypeStruct((bsz, SUB, D_CONV), F32),
        jax.ShapeDtypeStruct((bsz, HEADS, DK, DV), F32),
    )
    out_specs = (
        pl.BlockSpec((1, tm, D_MODEL), lambda b, t: (b, t, 0)),
        pl.BlockSpec((1, SUB, D_CONV), lambda b, t: (b, 0, 0)),
        pl.BlockSpec((1, HEADS, DK, DV), lambda b, t: (b, 0, 0, 0)),
    )
    scratch = [
        pltpu.VMEM((tm + SUB, D_CONV), F32),
        pltpu.VMEM((HEADS, DV, DK), F32),
        pltpu.VMEM((tm, KDIM), F32),
        pltpu.VMEM((tm, KDIM), F32),
        pltpu.VMEM((tm, KDIM), F32),
        pltpu.VMEM((tm, VDIM), BF16),
        pltpu.VMEM((tm, VDIM), F32),
        pltpu.VMEM((tm, D_CONV), BF16),
        pltpu.VMEM((tm, D_MODEL), F32),
        pltpu.VMEM((tm, VDIM), F32),
        pltpu.VMEM((tm, D_MODEL), F32),
        pltpu.VMEM((tm, D_MODEL), F32),
    ]
    return pl.pallas_call(
        _mixer_seq_kernel,
        grid=(bsz, T // tm),
        in_specs=in_specs,
        out_specs=out_specs,
        out_shape=out_shape,
        scratch_shapes=scratch,
        compiler_params=pltpu.CompilerParams(
            dimension_semantics=("arbitrary", "arbitrary"),
            vmem_limit_bytes=VMEM_LIMIT),
        name="mixer_seq",
    )(x, cprev8, s0, *params, lmat, emat)


def _mixer_step_kernel(x_ref, cp_ref, s_ref, wm_ref, wg_ref, wz_ref,
                       wa_ref, wb_ref, wo_ref, wgk_ref,
                       nmix_ref, convw_ref, bgk_ref, gain_ref, *rest):
    xo_ref, cnew_ref, sout_ref, qt_s, kt_s, at_s, v_s, o_s, ya_s = rest[-9:]
    i = pl.program_id(0)
    n = pl.num_programs(0)
    nseq = x_ref.shape[0]

    def xn_bf16():
        return _rms(x_ref[...], nmix_ref[...]).astype(BF16)

    @pl.when(i == 0)
    def _dense_in():
        xn = xn_bf16()
        u = _dot(xn, wm_ref[:, C_GC:C_GC + D_CONV]) * _dot(xn, wm_ref[:, C_H:C_H + D_CONV])
        cw = convw_ref[...]
        p1 = cp_ref[:, D_CONV:2 * D_CONV]
        yconv = cw[0:1] * cp_ref[:, 0:D_CONV] + cw[1:2] * p1 + cw[2:3] * u
        cnew_ref[:, 0:D_CONV] = p1
        cnew_ref[:, D_CONV:2 * D_CONV] = u
        ca = (_dot(xn, wm_ref[:, C_GB:C_GB + D_CONV]) * yconv).astype(BF16)
        ya_s[...] = _dot(ca, wa_ref[...])
        v_s[...] = _dot(xn, wm_ref[:, C_V:C_V + VDIM])
        qt_s[...] = (_dot(xn, wm_ref[:, C_Q:C_Q + KDIM]) * (DK ** -0.5)).T
        kt_s[...] = _dot(xn, wm_ref[:, C_K:C_K + KDIM]).T
        z = _dot(xn, wz_ref[...]).astype(BF16)
        g = _log_sigmoid(_dot(z, wgk_ref[...]) + bgk_ref[...]) * (1.0 / GATE_NORMALIZER)
        at_s[...] = jnp.exp(g).T

    shift = (nseq - i * SUB) % nseq
    qt = pltpu.roll(qt_s[...], shift, 1)
    kt = pltpu.roll(kt_s[...], shift, 1)
    at = pltpu.roll(at_s[...], shift, 1)
    for j in range(SUB):
        row = pl.ds(i * SUB + j, 1)
        for h in range(HEADS):
            rk = slice(h * DK, (h + 1) * DK)
            cv = slice(h * DV, (h + 1) * DV)
            sn = s_ref[j, h] * at[rk, j:j + 1] + kt[rk, j:j + 1] * v_s[row, cv]
            sout_ref[j, h] = sn
            o_s[row, cv] = jnp.sum(sn * qt[rk, j:j + 1], axis=0, keepdims=True)

    @pl.when(i == n - 1)
    def _dense_out():
        xn = xn_bf16()
        og = _dot(xn, wm_ref[:, C_OG:C_OG + VDIM])
        yb_in = (_head_norm(o_s[...], gain_ref[...]) * (og * _sigmoid(og))).astype(BF16)
        yb = _dot(yb_in, wb_ref[...])
        m = (_sigmoid(_dot(xn, wg_ref[:, 0:D_MODEL])) * ya_s[...]
             + _sigmoid(_dot(xn, wg_ref[:, D_MODEL:2 * D_MODEL])) * yb)
        xo_ref[...] = x_ref[...] + _dot(m.astype(BF16), wo_ref[...])


def _mixer_step(x, conv_all, s_all, s_out_prev, W, l):
    nseq = x.shape[0]
    assert nseq == LANES
    params = [W["w_main"], W["w_gate"], W["w_z"], W["w_a"], W["w_b"],
              W["w_o"], W["w_gk"], W["norm_mix"], W["conv_w"], W["b_gk"], W["gain"]]
    state_blk = (None, SUB, HEADS, DK, DV)
    in_specs = [
        _const_spec(x.shape),
        _layer_spec(conv_all, l),
        pl.BlockSpec(state_blk, lambda i: (l, i, 0, 0, 0)),
    ] + [_layer_spec(p, l) for p in params]
    args = [x, conv_all, s_all, *params]
    aliases = {}
    if s_out_prev is not None:
        in_specs.append(pl.BlockSpec(memory_space=pl.ANY))
        aliases = {len(args): 2}
        args.append(s_out_prev)
    out_shape = (
        jax.ShapeDtypeStruct(x.shape, F32),
        jax.ShapeDtypeStruct((nseq, 2 * D_CONV), F32),
        jax.ShapeDtypeStruct(s_all.shape, F32),
    )
    out_specs = (
        pl.BlockSpec(x.shape, lambda i: (0, 0)),
        pl.BlockSpec((nseq, 2 * D_CONV), lambda i: (0, 0)),
        pl.BlockSpec(state_blk, lambda i: (l, i, 0, 0, 0)),
    )
    scratch = [
        pltpu.VMEM((KDIM, nseq), F32),
        pltpu.VMEM((KDIM, nseq), F32),
        pltpu.VMEM((KDIM, nseq), F32),
        pltpu.VMEM((nseq, VDIM), F32),
        pltpu.VMEM((nseq, VDIM), F32),
        pltpu.VMEM((nseq, D_MODEL), F32),
    ]
    return pl.pallas_call(
        _mixer_step_kernel,
        grid=(nseq // SUB,),
        in_specs=in_specs,
        out_specs=out_specs,
        out_shape=out_shape,
        scratch_shapes=scratch,
        input_output_aliases=aliases,
        compiler_params=pltpu.CompilerParams(
            dimension_semantics=("arbitrary",),
            vmem_limit_bytes=VMEM_LIMIT),
        name="mixer_step",
    )(*args)


def _ffn_kernel(x_ref, nffn_ref, wgu_ref, wdn_ref, fin_ref, o_ref, *, final):
    x = x_ref[...]
    xn = _rms(x, nffn_ref[...]).astype(BF16)
    g = _dot(xn, wgu_ref[:, 0:D_FF])
    u = _dot(xn, wgu_ref[:, D_FF:2 * D_FF])
    hid = (g * _sigmoid(g) * u).astype(BF16)
    y = x + _dot(hid, wdn_ref[...])
    if final:
        y = _rms(y, fin_ref[...])
    o_ref[...] = y


def _ffn(x, W, l, final, tm):
    n = x.shape[0]
    assert n % tm == 0
    params = [W["norm_ffn"], W["w_gu"], W["w_down"]]
    fin = W["final_norm"]
    return pl.pallas_call(
        functools.partial(_ffn_kernel, final=final),
        grid=(n // tm,),
        in_specs=[pl.BlockSpec((tm, D_MODEL), lambda i: (i, 0))]
        + [_layer_spec(p, l) for p in params] + [_const_spec(fin.shape)],
        out_specs=pl.BlockSpec((tm, D_MODEL), lambda i: (i, 0)),
        out_shape=jax.ShapeDtypeStruct((n, D_MODEL), F32),
        compiler_params=pltpu.CompilerParams(
            dimension_semantics=("arbitrary",),
            vmem_limit_bytes=VMEM_LIMIT),
        name="ffn",
    )(x, *params, fin)


def _repack_kernel(w_ref, wm_ref, wg_ref, wz_ref):
    rows = w_ref.shape[0]
    wm_ref[...] = w_ref[:, 0:D_MAIN].astype(BF16)
    wg_ref[...] = w_ref[:, Z_HI:Z_HI + 2 * D_MODEL].astype(BF16)
    wz_ref[...] = jnp.concatenate(
        [w_ref[:, Z_LO:Z_HI], jnp.zeros((rows, LANES - GATE_RANK), F32)], axis=1).astype(BF16)


def _repack_w_in(w_in, rb=256):
    depth, d, d_in = w_in.shape
    out_spec = lambda n: pl.BlockSpec((None, rb, n), lambda l, i: (l, i, 0))
    return pl.pallas_call(
        _repack_kernel,
        grid=(depth, d // rb),
        in_specs=[pl.BlockSpec((None, rb, d_in), lambda l, i: (l, i, 0))],
        out_specs=(out_spec(D_MAIN), out_spec(2 * D_MODEL), out_spec(LANES)),
        out_shape=(jax.ShapeDtypeStruct((depth, d, D_MAIN), BF16),
                   jax.ShapeDtypeStruct((depth, d, 2 * D_MODEL), BF16),
                   jax.ShapeDtypeStruct((depth, d, LANES), BF16)),
        compiler_params=pltpu.CompilerParams(
            dimension_semantics=("arbitrary", "arbitrary"),
            vmem_limit_bytes=VMEM_LIMIT),
        name="repack_w_in",
    )(w_in)


def _prep_weights(w_in, conv_w, w_gk2, b_gk2, gla_gain, w_a_out, w_b_out, w_o,
                  norm_mix, norm_ffn, w_gu, w_down, final_norm):
    depth = w_in.shape[0]
    w_main, w_gate, w_z = _repack_w_in(w_in)
    w_gk = jnp.pad(w_gk2, ((0, 0), (0, LANES - GATE_RANK), (0, 0))).astype(BF16)
    return {
        "w_main": w_main,
        "w_gate": w_gate,
        "w_z": w_z,
        "w_gk": w_gk,
        "b_gk": b_gk2.reshape(depth, 1, KDIM),
        "gain": gla_gain.reshape(depth, 1, DV),
        "conv_w": conv_w,
        "w_a": w_a_out.astype(BF16),
        "w_b": w_b_out.astype(BF16),
        "w_o": w_o.astype(BF16),
        "norm_mix": norm_mix.reshape(depth, 1, D_MODEL),
        "norm_ffn": norm_ffn.reshape(depth, 1, D_MODEL),
        "w_gu": w_gu.astype(BF16),
        "w_down": w_down.astype(BF16),
        "final_norm": final_norm.reshape(1, D_MODEL),
    }


def kernel(x_prompt, x_sample, state_conv, state_gla, meta_tokens, w_in, conv_w, w_gk2, b_gk2,
           gla_gain, w_a_out, w_b_out, w_o, norm_mix, norm_ffn, w_gu, w_down, final_norm):
    depth = w_in.shape[0]
    bsz, seq, _ = x_prompt.shape
    nsmp = x_sample.shape[0]
    W = _prep_weights(w_in, conv_w, w_gk2, b_gk2, gla_gain, w_a_out, w_b_out, w_o,
                      norm_mix, norm_ffn, w_gu, w_down, final_norm)

    xm = jnp.pad(meta_tokens.astype(F32), ((CHUNK - N_META, 0), (0, 0)))[None]
    xp = x_prompt
    xs = x_sample.reshape(nsmp, D_MODEL)
    conv_all = state_conv.reshape(depth, nsmp, (CONV_W - 1) * D_CONV)
    zero_conv = jnp.zeros((1, SUB, D_CONV), F32)
    zero_state = jnp.zeros((1, HEADS, DK, DV), F32)

    p_conv, p_gla, s_conv = [], [], []
    s_gla = None
    for l in range(depth):
        last = l == depth - 1
        xm, m_conv, m_state = _mixer_seq(xm, zero_conv, zero_state, W, l, CHUNK)
        xp, pc, ps = _mixer_seq(xp, m_conv, m_state, W, l, 256)
        xs, sc, s_gla = _mixer_step(xs, conv_all, state_gla, s_gla, W, l)
        if not last:
            xm = _ffn(xm[0], W, l, False, CHUNK)[None]
        xp = _ffn(xp.reshape(bsz * seq, D_MODEL), W, l, last, 512).reshape(bsz, seq, D_MODEL)
        xs = _ffn(xs, W, l, last, nsmp)
        p_conv.append(pc[:, SUB - (CONV_W - 1):, :])
        p_gla.append(ps)
        s_conv.append(sc.reshape(nsmp, CONV_W - 1, D_CONV))

    return (xp, xs.reshape(nsmp, 1, D_MODEL), jnp.stack(p_conv), jnp.stack(p_gla),
            jnp.stack(s_conv), s_gla)
```

```python
import functools

import jax
import jax.numpy as jnp
from jax import lax
from jax.experimental import pallas as pl
from jax.experimental.pallas import tpu as pltpu

F32 = jnp.float32
BF16 = jnp.bfloat16

D_MODEL = 1024
N_META = 16
D_CONV = D_MODEL
CONV_W = 3
HEADS = 4
DK = 128
DV = 256
KDIM = HEADS * DK
VDIM = HEADS * DV
GATE_RANK = 16
GATE_NORMALIZER = 16.0
D_FF = 2816
EPS = 1e-6
LOG2_E = 1.4426950408889634

CHUNK = 64
SUB = 8
NB = CHUNK // SUB
LANES = 128
COLB = 512
CUMSUM_ROWS = 256
VMEM_LIMIT = 60 * 1024 * 1024

C_GB, C_GC, C_H = 0, 1024, 2048
C_Q, C_K, C_V, C_OG = 3072, 3584, 4096, 5120
D_MAIN = 6144
Z_LO = D_MAIN
Z_HI = Z_LO + GATE_RANK


def _rms(x, g):
    ms = jnp.mean(x * x, axis=-1, keepdims=True)
    return x * lax.rsqrt(ms + EPS) * g


def _log_sigmoid(x):
    return jnp.minimum(x, 0.0) - jnp.log(1.0 + jnp.exp(-jnp.abs(x)))


def _sigmoid(x):
    return 1.0 / (1.0 + jnp.exp(-x))


def _dot(a, b):
    return jnp.dot(a, b, preferred_element_type=F32)


def _dot_nt(a, b):
    return lax.dot_general(a, b, (((1,), (1,)), ((), ())), preferred_element_type=F32)


def _dot_tn(a, b):
    return lax.dot_general(a, b, (((0,), (0,)), ((), ())), preferred_element_type=F32)


def _split3(x):
    x1 = x.astype(BF16)
    r1 = x - x1.astype(F32)
    x2 = r1.astype(BF16)
    r2 = r1 - x2.astype(F32)
    return x1, x2, r2.astype(BF16)


def _head_norm(o, gain):
    return jnp.concatenate(
        [_rms(o[:, h * DV:(h + 1) * DV], gain) for h in range(HEADS)], axis=1)


def _mixer_seq_kernel(x_ref, cprev_ref, s0_ref, wm_ref, wg_ref, wz_ref, wa_ref, wb_ref, wo_ref,
                      wgk_ref, nmix_ref, convw_ref, bgk_ref, gain_ref, lmat_ref, emat_ref,
                      xo_ref, cnew_ref, sout_ref,
                      ubuf, st_s, q_s, k_s, b_s, v_s, o_s, ca_s, ya_s, og_s, ga_s, gb_s):
    t = pl.program_id(1)
    nt = pl.num_programs(1)
    tm = x_ref.shape[1]

    @pl.when(t == 0)
    def _init():
        ubuf[0:SUB, :] = cprev_ref[0]
        for h in range(HEADS):
            st_s[h] = s0_ref[0, h].T

    xn = _rms(x_ref[0], nmix_ref[...]).astype(BF16)

    def proj(w_ref, c0, n):
        return _dot(xn, w_ref[:, c0:c0 + n])

    z = proj(wz_ref, 0, LANES).astype(BF16)
    g = _log_sigmoid(_dot(z, wgk_ref[...]) + bgk_ref[...]) * (LOG2_E / GATE_NORMALIZER)
    g1, g2, g3 = _split3(g)
    q_s[...] = proj(wm_ref, C_Q, KDIM) * (DK ** -0.5)
    k_s[...] = proj(wm_ref, C_K, KDIM)
    v_s[...] = proj(wm_ref, C_V, VDIM).astype(BF16)
    cr = lmat_ref.shape[0]
    for r0 in range(0, tm, cr):
        b_s[r0:r0 + cr, :] = _dot(lmat_ref[...], jnp.concatenate(
            [g1[r0:r0 + cr], g2[r0:r0 + cr], g3[r0:r0 + cr]], axis=0))

    def cols_of(cb):
        return slice(cb * COLB, (cb + 1) * COLB)

    def task_u(cb):
        u = proj(wm_ref, C_GC + cb * COLB, COLB) * proj(wm_ref, C_H + cb * COLB, COLB)
        ubuf[SUB:SUB + tm, cols_of(cb)] = u

    def task_conv(cb):
        cols = cols_of(cb)
        cw = convw_ref[:, cols]
        yconv = (cw[0:1] * ubuf[SUB - 2:SUB - 2 + tm, cols]
                 + cw[1:2] * ubuf[SUB - 1:SUB - 1 + tm, cols]
                 + cw[2:3] * ubuf[SUB:SUB + tm, cols])
        ca_s[:, cols] = (proj(wm_ref, C_GB + cb * COLB, COLB) * yconv).astype(BF16)
        tail = ubuf[tm:tm + SUB, cols]
        ubuf[0:SUB, cols] = tail
        cnew_ref[0, :, cols] = tail

    def task_ya(cb):
        ya_s[:, cols_of(cb)] = _dot(ca_s[...], wa_ref[:, cols_of(cb)])

    def task_og(cb):
        og = proj(wm_ref, C_OG + cb * COLB, COLB)
        og_s[:, cols_of(cb)] = og * _sigmoid(og)

    def task_ga(cb):
        ga_s[:, cols_of(cb)] = _sigmoid(proj(wg_ref, cb * COLB, COLB))

    def task_gb(cb):
        gb_s[:, cols_of(cb)] = _sigmoid(proj(wg_ref, D_MODEL + cb * COLB, COLB))

    ncb = D_MODEL // COLB
    tasks = []
    for cb in range(ncb):
        tasks += [functools.partial(task_u, cb), functools.partial(task_conv, cb)]
    for fn in (task_ya, task_og, task_ga, task_gb):
        tasks += [functools.partial(fn, cb) for cb in range(ncb)]

    ti = lax.broadcasted_iota(jnp.int32, (CHUNK, CHUNK), 0)
    si = lax.broadcasted_iota(jnp.int32, (CHUNK, CHUNK), 1)
    diag_mask = ((si >> 3) == (ti >> 3)) & ((si & 7) <= (ti & 7))
    lane_blk = lax.broadcasted_iota(jnp.int32, (SUB, CHUNK), 1) >> 3
    emat = emat_ref[...]
    stack_off = [sum(CHUNK - SUB * (i + 1) for i in range(j)) for j in range(NB)]

    def gla_diag(c, h):
        r0 = c * CHUNK
        rows = slice(r0, r0 + CHUNK)
        kcols = slice(h * DK, (h + 1) * DK)
        qc = q_s[rows, kcols]
        bc = b_s[rows, kcols]

        def row_of_each_block(ref, j):
            return jnp.concatenate(
                [jnp.broadcast_to(ref[r0 + SUB * i + j:r0 + SUB * i + j + 1, kcols], (SUB, DK))
                 for i in range(NB)], axis=0)

        ps = []
        for j in range(SUB):
            kj = row_of_each_block(k_s, j)
            bj = row_of_each_block(b_s, j)
            ps.append((qc * kj * jnp.exp2(jnp.minimum(bc - bj, 0.0))).astype(BF16))
        return jnp.concatenate(ps, axis=1)

    def gla_rest(c, h, dm):
        r0 = c * CHUNK
        rows = slice(r0, r0 + CHUNK)
        kcols = slice(h * DK, (h + 1) * DK)
        vcols = slice(h * DV, (h + 1) * DV)
        qc = q_s[rows, kcols]
        kc = k_s[rows, kcols]
        bc = b_s[rows, kcols]
        vc = v_s[rows, vcols]
        st = st_s[h]
        bend = [b_s[r0 + SUB * j + SUB - 1:r0 + SUB * (j + 1), kcols] for j in range(NB)]
        bend_rows = jnp.concatenate([jnp.broadcast_to(e, (SUB, DK)) for e in bend], axis=0)
        blast = bend[NB - 1]
        qhat = (qc * jnp.exp2(bc)).astype(BF16)
        o = _dot_nt(qhat, st.astype(BF16))
        kt = (kc * jnp.exp2(bend_rows - bc)).astype(BF16)
        qs = [qc[SUB * (j + 1):] * jnp.exp2(bc[SUB * (j + 1):] - bend[j]) for j in range(NB - 1)]
        rm = _dot_nt(jnp.concatenate(qs, axis=0).astype(BF16), kt)
        arows = [jnp.zeros((SUB, CHUNK), F32)]
        for i in range(1, NB):
            acc = jnp.zeros((SUB, CHUNK), F32)
            for j in range(i):
                r0 = stack_off[j] + SUB * (i - j - 1)
                acc = jnp.where(lane_blk == j, rm[r0:r0 + SUB], acc)
            arows.append(acc)
        a = jnp.concatenate(arows, axis=0) + jnp.where(diag_mask, dm, 0.0)
        o_s[rows, vcols] = o + _dot(a.astype(BF16), vc)
        khat = (kc * jnp.exp2(blast - bc)).astype(BF16)
        st_s[h] = st * jnp.exp2(blast) + _dot_tn(vc, khat)

    nslots = (tm // CHUNK) * 2 * HEADS
    done = [0]

    def run_tasks(slot):
        upto = ((slot + 1) * len(tasks) + nslots - 1) // nslots
        while done[0] < upto:
            tasks[done[0]]()
            done[0] += 1

    slot = 0
    for c in range(tm // CHUNK):
        pcs = []
        for h in range(HEADS):
            pcs.append(gla_diag(c, h))
            run_tasks(slot)
            slot += 1
        dm_all = _dot(jnp.concatenate(pcs, axis=0), emat)
        for h in range(HEADS):
            gla_rest(c, h, dm_all[h * CHUNK:(h + 1) * CHUNK])
            run_tasks(slot)
            slot += 1

    yb_in = (_head_norm(o_s[...], gain_ref[...]) * og_s[...]).astype(BF16)
    yb = _dot(yb_in, wb_ref[...])
    m = ga_s[...] * ya_s[...] + gb_s[...] * yb
    xo_ref[0] = x_ref[0] + _dot(m.astype(BF16), wo_ref[...])

    @pl.when(t == nt - 1)
    def _fin():
        for h in range(HEADS):
            sout_ref[0, h] = st_s[h].T


def _const_spec(shape):
    nd = len(shape)
    return pl.BlockSpec(shape, lambda *_: (0,) * nd, pipeline_mode=pl.Buffered(1))


def _layer_spec(arr, l):
    nd = arr.ndim
    return pl.BlockSpec((None,) + arr.shape[1:], lambda *_: (l,) + (0,) * (nd - 1),
                        pipeline_mode=pl.Buffered(1))


def _mixer_seq(x, cprev8, s0, W, l, tm):
    bsz, T, _ = x.shape
    assert T % tm == 0 and tm % CHUNK == 0
    ridx = jnp.arange(min(tm, CUMSUM_ROWS))
    lmat =((ridx[:, None] // CHUNK == ridx[None, :] // CHUNK)
            & (ridx[None, :] <= ridx[:, None])).astype(BF16)
    lmat = jnp.concatenate([lmat, lmat, lmat], axis=1)
    emat = (jnp.arange(SUB * DK)[:, None] // DK == jnp.arange(CHUNK)[None, :] % SUB).astype(BF16)
    params = [W["w_main"], W["w_gate"], W["w_z"], W["w_a"], W["w_b"], W["w_o"], W["w_gk"],
              W["norm_mix"], W["conv_w"], W["b_gk"], W["gain"]]
    in_specs = [
        pl.BlockSpec((1, tm, D_MODEL), lambda b, t: (b, t, 0)),
        pl.BlockSpec((1, SUB, D_CONV), lambda b, t: (0, 0, 0)),
        pl.BlockSpec((1, HEADS, DK, DV), lambda b, t: (0, 0, 0, 0)),
    ] + [_layer_spec(p, l) for p in params] + [_const_spec(lmat.shape), _const_spec(emat.shape)]
    out_shape = (
        jax.ShapeDtypeStruct((bsz, T, D_MODEL), F32),
        jax.ShapeDtypeStruct((bsz, SUB, D_CONV), F32),
        jax.ShapeDtypeStruct((bsz, HEADS, DK, DV), F32),
    )
    out_specs = (
        pl.BlockSpec((1, tm, D_MODEL), lambda b, t: (b, t, 0)),
        pl.BlockSpec((1, SUB, D_CONV), lambda b, t: (b, 0, 0)),
        pl.BlockSpec((1, HEADS, DK, DV), lambda b, t: (b, 0, 0, 0)),
    )
    scratch = [
        pltpu.VMEM((tm + SUB, D_CONV), F32),
        pltpu.VMEM((HEADS, DV, DK), F32),
        pltpu.VMEM((tm, KDIM), F32),
        pltpu.VMEM((tm, KDIM), F32),
        pltpu.VMEM((tm, KDIM), F32),
        pltpu.VMEM((tm, VDIM), BF16),
        pltpu.VMEM((tm, VDIM), F32),
        pltpu.VMEM((tm, D_CONV), BF16),
        pltpu.VMEM((tm, D_MODEL), F32),
        pltpu.VMEM((tm, VDIM), F32),
        pltpu.VMEM((tm, D_MODEL), F32),
        pltpu.VMEM((tm, D_MODEL), F32),
    ]
    return pl.pallas_call(
        _mixer_seq_kernel,
        grid=(bsz, T // tm),
        in_specs=in_specs,
        out_specs=out_specs,
        out_shape=out_shape,
        scratch_shapes=scratch,
        compiler_params=pltpu.CompilerParams(
            dimension_semantics=("arbitrary", "arbitrary"),
            vmem_limit_bytes=VMEM_LIMIT),
        name="mixer_seq",
    )(x, cprev8, s0, *params, lmat, emat)


def _mixer_step_kernel(x_ref, cp_ref, s_ref, wm_ref, wg_ref, wz_ref,
                       wa_ref, wb_ref, wo_ref, wgk_ref,
                       nmix_ref, convw_ref, bgk_ref, gain_ref, *rest):
    xo_ref, cnew_ref, sout_ref, qt_s, kt_s, at_s, v_s, o_s, ya_s = rest[-9:]
    i = pl.program_id(0)
    n = pl.num_programs(0)
    nseq = x_ref.shape[0]

    def xn_bf16():
        return _rms(x_ref[...], nmix_ref[...]).astype(BF16)

    @pl.when(i == 0)
    def _dense_in():
        xn = xn_bf16()
        u = _dot(xn, wm_ref[:, C_GC:C_GC + D_CONV]) * _dot(xn, wm_ref[:, C_H:C_H + D_CONV])
        cw = convw_ref[...]
        p1 = cp_ref[:, D_CONV:2 * D_CONV]
        yconv = cw[0:1] * cp_ref[:, 0:D_CONV] + cw[1:2] * p1 + cw[2:3] * u
        cnew_ref[:, 0:D_CONV] = p1
        cnew_ref[:, D_CONV:2 * D_CONV] = u
        ca = (_dot(xn, wm_ref[:, C_GB:C_GB + D_CONV]) * yconv).astype(BF16)
        ya_s[...] = _dot(ca, wa_ref[...])
        v_s[...] = _dot(xn, wm_ref[:, C_V:C_V + VDIM])
        qt_s[...] = (_dot(xn, wm_ref[:, C_Q:C_Q + KDIM]) * (DK ** -0.5)).T
        kt_s[...] = _dot(xn, wm_ref[:, C_K:C_K + KDIM]).T
        z = _dot(xn, wz_ref[...]).astype(BF16)
        g = _log_sigmoid(_dot(z, wgk_ref[...]) + bgk_ref[...]) * (1.0 / GATE_NORMALIZER)
        at_s[...] = jnp.exp(g).T

    shift = (nseq - i * SUB) % nseq
    qt = pltpu.roll(qt_s[...], shift, 1)
    kt = pltpu.roll(kt_s[...], shift, 1)
    at = pltpu.roll(at_s[...], shift, 1)
    for j in range(SUB):
        row = pl.ds(i * SUB + j, 1)
        for h in range(HEADS):
            rk = slice(h * DK, (h + 1) * DK)
            cv = slice(h * DV, (h + 1) * DV)
            sn = s_ref[j, h] * at[rk, j:j + 1] + kt[rk, j:j + 1] * v_s[row, cv]
            sout_ref[j, h] = sn
            o_s[row, cv] = jnp.sum(sn * qt[rk, j:j + 1], axis=0, keepdims=True)

    @pl.when(i == n - 1)
    def _dense_out():
        xn = xn_bf16()
        og = _dot(xn, wm_ref[:, C_OG:C_OG + VDIM])
        yb_in = (_head_norm(o_s[...], gain_ref[...]) * (og * _sigmoid(og))).astype(BF16)
        yb = _dot(yb_in, wb_ref[...])
        m = (_sigmoid(_dot(xn, wg_ref[:, 0:D_MODEL])) * ya_s[...]
             + _sigmoid(_dot(xn, wg_ref[:, D_MODEL:2 * D_MODEL])) * yb)
        xo_ref[...] = x_ref[...] + _dot(m.astype(BF16), wo_ref[...])


def _mixer_step(x, conv_all, s_all, s_out_prev, W, l):
    nseq = x.shape[0]
    assert nseq == LANES
    params = [W["w_main"], W["w_gate"], W["w_z"], W["w_a"], W["w_b"],
              W["w_o"], W["w_gk"], W["norm_mix"], W["conv_w"], W["b_gk"], W["gain"]]
    state_blk = (None, SUB, HEADS, DK, DV)
    in_specs = [
        _const_spec(x.shape),
        _layer_spec(conv_all, l),
        pl.BlockSpec(state_blk, lambda i: (l, i, 0, 0, 0)),
    ] + [_layer_spec(p, l) for p in params]
    args = [x, conv_all, s_all, *params]
    aliases = {}
    if s_out_prev is not None:
        in_specs.append(pl.BlockSpec(memory_space=pl.ANY))
        aliases = {len(args): 2}
        args.append(s_out_prev)
    out_shape = (
        jax.ShapeDtypeStruct(x.shape, F32),
        jax.ShapeDtypeStruct((nseq, 2 * D_CONV), F32),
        jax.ShapeDtypeStruct(s_all.shape, F32),
    )
    out_specs = (
        pl.BlockSpec(x.shape, lambda i: (0, 0)),
        pl.BlockSpec((nseq, 2 * D_CONV), lambda i: (0, 0)),
        pl.BlockSpec(state_blk, lambda i: (l, i, 0, 0, 0)),
    )
    scratch = [
        pltpu.VMEM((KDIM, nseq), F32),
        pltpu.VMEM((KDIM, nseq), F32),
        pltpu.VMEM((KDIM, nseq), F32),
        pltpu.VMEM((nseq, VDIM), F32),
        pltpu.VMEM((nseq, VDIM), F32),
        pltpu.VMEM((nseq, D_MODEL), F32),
    ]
    return pl.pallas_call(
        _mixer_step_kernel,
        grid=(nseq // SUB,),
        in_specs=in_specs,
        out_specs=out_specs,
        out_shape=out_shape,
        scratch_shapes=scratch,
        input_output_aliases=aliases,
        compiler_params=pltpu.CompilerParams(
            dimension_semantics=("arbitrary",),
            vmem_limit_bytes=VMEM_LIMIT),
        name="mixer_step",
    )(*args)


def _ffn_kernel(x_ref, nffn_ref, wgu_ref, wdn_ref, fin_ref, o_ref, *, final):
    x = x_ref[...]
    xn = _rms(x, nffn_ref[...]).astype(BF16)
    g = _dot(xn, wgu_ref[:, 0:D_FF])
    u = _dot(xn, wgu_ref[:, D_FF:2 * D_FF])
    hid = (g * _sigmoid(g) * u).astype(BF16)
    y = x + _dot(hid, wdn_ref[...])
    if final:
        y = _rms(y, fin_ref[...])
    o_ref[...] = y


def _ffn(x, W, l, final, tm):
    n = x.shape[0]
    assert n % tm == 0
    params = [W["norm_ffn"], W["w_gu"], W["w_down"]]
    fin = W["final_norm"]
    return pl.pallas_call(
        functools.partial(_ffn_kernel, final=final),
        grid=(n // tm,),
        in_specs=[pl.BlockSpec((tm, D_MODEL), lambda i: (i, 0))]
        + [_layer_spec(p, l) for p in params] + [_const_spec(fin.shape)],
        out_specs=pl.BlockSpec((tm, D_MODEL), lambda i: (i, 0)),
        out_shape=jax.ShapeDtypeStruct((n, D_MODEL), F32),
        compiler_params=pltpu.CompilerParams(
            dimension_semantics=("arbitrary",),
            vmem_limit_bytes=VMEM_LIMIT),
        name="ffn",
    )(x, *params, fin)


def _transpose_cast_kernel(wt_ref, o_ref):
    o_ref[...] = wt_ref[0].T.astype(BF16)


def _transpose_cast(wt, row0, ncols, rb=512):
    depth, _, d = wt.shape
    return pl.pallas_call(
        _transpose_cast_kernel,
        grid=(depth, ncols // rb),
        in_specs=[pl.BlockSpec((pl.Element(1), pl.Element(rb), pl.Element(d)),
                               lambda l, i: (l, pl.multiple_of(row0 + i * rb, SUB), 0))],
        out_specs=pl.BlockSpec((None, d, rb), lambda l, i: (l, 0, i)),
        out_shape=jax.ShapeDtypeStruct((depth, d, ncols), BF16),
        compiler_params=pltpu.CompilerParams(
            dimension_semantics=("arbitrary", "arbitrary"),
            vmem_limit_bytes=VMEM_LIMIT),
        name="transpose_cast",
    )(wt)


def _prep_weights(w_in, conv_w, w_gk2, b_gk2, gla_gain, w_a_out, w_b_out, w_o,
                  norm_mix, norm_ffn, w_gu, w_down, final_norm):
    depth = w_in.shape[0]
    w_in_t = jnp.swapaxes(w_in, 1, 2)
    w_main = _transpose_cast(w_in_t, 0, D_MAIN)
    w_gate = _transpose_cast(w_in_t, Z_HI, 2 * D_MODEL)
    w_z = jnp.pad(w_in[:, :, Z_LO:Z_HI], ((0, 0), (0, 0), (0, LANES - GATE_RANK))).astype(BF16)
    w_gk = jnp.pad(w_gk2, ((0, 0), (0, LANES - GATE_RANK), (0, 0))).astype(BF16)
    return {
        "w_main": w_main,
        "w_gate": w_gate,
        "w_z": w_z,
        "w_gk": w_gk,
        "b_gk": b_gk2.reshape(depth, 1, KDIM),
        "gain": gla_gain.reshape(depth, 1, DV),
        "conv_w": conv_w,
        "w_a": w_a_out.astype(BF16),
        "w_b": w_b_out.astype(BF16),
        "w_o": w_o.astype(BF16),
        "norm_mix": norm_mix.reshape(depth, 1, D_MODEL),
        "norm_ffn": norm_ffn.reshape(depth, 1, D_MODEL),
        "w_gu": w_gu.astype(BF16),
        "w_down": w_down.astype(BF16),
        "final_norm": final_norm.reshape(1, D_MODEL),
    }


def kernel(x_prompt, x_sample, state_conv, state_gla, meta_tokens, w_in, conv_w, w_gk2, b_gk2,
           gla_gain, w_a_out, w_b_out, w_o, norm_mix, norm_ffn, w_gu, w_down, final_norm):
    depth = w_in.shape[0]
    bsz, seq, _ = x_prompt.shape
    nsmp = x_sample.shape[0]
    W = _prep_weights(w_in, conv_w, w_gk2, b_gk2, gla_gain, w_a_out, w_b_out, w_o,
                      norm_mix, norm_ffn, w_gu, w_down, final_norm)

    xm = jnp.pad(meta_tokens.astype(F32), ((CHUNK - N_META, 0), (0, 0)))[None]
    xp = x_prompt
    xs = x_sample.reshape(nsmp, D_MODEL)
    conv_all = state_conv.reshape(depth, nsmp, (CONV_W - 1) * D_CONV)
    zero_conv = jnp.zeros((1, SUB, D_CONV), F32)
    zero_state = jnp.zeros((1, HEADS, DK, DV), F32)

    p_conv, p_gla, s_conv = [], [], []
    s_gla = None
    for l in range(depth):
        last = l == depth - 1
        xm, m_conv, m_state = _mixer_seq(xm, zero_conv, zero_state, W, l, CHUNK)
        xp, pc, ps = _mixer_seq(xp, m_conv, m_state, W, l, 256)
        xs, sc, s_gla = _mixer_step(xs, conv_all, state_gla, s_gla, W, l)
        if not last:
            xm = _ffn(xm[0], W, l, False, CHUNK)[None]
        xp = _ffn(xp.reshape(bsz * seq, D_MODEL), W, l, last, 512).reshape(bsz, seq, D_MODEL)
        xs = _ffn(xs, W, l, last, nsmp)
        p_conv.append(pc[:, SUB - (CONV_W - 1):, :])
        p_gla.append(ps)
        s_conv.append(sc.reshape(nsmp, CONV_W - 1, D_CONV))

    return (xp, xs.reshape(nsmp, 1, D_MODEL), jnp.stack(p_conv), jnp.stack(p_gla),
            jnp.stack(s_conv), s_gla)
```

```python
import functools

import jax
import jax.numpy as jnp
from jax import lax
from jax.experimental import pallas as pl
from jax.experimental.pallas import tpu as pltpu

F32 = jnp.float32
BF16 = jnp.bfloat16

D_MODEL = 1024
N_META = 16
D_CONV = D_MODEL
CONV_W = 3
HEADS = 4
DK = 128
DV = 256
KDIM = HEADS * DK
VDIM = HEADS * DV
GATE_RANK = 16
GATE_NORMALIZER = 16.0
D_FF = 2816
EPS = 1e-6
LOG2_E = 1.4426950408889634

CHUNK = 64
SUB = 8
NB = CHUNK // SUB
LANES = 128
COLB = 512
CUMSUM_ROWS = 256
VMEM_LIMIT = 60 * 1024 * 1024

C_GB, C_GC, C_H = 0, 1024, 2048
C_Q, C_K, C_V, C_OG = 3072, 3584, 4096, 5120
D_MAIN = 6144
Z_LO = D_MAIN
Z_HI = Z_LO + GATE_RANK


def _rms(x, g):
    ms = jnp.mean(x * x, axis=-1, keepdims=True)
    return x * lax.rsqrt(ms + EPS) * g


def _log_sigmoid(x):
    return jnp.minimum(x, 0.0) - jnp.log(1.0 + jnp.exp(-jnp.abs(x)))


def _sigmoid(x):
    return 1.0 / (1.0 + jnp.exp(-x))


def _dot(a, b):
    return jnp.dot(a, b, preferred_element_type=F32)


def _dot_nt(a, b):
    return lax.dot_general(a, b, (((1,), (1,)), ((), ())), preferred_element_type=F32)


def _dot_tn(a, b):
    return lax.dot_general(a, b, (((0,), (0,)), ((), ())), preferred_element_type=F32)


def _split3(x):
    x1 = x.astype(BF16)
    r1 = x - x1.astype(F32)
    x2 = r1.astype(BF16)
    r2 = r1 - x2.astype(F32)
    return x1, x2, r2.astype(BF16)


def _head_norm(o, gain):
    return jnp.concatenate(
        [_rms(o[:, h * DV:(h + 1) * DV], gain) for h in range(HEADS)], axis=1)


def _mixer_seq_kernel(x_ref, xnext_ref, cprev_ref, s0_ref, wm_ref, wg_ref, wz_ref, wa_ref, wb_ref,
                      wo_ref, wgk_ref, nmix_ref, convw_ref, bgk_ref, gain_ref, lmat_ref,
                      xo_ref, cnew_ref, sout_ref,
                      xn_s, ubuf, st_s, q_s, k_s, b_s, v_s, o_s, ca_s, ya_s, og_s, ga_s, gb_s):
    t = pl.program_id(1)
    nt = pl.num_programs(1)
    tm = x_ref.shape[1]
    xn_buf = lax.bitwise_and(t, 1)
    xn_next_buf = lax.bitwise_and(t + 1, 1)

    @pl.when(t == 0)
    def _init():
        ubuf[0:SUB, :] = cprev_ref[0]
        xn_s[0] = _rms(x_ref[0], nmix_ref[...]).astype(BF16)
        for h in range(HEADS):
            st_s[h] = s0_ref[0, h].T

    xn = xn_s[xn_buf]

    def proj(w_ref, c0, n):
        return _dot(xn, w_ref[:, c0:c0 + n])

    def task_norm_next():
        xn_s[xn_next_buf] = _rms(xnext_ref[0], nmix_ref[...]).astype(BF16)

    z = proj(wz_ref, 0, LANES).astype(BF16)
    g = _log_sigmoid(_dot(z, wgk_ref[...]) + bgk_ref[...]) * (LOG2_E / GATE_NORMALIZER)
    g1, g2, g3 = _split3(g)
    q_s[...] = proj(wm_ref, C_Q, KDIM) * (DK ** -0.5)
    k_s[...] = proj(wm_ref, C_K, KDIM)
    v_s[...] = proj(wm_ref, C_V, VDIM).astype(BF16)
    cr = lmat_ref.shape[0]
    for r0 in range(0, tm, cr):
        b_s[r0:r0 + cr, :] = _dot(lmat_ref[...], jnp.concatenate(
            [g1[r0:r0 + cr], g2[r0:r0 + cr], g3[r0:r0 + cr]], axis=0))

    def cols_of(cb):
        return slice(cb * COLB, (cb + 1) * COLB)

    def task_u(cb):
        u = proj(wm_ref, C_GC + cb * COLB, COLB) * proj(wm_ref, C_H + cb * COLB, COLB)
        ubuf[SUB:SUB + tm, cols_of(cb)] = u

    def task_conv(cb):
        cols = cols_of(cb)
        cw = convw_ref[:, cols]
        yconv = (cw[0:1] * ubuf[SUB - 2:SUB - 2 + tm, cols]
                 + cw[1:2] * ubuf[SUB - 1:SUB - 1 + tm, cols]
                 + cw[2:3] * ubuf[SUB:SUB + tm, cols])
        ca_s[:, cols] = (proj(wm_ref, C_GB + cb * COLB, COLB) * yconv).astype(BF16)
        tail = ubuf[tm:tm + SUB, cols]
        ubuf[0:SUB, cols] = tail
        cnew_ref[0, :, cols] = tail

    def task_ya(cb):
        ya_s[:, cols_of(cb)] = _dot(ca_s[...], wa_ref[:, cols_of(cb)])

    def task_og(cb):
        og = proj(wm_ref, C_OG + cb * COLB, COLB)
        og_s[:, cols_of(cb)] = og * _sigmoid(og)

    def task_ga(cb):
        ga_s[:, cols_of(cb)] = _sigmoid(proj(wg_ref, cb * COLB, COLB))

    def task_gb(cb):
        gb_s[:, cols_of(cb)] = _sigmoid(proj(wg_ref, D_MODEL + cb * COLB, COLB))

    ncb = D_MODEL // COLB
    tasks = []
    for cb in range(ncb):
        tasks += [functools.partial(task_u, cb), functools.partial(task_conv, cb)]
    for fn in (task_ya, task_og, task_ga, task_gb):
        tasks += [functools.partial(fn, cb) for cb in range(ncb)]
    tasks.insert(2, task_norm_next)

    ti = lax.broadcasted_iota(jnp.int32, (CHUNK, CHUNK), 0)
    si = lax.broadcasted_iota(jnp.int32, (CHUNK, CHUNK), 1)
    diag_sel = [(si == ((ti >> 3) << 3) + j) & ((ti & 7) >= j) for j in range(SUB)]
    lane_blk = lax.broadcasted_iota(jnp.int32, (SUB, CHUNK), 1) >> 3
    stack_off = [sum(CHUNK - SUB * (i + 1) for i in range(j)) for j in range(NB)]

    def gla_diag(c, h):
        r0 = c * CHUNK
        rows = slice(r0, r0 + CHUNK)
        kcols = slice(h * DK, (h + 1) * DK)
        qc = q_s[rows, kcols]
        bc = b_s[rows, kcols]

        def row_of_each_block(ref, j):
            return jnp.concatenate(
                [jnp.broadcast_to(ref[r0 + SUB * i + j:r0 + SUB * i + j + 1, kcols], (SUB, DK))
                 for i in range(NB)], axis=0)

        acc = jnp.zeros((CHUNK, CHUNK), F32)
        for j in range(SUB):
            kj = row_of_each_block(k_s, j)
            bj = row_of_each_block(b_s, j)
            rj = jnp.sum(qc * kj * jnp.exp2(bc - bj), axis=-1, keepdims=True)
            acc = jnp.where(diag_sel[j], rj, acc)
        return acc

    def gla_rest(c, h, dm):
        r0 = c * CHUNK
        rows = slice(r0, r0 + CHUNK)
        kcols = slice(h * DK, (h + 1) * DK)
        vcols = slice(h * DV, (h + 1) * DV)
        qc = q_s[rows, kcols]
        kc = k_s[rows, kcols]
        bc = b_s[rows, kcols]
        vc = v_s[rows, vcols]
        st = st_s[h]
        bend = [b_s[r0 + SUB * j + SUB - 1:r0 + SUB * (j + 1), kcols] for j in range(NB)]
        bend_rows = jnp.concatenate([jnp.broadcast_to(e, (SUB, DK)) for e in bend], axis=0)
        blast = bend[NB - 1]
        qhat = (qc * jnp.exp2(bc)).astype(BF16)
        o = _dot_nt(qhat, st.astype(BF16))
        kt = (kc * jnp.exp2(bend_rows - bc)).astype(BF16)
        qs = [qc[SUB * (j + 1):] * jnp.exp2(bc[SUB * (j + 1):] - bend[j]) for j in range(NB - 1)]
        rm = _dot_nt(jnp.concatenate(qs, axis=0).astype(BF16), kt)
        arows = [jnp.zeros((SUB, CHUNK), F32)]
        for i in range(1, NB):
            acc = jnp.zeros((SUB, CHUNK), F32)
            for j in range(i):
                r0 = stack_off[j] + SUB * (i - j - 1)
                acc = jnp.where(lane_blk == j, rm[r0:r0 + SUB], acc)
            arows.append(acc)
        a = jnp.concatenate(arows, axis=0) + dm
        o_s[rows, vcols] = o + _dot(a.astype(BF16), vc)
        khat = (kc * jnp.exp2(blast - bc)).astype(BF16)
        st_s[h] = st * jnp.exp2(blast) + _dot_tn(vc, khat)

    nslots = (tm // CHUNK) * 2 * HEADS
    done = [0]

    def run_tasks(slot):
        upto = ((slot + 1) * len(tasks) + nslots - 1) // nslots
        while done[0] < upto:
            tasks[done[0]]()
            done[0] += 1

    slot = 0
    for c in range(tm // CHUNK):
        pcs = []
        for h in range(HEADS):
            pcs.append(gla_diag(c, h))
            run_tasks(slot)
            slot += 1
        for h in range(HEADS):
            gla_rest(c, h, pcs[h])
            run_tasks(slot)
            slot += 1

    yb_in = (_head_norm(o_s[...], gain_ref[...]) * og_s[...]).astype(BF16)
    yb = _dot(yb_in, wb_ref[...])
    m = ga_s[...] * ya_s[...] + gb_s[...] * yb
    xo_ref[0] = x_ref[0] + _dot(m.astype(BF16), wo_ref[...])

    @pl.when(t == nt - 1)
    def _fin():
        for h in range(HEADS):
            sout_ref[0, h] = st_s[h].T


def _const_spec(shape):
    nd = len(shape)
    return pl.BlockSpec(shape, lambda *_: (0,) * nd, pipeline_mode=pl.Buffered(1))


def _layer_spec(arr, l):
    nd = arr.ndim
    return pl.BlockSpec((None,) + arr.shape[1:], lambda *_: (l,) + (0,) * (nd - 1),
                        pipeline_mode=pl.Buffered(1))


def _mixer_seq(x, cprev8, s0, W, l, tm):
    bsz, T, _ = x.shape
    assert T % tm == 0 and tm % CHUNK == 0
    ridx = jnp.arange(min(tm, CUMSUM_ROWS))
    lmat =((ridx[:, None] // CHUNK == ridx[None, :] // CHUNK)
            & (ridx[None, :] <= ridx[:, None])).astype(BF16)
    lmat = jnp.concatenate([lmat, lmat, lmat], axis=1)
    params = [W["w_main"], W["w_gate"], W["w_z"], W["w_a"], W["w_b"], W["w_o"], W["w_gk"],
              W["norm_mix"], W["conv_w"], W["b_gk"], W["gain"]]
    nt = T // tm
    in_specs = [
        pl.BlockSpec((1, tm, D_MODEL), lambda b, t: (b, t, 0)),
        pl.BlockSpec((1, tm, D_MODEL), lambda b, t: (b, jnp.minimum(t + 1, nt - 1), 0)),
        pl.BlockSpec((1, SUB, D_CONV), lambda b, t: (0, 0, 0)),
        pl.BlockSpec((1, HEADS, DK, DV), lambda b, t: (0, 0, 0, 0)),
    ] + [_layer_spec(p, l) for p in params] + [_const_spec(lmat.shape)]
    out_shape = (
        jax.ShapeDtypeStruct((bsz, T, D_MODEL), F32),
        jax.ShapeDtypeStruct((bsz, SUB, D_CONV), F32),
        jax.ShapeDtypeStruct((bsz, HEADS, DK, DV), F32),
    )
    out_specs = (
        pl.BlockSpec((1, tm, D_MODEL), lambda b, t: (b, t, 0)),
        pl.BlockSpec((1, SUB, D_CONV), lambda b, t: (b, 0, 0)),
        pl.BlockSpec((1, HEADS, DK, DV), lambda b, t: (b, 0, 0, 0)),
    )
    scratch = [
        pltpu.VMEM((2, tm, D_MODEL), BF16),
        pltpu.VMEM((tm + SUB, D_CONV), F32),
        pltpu.VMEM((HEADS, DV, DK), F32),
        pltpu.VMEM((tm, KDIM), F32),
        pltpu.VMEM((tm, KDIM), F32),
        pltpu.VMEM((tm, KDIM), F32),
        pltpu.VMEM((tm, VDIM), BF16),
        pltpu.VMEM((tm, VDIM), F32),
        pltpu.VMEM((tm, D_CONV), BF16),
        pltpu.VMEM((tm, D_MODEL), F32),
        pltpu.VMEM((tm, VDIM), F32),
        pltpu.VMEM((tm, D_MODEL), F32),
        pltpu.VMEM((tm, D_MODEL), F32),
    ]
    return pl.pallas_call(
        _mixer_seq_kernel,
        grid=(bsz, T // tm),
        in_specs=in_specs,
        out_specs=out_specs,
        out_shape=out_shape,
        scratch_shapes=scratch,
        compiler_params=pltpu.CompilerParams(
            dimension_semantics=("arbitrary", "arbitrary"),
            vmem_limit_bytes=VMEM_LIMIT),
        name="mixer_seq",
    )(x, x, cprev8, s0, *params, lmat)


def _mixer_step_kernel(x_ref, cp_ref, s_ref, wm_ref, wg_ref, wz_ref,
                       wa_ref, wb_ref, wo_ref, wgk_ref,
                       nmix_ref, convw_ref, bgk_ref, gain_ref, *rest):
    xo_ref, cnew_ref, sout_ref, qt_s, kt_s, at_s, v_s, o_s, ya_s = rest[-9:]
    i = pl.program_id(0)
    n = pl.num_programs(0)
    nseq = x_ref.shape[0]

    def xn_bf16():
        return _rms(x_ref[...], nmix_ref[...]).astype(BF16)

    @pl.when(i == 0)
    def _dense_in():
        xn = xn_bf16()
        u = _dot(xn, wm_ref[:, C_GC:C_GC + D_CONV]) * _dot(xn, wm_ref[:, C_H:C_H + D_CONV])
        cw = convw_ref[...]
        p1 = cp_ref[:, D_CONV:2 * D_CONV]
        yconv = cw[0:1] * cp_ref[:, 0:D_CONV] + cw[1:2] * p1 + cw[2:3] * u
        cnew_ref[:, 0:D_CONV] = p1
        cnew_ref[:, D_CONV:2 * D_CONV] = u
        ca = (_dot(xn, wm_ref[:, C_GB:C_GB + D_CONV]) * yconv).astype(BF16)
        ya_s[...] = _dot(ca, wa_ref[...])
        v_s[...] = _dot(xn, wm_ref[:, C_V:C_V + VDIM])
        qt_s[...] = (_dot(xn, wm_ref[:, C_Q:C_Q + KDIM]) * (DK ** -0.5)).T
        kt_s[...] = _dot(xn, wm_ref[:, C_K:C_K + KDIM]).T
        z = _dot(xn, wz_ref[...]).astype(BF16)
        g = _log_sigmoid(_dot(z, wgk_ref[...]) + bgk_ref[...]) * (1.0 / GATE_NORMALIZER)
        at_s[...] = jnp.exp(g).T

    shift = (nseq - i * SUB) % nseq
    qt = pltpu.roll(qt_s[...], shift, 1)
    kt = pltpu.roll(kt_s[...], shift, 1)
    at = pltpu.roll(at_s[...], shift, 1)
    for j in range(SUB):
        row = pl.ds(i * SUB + j, 1)
        for h in range(HEADS):
            rk = slice(h * DK, (h + 1) * DK)
            cv = slice(h * DV, (h + 1) * DV)
            sn = s_ref[j, h] * at[rk, j:j + 1] + kt[rk, j:j + 1] * v_s[row, cv]
            sout_ref[j, h] = sn
            o_s[row, cv] = jnp.sum(sn * qt[rk, j:j + 1], axis=0, keepdims=True)

    @pl.when(i == n - 1)
    def _dense_out():
        xn = xn_bf16()
        og = _dot(xn, wm_ref[:, C_OG:C_OG + VDIM])
        yb_in = (_head_norm(o_s[...], gain_ref[...]) * (og * _sigmoid(og))).astype(BF16)
        yb = _dot(yb_in, wb_ref[...])
        m = (_sigmoid(_dot(xn, wg_ref[:, 0:D_MODEL])) * ya_s[...]
             + _sigmoid(_dot(xn, wg_ref[:, D_MODEL:2 * D_MODEL])) * yb)
        xo_ref[...] = x_ref[...] + _dot(m.astype(BF16), wo_ref[...])


def _mixer_step(x, conv_all, s_all, s_out_prev, W, l):
    nseq = x.shape[0]
    assert nseq == LANES
    params = [W["w_main"], W["w_gate"], W["w_z"], W["w_a"], W["w_b"],
              W["w_o"], W["w_gk"], W["norm_mix"], W["conv_w"], W["b_gk"], W["gain"]]
    state_blk = (None, SUB, HEADS, DK, DV)
    in_specs = [
        _const_spec(x.shape),
        _layer_spec(conv_all, l),
        pl.BlockSpec(state_blk, lambda i: (l, i, 0, 0, 0)),
    ] + [_layer_spec(p, l) for p in params]
    args = [x, conv_all, s_all, *params]
    aliases = {}
    if s_out_prev is not None:
        in_specs.append(pl.BlockSpec(memory_space=pl.ANY))
        aliases = {len(args): 2}
        args.append(s_out_prev)
    out_shape = (
        jax.ShapeDtypeStruct(x.shape, F32),
        jax.ShapeDtypeStruct((nseq, 2 * D_CONV), F32),
        jax.ShapeDtypeStruct(s_all.shape, F32),
    )
    out_specs = (
        pl.BlockSpec(x.shape, lambda i: (0, 0)),
        pl.BlockSpec((nseq, 2 * D_CONV), lambda i: (0, 0)),
        pl.BlockSpec(state_blk, lambda i: (l, i, 0, 0, 0)),
    )
    scratch = [
        pltpu.VMEM((KDIM, nseq), F32),
        pltpu.VMEM((KDIM, nseq), F32),
        pltpu.VMEM((KDIM, nseq), F32),
        pltpu.VMEM((nseq, VDIM), F32),
        pltpu.VMEM((nseq, VDIM), F32),
        pltpu.VMEM((nseq, D_MODEL), F32),
    ]
    return pl.pallas_call(
        _mixer_step_kernel,
        grid=(nseq // SUB,),
        in_specs=in_specs,
        out_specs=out_specs,
        out_shape=out_shape,
        scratch_shapes=scratch,
        input_output_aliases=aliases,
        compiler_params=pltpu.CompilerParams(
            dimension_semantics=("arbitrary",),
            vmem_limit_bytes=VMEM_LIMIT),
        name="mixer_step",
    )(*args)


def _ffn_kernel(x_ref, nffn_ref, wgu_ref, wdn_ref, fin_ref, o_ref, *, final):
    x = x_ref[...]
    xn = _rms(x, nffn_ref[...]).astype(BF16)
    g = _dot(xn, wgu_ref[:, 0:D_FF])
    u = _dot(xn, wgu_ref[:, D_FF:2 * D_FF])
    hid = (g * _sigmoid(g) * u).astype(BF16)
    y = x + _dot(hid, wdn_ref[...])
    if final:
        y = _rms(y, fin_ref[...])
    o_ref[...] = y


def _ffn(x, W, l, final, tm):
    n = x.shape[0]
    assert n % tm == 0
    params = [W["norm_ffn"], W["w_gu"], W["w_down"]]
    fin = W["final_norm"]
    return pl.pallas_call(
        functools.partial(_ffn_kernel, final=final),
        grid=(n // tm,),
        in_specs=[pl.BlockSpec((tm, D_MODEL), lambda i: (i, 0))]
        + [_layer_spec(p, l) for p in params] + [_const_spec(fin.shape)],
        out_specs=pl.BlockSpec((tm, D_MODEL), lambda i: (i, 0)),
        out_shape=jax.ShapeDtypeStruct((n, D_MODEL), F32),
        compiler_params=pltpu.CompilerParams(
            dimension_semantics=("arbitrary",),
            vmem_limit_bytes=VMEM_LIMIT),
        name="ffn",
    )(x, *params, fin)


def _transpose_cast_kernel(wt_ref, o_ref):
    o_ref[...] = wt_ref[0].T.astype(BF16)


def _transpose_cast(wt, row0, ncols, rb=512):
    depth, _, d = wt.shape
    return pl.pallas_call(
        _transpose_cast_kernel,
        grid=(depth, ncols // rb),
        in_specs=[pl.BlockSpec((pl.Element(1), pl.Element(rb), pl.Element(d)),
                               lambda l, i: (l, pl.multiple_of(row0 + i * rb, SUB), 0))],
        out_specs=pl.BlockSpec((None, d, rb), lambda l, i: (l, 0, i)),
        out_shape=jax.ShapeDtypeStruct((depth, d, ncols), BF16),
        compiler_params=pltpu.CompilerParams(
            dimension_semantics=("arbitrary", "arbitrary"),
            vmem_limit_bytes=VMEM_LIMIT),
        name="transpose_cast",
    )(wt)


def _prep_weights(w_in, conv_w, w_gk2, b_gk2, gla_gain, w_a_out, w_b_out, w_o,
                  norm_mix, norm_ffn, w_gu, w_down, final_norm):
    depth = w_in.shape[0]
    w_in_t = jnp.swapaxes(w_in, 1, 2)
    w_main = _transpose_cast(w_in_t, 0, D_MAIN)
    w_gate = _transpose_cast(w_in_t, Z_HI, 2 * D_MODEL)
    w_z = jnp.pad(w_in[:, :, Z_LO:Z_HI], ((0, 0), (0, 0), (0, LANES - GATE_RANK))).astype(BF16)
    w_gk = jnp.pad(w_gk2, ((0, 0), (0, LANES - GATE_RANK), (0, 0))).astype(BF16)
    return {
        "w_main": w_main,
        "w_gate": w_gate,
        "w_z": w_z,
        "w_gk": w_gk,
        "b_gk": b_gk2.reshape(depth, 1, KDIM),
        "gain": gla_gain.reshape(depth, 1, DV),
        "conv_w": conv_w,
        "w_a": w_a_out.astype(BF16),
        "w_b": w_b_out.astype(BF16),
        "w_o": w_o.astype(BF16),
        "norm_mix": norm_mix.reshape(depth, 1, D_MODEL),
        "norm_ffn": norm_ffn.reshape(depth, 1, D_MODEL),
        "w_gu": w_gu.astype(BF16),
        "w_down": w_down.astype(BF16),
        "final_norm": final_norm.reshape(1, D_MODEL),
    }


def kernel(x_prompt, x_sample, state_conv, state_gla, meta_tokens, w_in, conv_w, w_gk2, b_gk2,
           gla_gain, w_a_out, w_b_out, w_o, norm_mix, norm_ffn, w_gu, w_down, final_norm):
    depth = w_in.shape[0]
    bsz, seq, _ = x_prompt.shape
    nsmp = x_sample.shape[0]
    W = _prep_weights(w_in, conv_w, w_gk2, b_gk2, gla_gain, w_a_out, w_b_out, w_o,
                      norm_mix, norm_ffn, w_gu, w_down, final_norm)

    xm = jnp.pad(meta_tokens.astype(F32), ((CHUNK - N_META, 0), (0, 0)))[None]
    xp = x_prompt
    xs = x_sample.reshape(nsmp, D_MODEL)
    conv_all = state_conv.reshape(depth, nsmp, (CONV_W - 1) * D_CONV)
    zero_conv = jnp.zeros((1, SUB, D_CONV), F32)
    zero_state = jnp.zeros((1, HEADS, DK, DV), F32)

    p_conv, p_gla, s_conv = [], [], []
    s_gla = None
    for l in range(depth):
        last = l == depth - 1
        xm, m_conv, m_state = _mixer_seq(xm, zero_conv, zero_state, W, l, CHUNK)
        xp, pc, ps = _mixer_seq(xp, m_conv, m_state, W, l, 256)
        xs, sc, s_gla = _mixer_step(xs, conv_all, state_gla, s_gla, W, l)
        if not last:
            xm = _ffn(xm[0], W, l, False, CHUNK)[None]
        xp = _ffn(xp.reshape(bsz * seq, D_MODEL), W, l, last, 512).reshape(bsz, seq, D_MODEL)
        xs = _ffn(xs, W, l, last, nsmp)
        p_conv.append(pc[:, SUB - (CONV_W - 1):, :])
        p_gla.append(ps)
        s_conv.append(sc.reshape(nsmp, CONV_W - 1, D_CONV))

    return (xp, xs.reshape(nsmp, 1, D_MODEL), jnp.stack(p_conv), jnp.stack(p_gla),
            jnp.stack(s_conv), s_gla)
```

```python
import functools

import jax
import jax.numpy as jnp
from jax import lax
from jax.experimental import pallas as pl
from jax.experimental.pallas import tpu as pltpu

F32 = jnp.float32
BF16 = jnp.bfloat16

D_MODEL = 1024
N_META = 16
D_CONV = D_MODEL
CONV_W = 3
HEADS = 4
DK = 128
DV = 256
KDIM = HEADS * DK
VDIM = HEADS * DV
GATE_RANK = 16
GATE_NORMALIZER = 16.0
D_FF = 2816
EPS = 1e-6
LOG2_E = 1.4426950408889634

CHUNK = 64
SUB = 8
NB = CHUNK // SUB
LANES = 128
COLB = 512
CUMSUM_ROWS = 256
FFN_ROW_BLOCK = 256
VMEM_LIMIT = 60 * 1024 * 1024

C_GB, C_GC, C_H = 0, 1024, 2048
C_Q, C_K, C_V, C_OG = 3072, 3584, 4096, 5120
D_MAIN = 6144
Z_LO = D_MAIN
Z_HI = Z_LO + GATE_RANK


def _rms(x, g):
    ms = jnp.mean(x * x, axis=-1, keepdims=True)
    return x * lax.rsqrt(ms + EPS) * g


def _log_sigmoid(x):
    return jnp.minimum(x, 0.0) - jnp.log(1.0 + jnp.exp(-jnp.abs(x)))


def _sigmoid(x):
    return 1.0 / (1.0 + jnp.exp(-x))


def _dot(a, b):
    return jnp.dot(a, b, preferred_element_type=F32)


def _dot_nt(a, b):
    return lax.dot_general(a, b, (((1,), (1,)), ((), ())), preferred_element_type=F32)


def _dot_tn(a, b):
    return lax.dot_general(a, b, (((0,), (0,)), ((), ())), preferred_element_type=F32)


def _split3(x):
    x1 = x.astype(BF16)
    r1 = x - x1.astype(F32)
    x2 = r1.astype(BF16)
    r2 = r1 - x2.astype(F32)
    return x1, x2, r2.astype(BF16)


def _head_norm(o, gain):
    return jnp.concatenate(
        [_rms(o[:, h * DV:(h + 1) * DV], gain) for h in range(HEADS)], axis=1)


def _mixer_seq_kernel(x_ref, xnext_ref, cprev_ref, s0_ref, wm_ref, wg_ref, wz_ref, wa_ref, wb_ref,
                      wo_ref, wgk_ref, nmix_ref, convw_ref, bgk_ref, gain_ref, lmat_ref,
                      xo_ref, cnew_ref, sout_ref,
                      xn_s, ubuf, st_s, q_s, k_s, b_s, v_s, o_s, ca_s, ya_s, og_s, ga_s, gb_s):
    t = pl.program_id(1)
    nt = pl.num_programs(1)
    tm = x_ref.shape[1]
    xn_buf = lax.bitwise_and(t, 1)
    xn_next_buf = lax.bitwise_and(t + 1, 1)

    @pl.when(t == 0)
    def _init():
        ubuf[0:SUB, :] = cprev_ref[0]
        xn_s[0] = _rms(x_ref[0], nmix_ref[...]).astype(BF16)
        for h in range(HEADS):
            st_s[h] = s0_ref[0, h].T

    xn = xn_s[xn_buf]

    def proj(w_ref, c0, n):
        return _dot(xn, w_ref[:, c0:c0 + n])

    def task_norm_next():
        xn_s[xn_next_buf] = _rms(xnext_ref[0], nmix_ref[...]).astype(BF16)

    z = proj(wz_ref, 0, LANES).astype(BF16)
    g = _log_sigmoid(_dot(z, wgk_ref[...]) + bgk_ref[...]) * (LOG2_E / GATE_NORMALIZER)
    g1, g2, g3 = _split3(g)
    q_s[...] = proj(wm_ref, C_Q, KDIM) * (DK ** -0.5)
    k_s[...] = proj(wm_ref, C_K, KDIM)
    v_s[...] = proj(wm_ref, C_V, VDIM).astype(BF16)
    cr = lmat_ref.shape[0]
    for r0 in range(0, tm, cr):
        b_s[r0:r0 + cr, :] = _dot(lmat_ref[...], jnp.concatenate(
            [g1[r0:r0 + cr], g2[r0:r0 + cr], g3[r0:r0 + cr]], axis=0))

    def cols_of(cb):
        return slice(cb * COLB, (cb + 1) * COLB)

    def task_u(cb):
        u = proj(wm_ref, C_GC + cb * COLB, COLB) * proj(wm_ref, C_H + cb * COLB, COLB)
        ubuf[SUB:SUB + tm, cols_of(cb)] = u

    def task_conv(cb):
        cols = cols_of(cb)
        cw = convw_ref[:, cols]
        yconv = (cw[0:1] * ubuf[SUB - 2:SUB - 2 + tm, cols]
                 + cw[1:2] * ubuf[SUB - 1:SUB - 1 + tm, cols]
                 + cw[2:3] * ubuf[SUB:SUB + tm, cols])
        ca_s[:, cols] = (proj(wm_ref, C_GB + cb * COLB, COLB) * yconv).astype(BF16)
        tail = ubuf[tm:tm + SUB, cols]
        ubuf[0:SUB, cols] = tail
        cnew_ref[0, :, cols] = tail

    def task_ya(cb):
        ya_s[:, cols_of(cb)] = _dot(ca_s[...], wa_ref[:, cols_of(cb)])

    def task_og(cb):
        og = proj(wm_ref, C_OG + cb * COLB, COLB)
        og_s[:, cols_of(cb)] = og * _sigmoid(og)

    def task_ga(cb):
        ga_s[:, cols_of(cb)] = _sigmoid(proj(wg_ref, cb * COLB, COLB))

    def task_gb(cb):
        gb_s[:, cols_of(cb)] = _sigmoid(proj(wg_ref, D_MODEL + cb * COLB, COLB))

    ncb = D_MODEL // COLB
    tasks = []
    for cb in range(ncb):
        tasks += [functools.partial(task_u, cb), functools.partial(task_conv, cb)]
    for fn in (task_ya, task_og, task_ga, task_gb):
        tasks += [functools.partial(fn, cb) for cb in range(ncb)]
    tasks.insert(2, task_norm_next)

    ti = lax.broadcasted_iota(jnp.int32, (CHUNK, CHUNK), 0)
    si = lax.broadcasted_iota(jnp.int32, (CHUNK, CHUNK), 1)
    diag_sel = [(si == ((ti >> 3) << 3) + j) & ((ti & 7) >= j) for j in range(SUB)]
    lane_blk = lax.broadcasted_iota(jnp.int32, (SUB, CHUNK), 1) >> 3
    stack_off = [sum(CHUNK - SUB * (i + 1) for i in range(j)) for j in range(NB)]

    def gla_diag(c, h):
        r0 = c * CHUNK
        rows = slice(r0, r0 + CHUNK)
        kcols = slice(h * DK, (h + 1) * DK)
        qc = q_s[rows, kcols]
        bc = b_s[rows, kcols]

        def row_of_each_block(ref, j):
            return jnp.concatenate(
                [jnp.broadcast_to(ref[r0 + SUB * i + j:r0 + SUB * i + j + 1, kcols], (SUB, DK))
                 for i in range(NB)], axis=0)

        acc = jnp.zeros((CHUNK, CHUNK), F32)
        for j in range(SUB):
            kj = row_of_each_block(k_s, j)
            bj = row_of_each_block(b_s, j)
            rj = jnp.sum(qc * kj * jnp.exp2(bc - bj), axis=-1, keepdims=True)
            acc = jnp.where(diag_sel[j], rj, acc)
        return acc

    def gla_rest(c, h, dm):
        r0 = c * CHUNK
        rows = slice(r0, r0 + CHUNK)
        kcols = slice(h * DK, (h + 1) * DK)
        vcols = slice(h * DV, (h + 1) * DV)
        qc = q_s[rows, kcols]
        kc = k_s[rows, kcols]
        bc = b_s[rows, kcols]
        vc = v_s[rows, vcols]
        st = st_s[h]
        bend = [b_s[r0 + SUB * j + SUB - 1:r0 + SUB * (j + 1), kcols] for j in range(NB)]
        bend_rows = jnp.concatenate([jnp.broadcast_to(e, (SUB, DK)) for e in bend], axis=0)
        blast = bend[NB - 1]
        qhat = (qc * jnp.exp2(bc)).astype(BF16)
        o = _dot_nt(qhat, st.astype(BF16))
        kt = (kc * jnp.exp2(bend_rows - bc)).astype(BF16)
        qs = [qc[SUB * (j + 1):] * jnp.exp2(bc[SUB * (j + 1):] - bend[j]) for j in range(NB - 1)]
        rm = _dot_nt(jnp.concatenate(qs, axis=0).astype(BF16), kt)
        arows = [jnp.zeros((SUB, CHUNK), F32)]
        for i in range(1, NB):
            acc = jnp.zeros((SUB, CHUNK), F32)
            for j in range(i):
                r0 = stack_off[j] + SUB * (i - j - 1)
                acc = jnp.where(lane_blk == j, rm[r0:r0 + SUB], acc)
            arows.append(acc)
        a = jnp.concatenate(arows, axis=0) + dm
        o_s[rows, vcols] = o + _dot(a.astype(BF16), vc)
        khat = (kc * jnp.exp2(blast - bc)).astype(BF16)
        st_s[h] = st * jnp.exp2(blast) + _dot_tn(vc, khat)

    nslots = (tm // CHUNK) * 2 * HEADS
    done = [0]

    def run_tasks(slot):
        upto = ((slot + 1) * len(tasks) + nslots - 1) // nslots
        while done[0] < upto:
            tasks[done[0]]()
            done[0] += 1

    slot = 0
    for c in range(tm // CHUNK):
        pcs = []
        for h in range(HEADS):
            pcs.append(gla_diag(c, h))
            run_tasks(slot)
            slot += 1
        for h in range(HEADS):
            gla_rest(c, h, pcs[h])
            run_tasks(slot)
            slot += 1

    yb_in = (_head_norm(o_s[...], gain_ref[...]) * og_s[...]).astype(BF16)
    yb = _dot(yb_in, wb_ref[...])
    m = ga_s[...] * ya_s[...] + gb_s[...] * yb
    xo_ref[0] = x_ref[0] + _dot(m.astype(BF16), wo_ref[...])

    @pl.when(t == nt - 1)
    def _fin():
        for h in range(HEADS):
            sout_ref[0, h] = st_s[h].T


def _const_spec(shape):
    nd = len(shape)
    return pl.BlockSpec(shape, lambda *_: (0,) * nd, pipeline_mode=pl.Buffered(1))


def _layer_spec(arr, l):
    nd = arr.ndim
    return pl.BlockSpec((None,) + arr.shape[1:], lambda *_: (l,) + (0,) * (nd - 1),
                        pipeline_mode=pl.Buffered(1))


def _mixer_seq(x, cprev8, s0, W, l, tm):
    bsz, T, _ = x.shape
    assert T % tm == 0 and tm % CHUNK == 0
    ridx = jnp.arange(min(tm, CUMSUM_ROWS))
    lmat =((ridx[:, None] // CHUNK == ridx[None, :] // CHUNK)
            & (ridx[None, :] <= ridx[:, None])).astype(BF16)
    lmat = jnp.concatenate([lmat, lmat, lmat], axis=1)
    params = [W["w_main"], W["w_gate"], W["w_z"], W["w_a"], W["w_b"], W["w_o"], W["w_gk"],
              W["norm_mix"], W["conv_w"], W["b_gk"], W["gain"]]
    nt = T // tm
    in_specs = [
        pl.BlockSpec((1, tm, D_MODEL), lambda b, t: (b, t, 0)),
        pl.BlockSpec((1, tm, D_MODEL), lambda b, t: (b, jnp.minimum(t + 1, nt - 1), 0)),
        pl.BlockSpec((1, SUB, D_CONV), lambda b, t: (0, 0, 0)),
        pl.BlockSpec((1, HEADS, DK, DV), lambda b, t: (0, 0, 0, 0)),
    ] + [_layer_spec(p, l) for p in params] + [_const_spec(lmat.shape)]
    out_shape = (
        jax.ShapeDtypeStruct((bsz, T, D_MODEL), F32),
        jax.ShapeDtypeStruct((bsz, SUB, D_CONV), F32),
        jax.ShapeDtypeStruct((bsz, HEADS, DK, DV), F32),
    )
    out_specs = (
        pl.BlockSpec((1, tm, D_MODEL), lambda b, t: (b, t, 0)),
        pl.BlockSpec((1, SUB, D_CONV), lambda b, t: (b, 0, 0)),
        pl.BlockSpec((1, HEADS, DK, DV), lambda b, t: (b, 0, 0, 0)),
    )
    scratch = [
        pltpu.VMEM((2, tm, D_MODEL), BF16),
        pltpu.VMEM((tm + SUB, D_CONV), F32),
        pltpu.VMEM((HEADS, DV, DK), F32),
        pltpu.VMEM((tm, KDIM), F32),
        pltpu.VMEM((tm, KDIM), F32),
        pltpu.VMEM((tm, KDIM), F32),
        pltpu.VMEM((tm, VDIM), BF16),
        pltpu.VMEM((tm, VDIM), F32),
        pltpu.VMEM((tm, D_CONV), BF16),
        pltpu.VMEM((tm, D_MODEL), F32),
        pltpu.VMEM((tm, VDIM), F32),
        pltpu.VMEM((tm, D_MODEL), F32),
        pltpu.VMEM((tm, D_MODEL), F32),
    ]
    return pl.pallas_call(
        _mixer_seq_kernel,
        grid=(bsz, T // tm),
        in_specs=in_specs,
        out_specs=out_specs,
        out_shape=out_shape,
        scratch_shapes=scratch,
        compiler_params=pltpu.CompilerParams(
            dimension_semantics=("arbitrary", "arbitrary"),
            vmem_limit_bytes=VMEM_LIMIT),
        name="mixer_seq",
    )(x, x, cprev8, s0, *params, lmat)


def _mixer_step_kernel(x_ref, cp_ref, s_ref, wm_ref, wg_ref, wz_ref,
                       wa_ref, wb_ref, wo_ref, wgk_ref,
                       nmix_ref, convw_ref, bgk_ref, gain_ref, *rest):
    xo_ref, cnew_ref, sout_ref, qt_s, kt_s, at_s, v_s, o_s, ya_s = rest[-9:]
    i = pl.program_id(0)
    n = pl.num_programs(0)
    nseq = x_ref.shape[0]

    def xn_bf16():
        return _rms(x_ref[...], nmix_ref[...]).astype(BF16)

    @pl.when(i == 0)
    def _dense_in():
        xn = xn_bf16()
        u = _dot(xn, wm_ref[:, C_GC:C_GC + D_CONV]) * _dot(xn, wm_ref[:, C_H:C_H + D_CONV])
        cw = convw_ref[...]
        p1 = cp_ref[:, D_CONV:2 * D_CONV]
        yconv = cw[0:1] * cp_ref[:, 0:D_CONV] + cw[1:2] * p1 + cw[2:3] * u
        cnew_ref[:, 0:D_CONV] = p1
        cnew_ref[:, D_CONV:2 * D_CONV] = u
        ca = (_dot(xn, wm_ref[:, C_GB:C_GB + D_CONV]) * yconv).astype(BF16)
        ya_s[...] = _dot(ca, wa_ref[...])
        v_s[...] = _dot(xn, wm_ref[:, C_V:C_V + VDIM])
        qt_s[...] = (_dot(xn, wm_ref[:, C_Q:C_Q + KDIM]) * (DK ** -0.5)).T
        kt_s[...] = _dot(xn, wm_ref[:, C_K:C_K + KDIM]).T
        z = _dot(xn, wz_ref[...]).astype(BF16)
        g = _log_sigmoid(_dot(z, wgk_ref[...]) + bgk_ref[...]) * (1.0 / GATE_NORMALIZER)
        at_s[...] = jnp.exp(g).T

    shift = (nseq - i * SUB) % nseq
    qt = pltpu.roll(qt_s[...], shift, 1)
    kt = pltpu.roll(kt_s[...], shift, 1)
    at = pltpu.roll(at_s[...], shift, 1)
    for j in range(SUB):
        row = pl.ds(i * SUB + j, 1)
        for h in range(HEADS):
            rk = slice(h * DK, (h + 1) * DK)
            cv = slice(h * DV, (h + 1) * DV)
            sn = s_ref[j, h] * at[rk, j:j + 1] + kt[rk, j:j + 1] * v_s[row, cv]
            sout_ref[j, h] = sn
            o_s[row, cv] = jnp.sum(sn * qt[rk, j:j + 1], axis=0, keepdims=True)

    @pl.when(i == n - 1)
    def _dense_out():
        xn = xn_bf16()
        og = _dot(xn, wm_ref[:, C_OG:C_OG + VDIM])
        yb_in = (_head_norm(o_s[...], gain_ref[...]) * (og * _sigmoid(og))).astype(BF16)
        yb = _dot(yb_in, wb_ref[...])
        m = (_sigmoid(_dot(xn, wg_ref[:, 0:D_MODEL])) * ya_s[...]
             + _sigmoid(_dot(xn, wg_ref[:, D_MODEL:2 * D_MODEL])) * yb)
        xo_ref[...] = x_ref[...] + _dot(m.astype(BF16), wo_ref[...])


def _mixer_step(x, conv_all, s_all, s_out_prev, W, l):
    nseq = x.shape[0]
    assert nseq == LANES
    params = [W["w_main"], W["w_gate"], W["w_z"], W["w_a"], W["w_b"],
              W["w_o"], W["w_gk"], W["norm_mix"], W["conv_w"], W["b_gk"], W["gain"]]
    state_blk = (None, SUB, HEADS, DK, DV)
    in_specs = [
        _const_spec(x.shape),
        _layer_spec(conv_all, l),
        pl.BlockSpec(state_blk, lambda i: (l, i, 0, 0, 0)),
    ] + [_layer_spec(p, l) for p in params]
    args = [x, conv_all, s_all, *params]
    aliases = {}
    if s_out_prev is not None:
        in_specs.append(pl.BlockSpec(memory_space=pl.ANY))
        aliases = {len(args): 2}
        args.append(s_out_prev)
    out_shape = (
        jax.ShapeDtypeStruct(x.shape, F32),
        jax.ShapeDtypeStruct((nseq, 2 * D_CONV), F32),
        jax.ShapeDtypeStruct(s_all.shape, F32),
    )
    out_specs = (
        pl.BlockSpec(x.shape, lambda i: (0, 0)),
        pl.BlockSpec((nseq, 2 * D_CONV), lambda i: (0, 0)),
        pl.BlockSpec(state_blk, lambda i: (l, i, 0, 0, 0)),
    )
    scratch = [
        pltpu.VMEM((KDIM, nseq), F32),
        pltpu.VMEM((KDIM, nseq), F32),
        pltpu.VMEM((KDIM, nseq), F32),
        pltpu.VMEM((nseq, VDIM), F32),
        pltpu.VMEM((nseq, VDIM), F32),
        pltpu.VMEM((nseq, D_MODEL), F32),
    ]
    return pl.pallas_call(
        _mixer_step_kernel,
        grid=(nseq // SUB,),
        in_specs=in_specs,
        out_specs=out_specs,
        out_shape=out_shape,
        scratch_shapes=scratch,
        input_output_aliases=aliases,
        compiler_params=pltpu.CompilerParams(
            dimension_semantics=("arbitrary",),
            vmem_limit_bytes=VMEM_LIMIT),
        name="mixer_step",
    )(*args)


def _ffn_kernel(x_ref, nffn_ref, wgu_ref, wdn_ref, fin_ref, o_ref, *, final, rb):
    for r0 in range(0, x_ref.shape[0], rb):
        x = x_ref[r0:r0 + rb, :]
        xn = _rms(x, nffn_ref[...]).astype(BF16)
        g = _dot(xn, wgu_ref[:, 0:D_FF])
        u = _dot(xn, wgu_ref[:, D_FF:2 * D_FF])
        hid = (g * _sigmoid(g) * u).astype(BF16)
        y = x + _dot(hid, wdn_ref[...])
        if final:
            y = _rms(y, fin_ref[...])
        o_ref[r0:r0 + rb, :] = y


def _ffn(x, W, l, final, tm):
    n = x.shape[0]
    assert n % tm == 0
    params = [W["norm_ffn"], W["w_gu"], W["w_down"]]
    fin = W["final_norm"]
    return pl.pallas_call(
        functools.partial(_ffn_kernel, final=final, rb=min(tm, FFN_ROW_BLOCK)),
        grid=(n // tm,),
        in_specs=[pl.BlockSpec((tm, D_MODEL), lambda i: (i, 0))]
        + [_layer_spec(p, l) for p in params] + [_const_spec(fin.shape)],
        out_specs=pl.BlockSpec((tm, D_MODEL), lambda i: (i, 0)),
        out_shape=jax.ShapeDtypeStruct((n, D_MODEL), F32),
        compiler_params=pltpu.CompilerParams(
            dimension_semantics=("arbitrary",),
            vmem_limit_bytes=VMEM_LIMIT),
        name="ffn",
    )(x, *params, fin)


def _transpose_cast_kernel(wt_ref, o_ref):
    o_ref[...] = wt_ref[0].T.astype(BF16)


def _transpose_cast(wt, row0, ncols, rb=512):
    depth, _, d = wt.shape
    return pl.pallas_call(
        _transpose_cast_kernel,
        grid=(depth, ncols // rb),
        in_specs=[pl.BlockSpec((pl.Element(1), pl.Element(rb), pl.Element(d)),
                               lambda l, i: (l, pl.multiple_of(row0 + i * rb, SUB), 0))],
        out_specs=pl.BlockSpec((None, d, rb), lambda l, i: (l, 0, i)),
        out_shape=jax.ShapeDtypeStruct((depth, d, ncols), BF16),
        compiler_params=pltpu.CompilerParams(
            dimension_semantics=("arbitrary", "arbitrary"),
            vmem_limit_bytes=VMEM_LIMIT),
        name="transpose_cast",
    )(wt)


def _prep_weights(w_in, conv_w, w_gk2, b_gk2, gla_gain, w_a_out, w_b_out, w_o,
                  norm_mix, norm_ffn, w_gu, w_down, final_norm):
    depth = w_in.shape[0]
    w_in_t = jnp.swapaxes(w_in, 1, 2)
    w_main = _transpose_cast(w_in_t, 0, D_MAIN)
    w_gate = _transpose_cast(w_in_t, Z_HI, 2 * D_MODEL)
    w_z = jnp.pad(w_in[:, :, Z_LO:Z_HI], ((0, 0), (0, 0), (0, LANES - GATE_RANK))).astype(BF16)
    w_gk = jnp.pad(w_gk2, ((0, 0), (0, LANES - GATE_RANK), (0, 0))).astype(BF16)
    return {
        "w_main": w_main,
        "w_gate": w_gate,
        "w_z": w_z,
        "w_gk": w_gk,
        "b_gk": b_gk2.reshape(depth, 1, KDIM),
        "gain": gla_gain.reshape(depth, 1, DV),
        "conv_w": conv_w,
        "w_a": w_a_out.astype(BF16),
        "w_b": w_b_out.astype(BF16),
        "w_o": w_o.astype(BF16),
        "norm_mix": norm_mix.reshape(depth, 1, D_MODEL),
        "norm_ffn": norm_ffn.reshape(depth, 1, D_MODEL),
        "w_gu": w_gu.astype(BF16),
        "w_down": w_down.astype(BF16),
        "final_norm": final_norm.reshape(1, D_MODEL),
    }


def kernel(x_prompt, x_sample, state_conv, state_gla, meta_tokens, w_in, conv_w, w_gk2, b_gk2,
           gla_gain, w_a_out, w_b_out, w_o, norm_mix, norm_ffn, w_gu, w_down, final_norm):
    depth = w_in.shape[0]
    bsz, seq, _ = x_prompt.shape
    nsmp = x_sample.shape[0]
    W = _prep_weights(w_in, conv_w, w_gk2, b_gk2, gla_gain, w_a_out, w_b_out, w_o,
                      norm_mix, norm_ffn, w_gu, w_down, final_norm)

    xm = jnp.pad(meta_tokens.astype(F32), ((CHUNK - N_META, 0), (0, 0)))[None]
    xp = x_prompt
    xs = x_sample.reshape(nsmp, D_MODEL)
    conv_all = state_conv.reshape(depth, nsmp, (CONV_W - 1) * D_CONV)
    zero_conv = jnp.zeros((1, SUB, D_CONV), F32)
    zero_state = jnp.zeros((1, HEADS, DK, DV), F32)

    p_conv, p_gla, s_conv = [], [], []
    s_gla = None
    for l in range(depth):
        last = l == depth - 1
        xm, m_conv, m_state = _mixer_seq(xm, zero_conv, zero_state, W, l, CHUNK)
        xp, pc, ps = _mixer_seq(xp, m_conv, m_state, W, l, 256)
        xs, sc, s_gla = _mixer_step(xs, conv_all, state_gla, s_gla, W, l)
        if not last:
            xm = _ffn(xm[0], W, l, False, CHUNK)[None]
        xp = _ffn(xp.reshape(bsz * seq, D_MODEL), W, l, last, 1024).reshape(bsz, seq, D_MODEL)
        xs = _ffn(xs, W, l, last, nsmp)
        p_conv.append(pc[:, SUB - (CONV_W - 1):, :])
        p_gla.append(ps)
        s_conv.append(sc.reshape(nsmp, CONV_W - 1, D_CONV))

    return (xp, xs.reshape(nsmp, 1, D_MODEL), jnp.stack(p_conv), jnp.stack(p_gla),
            jnp.stack(s_conv), s_gla)
```

```python
import functools

import jax
import jax.numpy as jnp
from jax import lax
from jax.experimental import pallas as pl
from jax.experimental.pallas import tpu as pltpu

F32 = jnp.float32
BF16 = jnp.bfloat16

D_MODEL = 1024
N_META = 16
D_CONV = D_MODEL
CONV_W = 3
HEADS = 4
DK = 128
DV = 256
KDIM = HEADS * DK
VDIM = HEADS * DV
GATE_RANK = 16
GATE_NORMALIZER = 16.0
D_FF = 2816
EPS = 1e-6
LOG2_E = 1.4426950408889634

CHUNK = 64
SUB = 8
NB = CHUNK // SUB
LANES = 128
COLB = 512
CUMSUM_ROWS = 256
FFN_ROW_BLOCK = 256
PROMPT_TILE = 256
SEQS_PER_STEP = 1
VMEM_LIMIT = 60 * 1024 * 1024

C_GB, C_GC, C_H = 0, 1024, 2048
C_Q, C_K, C_V, C_OG = 3072, 3584, 4096, 5120
D_MAIN = 6144
Z_LO = D_MAIN
Z_HI = Z_LO + GATE_RANK


def _rms(x, g):
    ms = jnp.mean(x * x, axis=-1, keepdims=True)
    return x * lax.rsqrt(ms + EPS) * g


def _log_sigmoid(x):
    return jnp.minimum(x, 0.0) - jnp.log(1.0 + jnp.exp(-jnp.abs(x)))


def _sigmoid(x):
    return 1.0 / (1.0 + jnp.exp(-x))


def _dot(a, b):
    return jnp.dot(a, b, preferred_element_type=F32)


def _dot_nt(a, b):
    return lax.dot_general(a, b, (((1,), (1,)), ((), ())), preferred_element_type=F32)


def _dot_tn(a, b):
    return lax.dot_general(a, b, (((0,), (0,)), ((), ())), preferred_element_type=F32)


def _split3(x):
    x1 = x.astype(BF16)
    r1 = x - x1.astype(F32)
    x2 = r1.astype(BF16)
    r2 = r1 - x2.astype(F32)
    return x1, x2, r2.astype(BF16)


def _head_norm(o, gain):
    return jnp.concatenate(
        [_rms(o[:, h * DV:(h + 1) * DV], gain) for h in range(HEADS)], axis=1)


def _interleave(slots, pieces):
    done = 0
    for i, slot_fn in enumerate(slots):
        slot_fn()
        upto = ((i + 1) * len(pieces) + len(slots) - 1) // len(slots)
        while done < upto:
            pieces[done]()
            done += 1


def _mixer_seq_kernel(x_ref, xnext_ref, cprev_ref, s0_ref, wm_ref, wg_ref, wz_ref, wa_ref, wb_ref,
                      wo_ref, wgk_ref, nmix_ref, convw_ref, bgk_ref, gain_ref, lmat_ref,
                      xo_ref, cnew_ref, sout_ref,
                      xn_s, ubuf, st_s, q_s, k_s, b_s, v_s, o_s, ca_s, ya_s, og_s, ga_s, gb_s):
    t = pl.program_id(1)
    nt = pl.num_programs(1)
    nseq, tm = x_ref.shape[0], x_ref.shape[1]
    xn_buf = lax.bitwise_and(t, 1)
    xn_next_buf = lax.bitwise_and(t + 1, 1)

    @pl.when(t == 0)
    def _init():
        for p in range(nseq):
            ubuf[p, 0:SUB, :] = cprev_ref[0]
            xn_s[p, 0] = _rms(x_ref[p], nmix_ref[...]).astype(BF16)
            for h in range(HEADS):
                st_s[p, h] = s0_ref[0, h].T

    ti = lax.broadcasted_iota(jnp.int32, (CHUNK, CHUNK), 0)
    si = lax.broadcasted_iota(jnp.int32, (CHUNK, CHUNK), 1)
    diag_sel = [(si == ((ti >> 3) << 3) + j) & ((ti & 7) >= j) for j in range(SUB)]
    lane_blk = lax.broadcasted_iota(jnp.int32, (SUB, CHUNK), 1) >> 3
    stack_off = [sum(CHUNK - SUB * (i + 1) for i in range(j)) for j in range(NB)]
    ncb = D_MODEL // COLB

    def cols_of(cb):
        return slice(cb * COLB, (cb + 1) * COLB)

    def make_stream(p):
        xn = xn_s[p, xn_buf]
        vals = {}

        def proj(w_ref, c0, n):
            return _dot(xn, w_ref[:, c0:c0 + n])

        def head_gate():
            z = proj(wz_ref, 0, LANES).astype(BF16)
            g = (_log_sigmoid(_dot(z, wgk_ref[...]) + bgk_ref[...])
                 * (LOG2_E / GATE_NORMALIZER))
            vals["g"] = _split3(g)

        def head_q():
            q_s[p] = proj(wm_ref, C_Q, KDIM) * (DK ** -0.5)

        def head_k():
            k_s[p] = proj(wm_ref, C_K, KDIM)

        def head_v():
            v_s[p] = proj(wm_ref, C_V, VDIM).astype(BF16)

        def head_cumsum():
            g1, g2, g3 = vals.pop("g")
            cr = lmat_ref.shape[0]
            for r0 in range(0, tm, cr):
                b_s[p, r0:r0 + cr, :] = _dot(lmat_ref[...], jnp.concatenate(
                    [g1[r0:r0 + cr], g2[r0:r0 + cr], g3[r0:r0 + cr]], axis=0))

        head = [head_gate, head_q, head_k, head_v, head_cumsum]

        def mid_u(cb):
            u = proj(wm_ref, C_GC + cb * COLB, COLB) * proj(wm_ref, C_H + cb * COLB, COLB)
            ubuf[p, SUB:SUB + tm, cols_of(cb)] = u

        def mid_conv(cb):
            cols = cols_of(cb)
            cw = convw_ref[:, cols]
            yconv = (cw[0:1] * ubuf[p, SUB - 2:SUB - 2 + tm, cols]
                     + cw[1:2] * ubuf[p, SUB - 1:SUB - 1 + tm, cols]
                     + cw[2:3] * ubuf[p, SUB:SUB + tm, cols])
            ca_s[p, :, cols] = (proj(wm_ref, C_GB + cb * COLB, COLB) * yconv).astype(BF16)
            tail = ubuf[p, tm:tm + SUB, cols]
            ubuf[p, 0:SUB, cols] = tail
            cnew_ref[p, :, cols] = tail

        def mid_ya(cb):
            ya_s[p, :, cols_of(cb)] = _dot(ca_s[p], wa_ref[:, cols_of(cb)])

        def mid_norm_next():
            xn_s[p, xn_next_buf] = _rms(xnext_ref[p], nmix_ref[...]).astype(BF16)

        def mid_og(cb):
            og = proj(wm_ref, C_OG + cb * COLB, COLB)
            og_s[p, :, cols_of(cb)] = og * _sigmoid(og)

        def mid_ga(cb):
            ga_s[p, :, cols_of(cb)] = _sigmoid(proj(wg_ref, cb * COLB, COLB))

        def mid_gb(cb):
            gb_s[p, :, cols_of(cb)] = _sigmoid(proj(wg_ref, D_MODEL + cb * COLB, COLB))

        mid = []
        for cb in range(ncb):
            mid += [functools.partial(mid_u, cb), functools.partial(mid_conv, cb)]
        mid.insert(2, mid_norm_next)
        for fn in (mid_ya, mid_og, mid_ga, mid_gb):
            mid += [functools.partial(fn, cb) for cb in range(ncb)]

        def gla_diag(c, h):
            r0 = c * CHUNK
            rows = slice(r0, r0 + CHUNK)
            kcols = slice(h * DK, (h + 1) * DK)
            qc = q_s[p, rows, kcols]
            bc = b_s[p, rows, kcols]

            def row_of_each_block(ref, j):
                return jnp.concatenate(
                    [jnp.broadcast_to(ref[p, r0 + SUB * i + j:r0 + SUB * i + j + 1, kcols],
                                      (SUB, DK)) for i in range(NB)], axis=0)

            acc = jnp.zeros((CHUNK, CHUNK), F32)
            for j in range(SUB):
                kj = row_of_each_block(k_s, j)
                bj = row_of_each_block(b_s, j)
                rj = jnp.sum(qc * kj * jnp.exp2(bc - bj), axis=-1, keepdims=True)
                acc = jnp.where(diag_sel[j], rj, acc)
            vals[("diag", c, h)] = acc

        def gla_rest(c, h):
            r0 = c * CHUNK
            rows = slice(r0, r0 + CHUNK)
            kcols = slice(h * DK, (h + 1) * DK)
            vcols = slice(h * DV, (h + 1) * DV)
            qc = q_s[p, rows, kcols]
            kc = k_s[p, rows, kcols]
            bc = b_s[p, rows, kcols]
            vc = v_s[p, rows, vcols]
            st = st_s[p, h]
            bend = [b_s[p, r0 + SUB * j + SUB - 1:r0 + SUB * (j + 1), kcols] for j in range(NB)]
            bend_rows = jnp.concatenate([jnp.broadcast_to(e, (SUB, DK)) for e in bend], axis=0)
            blast = bend[NB - 1]
            qhat = (qc * jnp.exp2(bc)).astype(BF16)
            o = _dot_nt(qhat, st.astype(BF16))
            kt = (kc * jnp.exp2(bend_rows - bc)).astype(BF16)
            qs = [qc[SUB * (j + 1):] * jnp.exp2(bc[SUB * (j + 1):] - bend[j])
                  for j in range(NB - 1)]
            rm = _dot_nt(jnp.concatenate(qs, axis=0).astype(BF16), kt)
            arows = [jnp.zeros((SUB, CHUNK), F32)]
            for i in range(1, NB):
                acc = jnp.zeros((SUB, CHUNK), F32)
                for j in range(i):
                    s0 = stack_off[j] + SUB * (i - j - 1)
                    acc = jnp.where(lane_blk == j, rm[s0:s0 + SUB], acc)
                arows.append(acc)
            a = jnp.concatenate(arows, axis=0) + vals.pop(("diag", c, h))
            o_s[p, rows, vcols] = o + _dot(a.astype(BF16), vc)
            khat = (kc * jnp.exp2(blast - bc)).astype(BF16)
            st_s[p, h] = st * jnp.exp2(blast) + _dot_tn(vc, khat)

        gla = []
        for c in range(tm // CHUNK):
            for h in range(HEADS):
                gla += [functools.partial(gla_diag, c, h), functools.partial(gla_rest, c, h)]

        def tail_yb(rs):
            yb_in = (_head_norm(o_s[p, rs, :], gain_ref[...]) * og_s[p, rs, :]).astype(BF16)
            vals[("yb", rs.start)] = _dot(yb_in, wb_ref[...])

        def tail_out(rs):
            m = ga_s[p, rs, :] * ya_s[p, rs, :] + gb_s[p, rs, :] * vals.pop(("yb", rs.start))
            xo_ref[p, rs, :] = x_ref[p, rs, :] + _dot(m.astype(BF16), wo_ref[...])

        def tail(rs):
            return [functools.partial(tail_yb, rs), functools.partial(tail_out, rs)]

        return head, mid, gla, tail

    for p in range(nseq):
        head, mid, gla, tail = make_stream(p)
        for fn in head:
            fn()
        _interleave(gla, mid)
        for fn in tail(slice(0, tm)):
            fn()

    @pl.when(t == nt - 1)
    def _fin():
        for p in range(nseq):
            for h in range(HEADS):
                sout_ref[p, h] = st_s[p, h].T


def _const_spec(shape):
    nd = len(shape)
    return pl.BlockSpec(shape, lambda *_: (0,) * nd, pipeline_mode=pl.Buffered(1))


def _layer_spec(arr, l):
    nd = arr.ndim
    return pl.BlockSpec((None,) + arr.shape[1:], lambda *_: (l,) + (0,) * (nd - 1),
                        pipeline_mode=pl.Buffered(1))


def _mixer_seq(x, cprev8, s0, W, l, tm):
    bsz, T, _ = x.shape
    assert T % tm == 0 and tm % CHUNK == 0
    ns = SEQS_PER_STEP if bsz % SEQS_PER_STEP == 0 else 1
    ridx = jnp.arange(min(tm, CUMSUM_ROWS))
    lmat = ((ridx[:, None] // CHUNK == ridx[None, :] // CHUNK)
            & (ridx[None, :] <= ridx[:, None])).astype(BF16)
    lmat = jnp.concatenate([lmat, lmat, lmat], axis=1)
    params = [W["w_main"], W["w_gate"], W["w_z"], W["w_a"], W["w_b"], W["w_o"], W["w_gk"],
              W["norm_mix"], W["conv_w"], W["b_gk"], W["gain"]]
    nt = T // tm
    in_specs = [
        pl.BlockSpec((ns, tm, D_MODEL), lambda b, t: (b, t, 0)),
        pl.BlockSpec((ns, tm, D_MODEL), lambda b, t: (b, jnp.minimum(t + 1, nt - 1), 0)),
        pl.BlockSpec((1, SUB, D_CONV), lambda b, t: (0, 0, 0)),
        pl.BlockSpec((1, HEADS, DK, DV), lambda b, t: (0, 0, 0, 0)),
    ] + [_layer_spec(p, l) for p in params] + [_const_spec(lmat.shape)]
    out_shape = (
        jax.ShapeDtypeStruct((bsz, T, D_MODEL), F32),
        jax.ShapeDtypeStruct((bsz, SUB, D_CONV), F32),
        jax.ShapeDtypeStruct((bsz, HEADS, DK, DV), F32),
    )
    out_specs = (
        pl.BlockSpec((ns, tm, D_MODEL), lambda b, t: (b, t, 0)),
        pl.BlockSpec((ns, SUB, D_CONV), lambda b, t: (b, 0, 0)),
        pl.BlockSpec((ns, HEADS, DK, DV), lambda b, t: (b, 0, 0, 0)),
    )
    scratch = [
        pltpu.VMEM((ns, 2, tm, D_MODEL), BF16),
        pltpu.VMEM((ns, tm + SUB, D_CONV), F32),
        pltpu.VMEM((ns, HEADS, DV, DK), F32),
        pltpu.VMEM((ns, tm, KDIM), F32),
        pltpu.VMEM((ns, tm, KDIM), F32),
        pltpu.VMEM((ns, tm, KDIM), F32),
        pltpu.VMEM((ns, tm, VDIM), BF16),
        pltpu.VMEM((ns, tm, VDIM), F32),
        pltpu.VMEM((ns, tm, D_CONV), BF16),
        pltpu.VMEM((ns, tm, D_MODEL), F32),
        pltpu.VMEM((ns, tm, VDIM), F32),
        pltpu.VMEM((ns, tm, D_MODEL), F32),
        pltpu.VMEM((ns, tm, D_MODEL), F32),
    ]
    return pl.pallas_call(
        _mixer_seq_kernel,
        grid=(bsz // ns, nt),
        in_specs=in_specs,
        out_specs=out_specs,
        out_shape=out_shape,
        scratch_shapes=scratch,
        compiler_params=pltpu.CompilerParams(
            dimension_semantics=("arbitrary", "arbitrary"),
            vmem_limit_bytes=VMEM_LIMIT),
        name="mixer_seq",
    )(x, x, cprev8, s0, *params, lmat)


def _mixer_step_kernel(x_ref, cp_ref, s_ref, wm_ref, wg_ref, wz_ref,
                       wa_ref, wb_ref, wo_ref, wgk_ref,
                       nmix_ref, convw_ref, bgk_ref, gain_ref, *rest):
    xo_ref, cnew_ref, sout_ref, qt_s, kt_s, at_s, v_s, o_s, ya_s = rest[-9:]
    i = pl.program_id(0)
    n = pl.num_programs(0)
    nseq = x_ref.shape[0]

    def xn_bf16():
        return _rms(x_ref[...], nmix_ref[...]).astype(BF16)

    @pl.when(i == 0)
    def _dense_in():
        xn = xn_bf16()
        u = _dot(xn, wm_ref[:, C_GC:C_GC + D_CONV]) * _dot(xn, wm_ref[:, C_H:C_H + D_CONV])
        cw = convw_ref[...]
        p1 = cp_ref[:, D_CONV:2 * D_CONV]
        yconv = cw[0:1] * cp_ref[:, 0:D_CONV] + cw[1:2] * p1 + cw[2:3] * u
        cnew_ref[:, 0:D_CONV] = p1
        cnew_ref[:, D_CONV:2 * D_CONV] = u
        ca = (_dot(xn, wm_ref[:, C_GB:C_GB + D_CONV]) * yconv).astype(BF16)
        ya_s[...] = _dot(ca, wa_ref[...])
        v_s[...] = _dot(xn, wm_ref[:, C_V:C_V + VDIM])
        qt_s[...] = (_dot(xn, wm_ref[:, C_Q:C_Q + KDIM]) * (DK ** -0.5)).T
        kt_s[...] = _dot(xn, wm_ref[:, C_K:C_K + KDIM]).T
        z = _dot(xn, wz_ref[...]).astype(BF16)
        g = _log_sigmoid(_dot(z, wgk_ref[...]) + bgk_ref[...]) * (1.0 / GATE_NORMALIZER)
        at_s[...] = jnp.exp(g).T

    shift = (nseq - i * SUB) % nseq
    qt = pltpu.roll(qt_s[...], shift, 1)
    kt = pltpu.roll(kt_s[...], shift, 1)
    at = pltpu.roll(at_s[...], shift, 1)
    for j in range(SUB):
        row = pl.ds(i * SUB + j, 1)
        for h in range(HEADS):
            rk = slice(h * DK, (h + 1) * DK)
            cv = slice(h * DV, (h + 1) * DV)
            sn = s_ref[j, h] * at[rk, j:j + 1] + kt[rk, j:j + 1] * v_s[row, cv]
            sout_ref[j, h] = sn
            o_s[row, cv] = jnp.sum(sn * qt[rk, j:j + 1], axis=0, keepdims=True)

    @pl.when(i == n - 1)
    def _dense_out():
        xn = xn_bf16()
        og = _dot(xn, wm_ref[:, C_OG:C_OG + VDIM])
        yb_in = (_head_norm(o_s[...], gain_ref[...]) * (og * _sigmoid(og))).astype(BF16)
        yb = _dot(yb_in, wb_ref[...])
        m = (_sigmoid(_dot(xn, wg_ref[:, 0:D_MODEL])) * ya_s[...]
             + _sigmoid(_dot(xn, wg_ref[:, D_MODEL:2 * D_MODEL])) * yb)
        xo_ref[...] = x_ref[...] + _dot(m.astype(BF16), wo_ref[...])


def _mixer_step(x, conv_all, s_all, s_out_prev, W, l):
    nseq = x.shape[0]
    assert nseq == LANES
    params = [W["w_main"], W["w_gate"], W["w_z"], W["w_a"], W["w_b"],
              W["w_o"], W["w_gk"], W["norm_mix"], W["conv_w"], W["b_gk"], W["gain"]]
    state_blk = (None, SUB, HEADS, DK, DV)
    in_specs = [
        _const_spec(x.shape),
        _layer_spec(conv_all, l),
        pl.BlockSpec(state_blk, lambda i: (l, i, 0, 0, 0)),
    ] + [_layer_spec(p, l) for p in params]
    args = [x, conv_all, s_all, *params]
    aliases = {}
    if s_out_prev is not None:
        in_specs.append(pl.BlockSpec(memory_space=pl.ANY))
        aliases = {len(args): 2}
        args.append(s_out_prev)
    out_shape = (
        jax.ShapeDtypeStruct(x.shape, F32),
        jax.ShapeDtypeStruct((nseq, 2 * D_CONV), F32),
        jax.ShapeDtypeStruct(s_all.shape, F32),
    )
    out_specs = (
        pl.BlockSpec(x.shape, lambda i: (0, 0)),
        pl.BlockSpec((nseq, 2 * D_CONV), lambda i: (0, 0)),
        pl.BlockSpec(state_blk, lambda i: (l, i, 0, 0, 0)),
    )
    scratch = [
        pltpu.VMEM((KDIM, nseq), F32),
        pltpu.VMEM((KDIM, nseq), F32),
        pltpu.VMEM((KDIM, nseq), F32),
        pltpu.VMEM((nseq, VDIM), F32),
        pltpu.VMEM((nseq, VDIM), F32),
        pltpu.VMEM((nseq, D_MODEL), F32),
    ]
    return pl.pallas_call(
        _mixer_step_kernel,
        grid=(nseq // SUB,),
        in_specs=in_specs,
        out_specs=out_specs,
        out_shape=out_shape,
        scratch_shapes=scratch,
        input_output_aliases=aliases,
        compiler_params=pltpu.CompilerParams(
            dimension_semantics=("arbitrary",),
            vmem_limit_bytes=VMEM_LIMIT),
        name="mixer_step",
    )(*args)


def _ffn_kernel(*refs, final, rb, n_small):
    x_ref, small_refs = refs[0], refs[1:1 + n_small]
    nffn_ref, wgu_ref, wdn_ref, fin_ref = refs[1 + n_small:5 + n_small]
    o_ref, small_out_refs = refs[5 + n_small], refs[6 + n_small:]

    def block(src_ref, dst_ref, r0, rows):
        x = src_ref[r0:r0 + rows, :]
        xn = _rms(x, nffn_ref[...]).astype(BF16)
        g = _dot(xn, wgu_ref[:, 0:D_FF])
        u = _dot(xn, wgu_ref[:, D_FF:2 * D_FF])
        hid = (g * _sigmoid(g) * u).astype(BF16)
        y = x + _dot(hid, wdn_ref[...])
        if final:
            y = _rms(y, fin_ref[...])
        dst_ref[r0:r0 + rows, :] = y

    for r0 in range(0, x_ref.shape[0], rb):
        block(x_ref, o_ref, r0, rb)

    @pl.when(pl.program_id(0) == pl.num_programs(0) - 1)
    def _small_groups():
        for src_ref, dst_ref in zip(small_refs, small_out_refs):
            block(src_ref, dst_ref, 0, src_ref.shape[0])


def _ffn(x, small, W, l, final, tm):
    n = x.shape[0]
    assert n % tm == 0
    params = [W["norm_ffn"], W["w_gu"], W["w_down"]]
    fin = W["final_norm"]
    small_specs = [pl.BlockSpec(s.shape, lambda i: (0, 0)) for s in small]
    return pl.pallas_call(
        functools.partial(_ffn_kernel, final=final, rb=min(tm, FFN_ROW_BLOCK),
                          n_small=len(small)),
        grid=(n // tm,),
        in_specs=[pl.BlockSpec((tm, D_MODEL), lambda i: (i, 0))] + small_specs
        + [_layer_spec(p, l) for p in params] + [_const_spec(fin.shape)],
        out_specs=[pl.BlockSpec((tm, D_MODEL), lambda i: (i, 0))] + small_specs,
        out_shape=[jax.ShapeDtypeStruct((n, D_MODEL), F32)]
        + [jax.ShapeDtypeStruct(s.shape, F32) for s in small],
        compiler_params=pltpu.CompilerParams(
            dimension_semantics=("arbitrary",),
            vmem_limit_bytes=VMEM_LIMIT),
        name="ffn",
    )(x, *small, *params, fin)


def _transpose_cast_kernel(wt_ref, o_ref):
    o_ref[...] = wt_ref[0].T.astype(BF16)


def _transpose_cast(wt, row0, ncols, rb=512):
    depth, _, d = wt.shape
    return pl.pallas_call(
        _transpose_cast_kernel,
        grid=(depth, ncols // rb),
        in_specs=[pl.BlockSpec((pl.Element(1), pl.Element(rb), pl.Element(d)),
                               lambda l, i: (l, pl.multiple_of(row0 + i * rb, SUB), 0))],
        out_specs=pl.BlockSpec((None, d, rb), lambda l, i: (l, 0, i)),
        out_shape=jax.ShapeDtypeStruct((depth, d, ncols), BF16),
        compiler_params=pltpu.CompilerParams(
            dimension_semantics=("arbitrary", "arbitrary"),
            vmem_limit_bytes=VMEM_LIMIT),
        name="transpose_cast",
    )(wt)


def _prep_weights(w_in, conv_w, w_gk2, b_gk2, gla_gain, w_a_out, w_b_out, w_o,
                  norm_mix, norm_ffn, w_gu, w_down, final_norm):
    depth = w_in.shape[0]
    w_in_t = jnp.swapaxes(w_in, 1, 2)
    w_main = _transpose_cast(w_in_t, 0, D_MAIN)
    w_gate = _transpose_cast(w_in_t, Z_HI, 2 * D_MODEL)
    w_z = jnp.pad(w_in[:, :, Z_LO:Z_HI], ((0, 0), (0, 0), (0, LANES - GATE_RANK))).astype(BF16)
    w_gk = jnp.pad(w_gk2, ((0, 0), (0, LANES - GATE_RANK), (0, 0))).astype(BF16)
    return {
        "w_main": w_main,
        "w_gate": w_gate,
        "w_z": w_z,
        "w_gk": w_gk,
        "b_gk": b_gk2.reshape(depth, 1, KDIM),
        "gain": gla_gain.reshape(depth, 1, DV),
        "conv_w": conv_w,
        "w_a": w_a_out.astype(BF16),
        "w_b": w_b_out.astype(BF16),
        "w_o": w_o.astype(BF16),
        "norm_mix": norm_mix.reshape(depth, 1, D_MODEL),
        "norm_ffn": norm_ffn.reshape(depth, 1, D_MODEL),
        "w_gu": w_gu.astype(BF16),
        "w_down": w_down.astype(BF16),
        "final_norm": final_norm.reshape(1, D_MODEL),
    }


def kernel(x_prompt, x_sample, state_conv, state_gla, meta_tokens, w_in, conv_w, w_gk2, b_gk2,
           gla_gain, w_a_out, w_b_out, w_o, norm_mix, norm_ffn, w_gu, w_down, final_norm):
    depth = w_in.shape[0]
    bsz, seq, _ = x_prompt.shape
    nsmp = x_sample.shape[0]
    W = _prep_weights(w_in, conv_w, w_gk2, b_gk2, gla_gain, w_a_out, w_b_out, w_o,
                      norm_mix, norm_ffn, w_gu, w_down, final_norm)

    xm = jnp.pad(meta_tokens.astype(F32), ((CHUNK - N_META, 0), (0, 0)))[None]
    xp = x_prompt
    xs = x_sample.reshape(nsmp, D_MODEL)
    conv_all = state_conv.reshape(depth, nsmp, (CONV_W - 1) * D_CONV)
    zero_conv = jnp.zeros((1, SUB, D_CONV), F32)
    zero_state = jnp.zeros((1, HEADS, DK, DV), F32)

    p_conv, p_gla, s_conv = [], [], []
    s_gla = None
    for l in range(depth):
        last = l == depth - 1
        xm, m_conv, m_state = _mixer_seq(xm, zero_conv, zero_state, W, l, CHUNK)
        xp, pc, ps = _mixer_seq(xp, m_conv, m_state, W, l, PROMPT_TILE)
        xs, sc, s_gla = _mixer_step(xs, conv_all, state_gla, s_gla, W, l)
        small = [xs] if last else [xs, xm[0]]
        outs = _ffn(xp.reshape(bsz * seq, D_MODEL), small, W, l, last, 1024)
        xp, xs = outs[0].reshape(bsz, seq, D_MODEL), outs[1]
        if not last:
            xm = outs[2][None]
        p_conv.append(pc[:, SUB - (CONV_W - 1):, :])
        p_gla.append(ps)
        s_conv.append(sc.reshape(nsmp, CONV_W - 1, D_CONV))

    return (xp, xs.reshape(nsmp, 1, D_MODEL), jnp.stack(p_conv), jnp.stack(p_gla),
            jnp.stack(s_conv), s_gla)
```

```python
import functools

import jax
import jax.numpy as jnp
from jax import lax
from jax.experimental import pallas as pl
from jax.experimental.pallas import tpu as pltpu

F32 = jnp.float32
BF16 = jnp.bfloat16

D_MODEL = 1024
N_META = 16
D_CONV = D_MODEL
CONV_W = 3
HEADS = 4
DK = 128
DV = 256
KDIM = HEADS * DK
VDIM = HEADS * DV
GATE_RANK = 16
GATE_NORMALIZER = 16.0
D_FF = 2816
EPS = 1e-6
LOG2_E = 1.4426950408889634

CHUNK = 64
SUB = 8
NB = CHUNK // SUB
LANES = 128
COLB = 512
CUMSUM_ROWS = 256
FFN_ROW_BLOCK = 256
PROMPT_TILE = 256
SEQS_PER_STEP = 1
VMEM_LIMIT = 60 * 1024 * 1024

C_GB, C_GC, C_H = 0, 1024, 2048
C_Q, C_K, C_V, C_OG = 3072, 3584, 4096, 5120
D_MAIN = 6144
Z_LO = D_MAIN
Z_HI = Z_LO + GATE_RANK


def _rms(x, g):
    ms = jnp.mean(x * x, axis=-1, keepdims=True)
    return x * lax.rsqrt(ms + EPS) * g


def _log_sigmoid(x):
    return jnp.minimum(x, 0.0) - jnp.log(1.0 + jnp.exp(-jnp.abs(x)))


def _sigmoid(x):
    return 1.0 / (1.0 + jnp.exp(-x))


def _dot(a, b):
    return jnp.dot(a, b, preferred_element_type=F32)


def _dot_nt(a, b):
    return lax.dot_general(a, b, (((1,), (1,)), ((), ())), preferred_element_type=F32)


def _dot_tn(a, b):
    return lax.dot_general(a, b, (((0,), (0,)), ((), ())), preferred_element_type=F32)


def _split3(x):
    x1 = x.astype(BF16)
    r1 = x - x1.astype(F32)
    x2 = r1.astype(BF16)
    r2 = r1 - x2.astype(F32)
    return x1, x2, r2.astype(BF16)


def _head_norm(o, gain):
    return jnp.concatenate(
        [_rms(o[:, h * DV:(h + 1) * DV], gain) for h in range(HEADS)], axis=1)


def _interleave(slots, pieces):
    done = 0
    for i, slot_fn in enumerate(slots):
        slot_fn()
        upto = ((i + 1) * len(pieces) + len(slots) - 1) // len(slots)
        while done < upto:
            pieces[done]()
            done += 1


def _mixer_seq_kernel(x_ref, xnext_ref, cprev_ref, s0_ref, wm_ref, wg_ref, wz_ref, wa_ref, wb_ref,
                      wo_ref, wgk_ref, nmix_ref, convw_ref, bgk_ref, gain_ref, lmat_ref,
                      xo_ref, cnew_ref, sout_ref,
                      xn_s, ubuf, st_s, q_s, k_s, b_s, v_s, o_s, ca_s, ya_s, og_s, ga_s, gb_s):
    t = pl.program_id(1)
    nt = pl.num_programs(1)
    nseq, tm = x_ref.shape[0], x_ref.shape[1]
    xn_buf = lax.bitwise_and(t, 1)
    xn_next_buf = lax.bitwise_and(t + 1, 1)

    @pl.when(t == 0)
    def _init():
        for p in range(nseq):
            ubuf[p, 0:SUB, :] = cprev_ref[0]
            xn_s[p, 0] = _rms(x_ref[p], nmix_ref[...]).astype(BF16)
            for h in range(HEADS):
                st_s[p, h] = s0_ref[0, h].T

    ti = lax.broadcasted_iota(jnp.int32, (CHUNK, CHUNK), 0)
    si = lax.broadcasted_iota(jnp.int32, (CHUNK, CHUNK), 1)
    diag_sel = [(si == ((ti >> 3) << 3) + j) & ((ti & 7) >= j) for j in range(SUB)]
    lane_blk = lax.broadcasted_iota(jnp.int32, (SUB, CHUNK), 1) >> 3
    stack_off = [sum(CHUNK - SUB * (i + 1) for i in range(j)) for j in range(NB)]
    ncb = D_MODEL // COLB

    def cols_of(cb):
        return slice(cb * COLB, (cb + 1) * COLB)

    def make_stream(p):
        xn = xn_s[p, xn_buf]
        vals = {}

        def proj(w_ref, c0, n):
            return _dot(xn, w_ref[:, c0:c0 + n])

        def head_gate():
            z = proj(wz_ref, 0, LANES).astype(BF16)
            g = (_log_sigmoid(_dot(z, wgk_ref[...]) + bgk_ref[...])
                 * (LOG2_E / GATE_NORMALIZER))
            vals["g"] = _split3(g)

        def head_q():
            q_s[p] = proj(wm_ref, C_Q, KDIM) * (DK ** -0.5)

        def head_k():
            k_s[p] = proj(wm_ref, C_K, KDIM)

        def head_v():
            v_s[p] = proj(wm_ref, C_V, VDIM).astype(BF16)

        def head_cumsum():
            g1, g2, g3 = vals.pop("g")
            cr = lmat_ref.shape[0]
            for r0 in range(0, tm, cr):
                b_s[p, r0:r0 + cr, :] = _dot(lmat_ref[...], jnp.concatenate(
                    [g1[r0:r0 + cr], g2[r0:r0 + cr], g3[r0:r0 + cr]], axis=0))

        head = [head_gate, head_q, head_k, head_v, head_cumsum]

        def mid_u(cb):
            u = proj(wm_ref, C_GC + cb * COLB, COLB) * proj(wm_ref, C_H + cb * COLB, COLB)
            ubuf[p, SUB:SUB + tm, cols_of(cb)] = u

        def mid_conv(cb):
            cols = cols_of(cb)
            cw = convw_ref[:, cols]
            yconv = (cw[0:1] * ubuf[p, SUB - 2:SUB - 2 + tm, cols]
                     + cw[1:2] * ubuf[p, SUB - 1:SUB - 1 + tm, cols]
                     + cw[2:3] * ubuf[p, SUB:SUB + tm, cols])
            ca_s[p, :, cols] = (proj(wm_ref, C_GB + cb * COLB, COLB) * yconv).astype(BF16)
            tail = ubuf[p, tm:tm + SUB, cols]
            ubuf[p, 0:SUB, cols] = tail
            cnew_ref[p, :, cols] = tail

        def mid_ya(cb):
            ya_s[p, :, cols_of(cb)] = _dot(ca_s[p], wa_ref[:, cols_of(cb)])

        def mid_norm_next():
            xn_s[p, xn_next_buf] = _rms(xnext_ref[p], nmix_ref[...]).astype(BF16)

        def mid_og(cb):
            og = proj(wm_ref, C_OG + cb * COLB, COLB)
            og_s[p, :, cols_of(cb)] = og * _sigmoid(og)

        def mid_ga(cb):
            ga_s[p, :, cols_of(cb)] = _sigmoid(proj(wg_ref, cb * COLB, COLB))

        def mid_gb(cb):
            gb_s[p, :, cols_of(cb)] = _sigmoid(proj(wg_ref, D_MODEL + cb * COLB, COLB))

        mid = []
        for cb in range(ncb):
            mid += [functools.partial(mid_u, cb), functools.partial(mid_conv, cb)]
        mid.insert(2, mid_norm_next)
        for fn in (mid_ya, mid_og, mid_ga, mid_gb):
            mid += [functools.partial(fn, cb) for cb in range(ncb)]

        def gla_diag(c, h):
            r0 = c * CHUNK
            rows = slice(r0, r0 + CHUNK)
            kcols = slice(h * DK, (h + 1) * DK)
            qc = q_s[p, rows, kcols]
            bc = b_s[p, rows, kcols]

            def row_of_each_block(ref, j):
                return jnp.concatenate(
                    [jnp.broadcast_to(ref[p, r0 + SUB * i + j:r0 + SUB * i + j + 1, kcols],
                                      (SUB, DK)) for i in range(NB)], axis=0)

            acc = jnp.zeros((CHUNK, CHUNK), F32)
            for j in range(SUB):
                kj = row_of_each_block(k_s, j)
                bj = row_of_each_block(b_s, j)
                rj = jnp.sum(qc * kj * jnp.exp2(bc - bj), axis=-1, keepdims=True)
                acc = jnp.where(diag_sel[j], rj, acc)
            vals[("diag", c, h)] = acc

        def gla_rest(c, h):
            r0 = c * CHUNK
            rows = slice(r0, r0 + CHUNK)
            kcols = slice(h * DK, (h + 1) * DK)
            vcols = slice(h * DV, (h + 1) * DV)
            qc = q_s[p, rows, kcols]
            kc = k_s[p, rows, kcols]
            bc = b_s[p, rows, kcols]
            vc = v_s[p, rows, vcols]
            st = st_s[p, h]
            bend = [b_s[p, r0 + SUB * j + SUB - 1:r0 + SUB * (j + 1), kcols] for j in range(NB)]
            bend_rows = jnp.concatenate([jnp.broadcast_to(e, (SUB, DK)) for e in bend], axis=0)
            blast = bend[NB - 1]
            qhat = (qc * jnp.exp2(bc)).astype(BF16)
            o = _dot_nt(qhat, st.astype(BF16))
            kt = (kc * jnp.exp2(bend_rows - bc)).astype(BF16)
            qs = [qc[SUB * (j + 1):] * jnp.exp2(bc[SUB * (j + 1):] - bend[j])
                  for j in range(NB - 1)]
            rm = _dot_nt(jnp.concatenate(qs, axis=0).astype(BF16), kt)
            arows = [jnp.zeros((SUB, CHUNK), F32)]
            for i in range(1, NB):
                acc = jnp.zeros((SUB, CHUNK), F32)
                for j in range(i):
                    s0 = stack_off[j] + SUB * (i - j - 1)
                    acc = jnp.where(lane_blk == j, rm[s0:s0 + SUB], acc)
                arows.append(acc)
            a = jnp.concatenate(arows, axis=0) + vals.pop(("diag", c, h))
            o_s[p, rows, vcols] = o + _dot(a.astype(BF16), vc)
            khat = (kc * jnp.exp2(blast - bc)).astype(BF16)
            st_s[p, h] = st * jnp.exp2(blast) + _dot_tn(vc, khat)

        gla = []
        for c in range(tm // CHUNK):
            for h in range(HEADS):
                gla += [functools.partial(gla_diag, c, h), functools.partial(gla_rest, c, h)]

        def tail_yb(rs):
            yb_in = (_head_norm(o_s[p, rs, :], gain_ref[...]) * og_s[p, rs, :]).astype(BF16)
            vals[("yb", rs.start)] = _dot(yb_in, wb_ref[...])

        def tail_out(rs):
            m = ga_s[p, rs, :] * ya_s[p, rs, :] + gb_s[p, rs, :] * vals.pop(("yb", rs.start))
            xo_ref[p, rs, :] = x_ref[p, rs, :] + _dot(m.astype(BF16), wo_ref[...])

        def tail(rs):
            return [functools.partial(tail_yb, rs), functools.partial(tail_out, rs)]

        return head, mid, gla, tail

    for p in range(nseq):
        head, mid, gla, tail = make_stream(p)
        for fn in head:
            fn()
        _interleave(gla, mid)
        for fn in tail(slice(0, tm)):
            fn()

    @pl.when(t == nt - 1)
    def _fin():
        for p in range(nseq):
            for h in range(HEADS):
                sout_ref[p, h] = st_s[p, h].T


def _const_spec(shape):
    nd = len(shape)
    return pl.BlockSpec(shape, lambda *_: (0,) * nd, pipeline_mode=pl.Buffered(1))


def _layer_spec(arr, l):
    nd = arr.ndim
    return pl.BlockSpec((None,) + arr.shape[1:], lambda *_: (l,) + (0,) * (nd - 1),
                        pipeline_mode=pl.Buffered(1))


def _mixer_seq(x, cprev8, s0, W, l, tm):
    bsz, T, _ = x.shape
    assert T % tm == 0 and tm % CHUNK == 0
    ns = SEQS_PER_STEP if bsz % SEQS_PER_STEP == 0 else 1
    ridx = jnp.arange(min(tm, CUMSUM_ROWS))
    lmat = ((ridx[:, None] // CHUNK == ridx[None, :] // CHUNK)
            & (ridx[None, :] <= ridx[:, None])).astype(BF16)
    lmat = jnp.concatenate([lmat, lmat, lmat], axis=1)
    params = [W["w_main"], W["w_gate"], W["w_z"], W["w_a"], W["w_b"], W["w_o"], W["w_gk"],
              W["norm_mix"], W["conv_w"], W["b_gk"], W["gain"]]
    nt = T // tm
    in_specs = [
        pl.BlockSpec((ns, tm, D_MODEL), lambda b, t: (b, t, 0)),
        pl.BlockSpec((ns, tm, D_MODEL), lambda b, t: (b, jnp.minimum(t + 1, nt - 1), 0)),
        pl.BlockSpec((1, SUB, D_CONV), lambda b, t: (0, 0, 0)),
        pl.BlockSpec((1, HEADS, DK, DV), lambda b, t: (0, 0, 0, 0)),
    ] + [_layer_spec(p, l) for p in params] + [_const_spec(lmat.shape)]
    out_shape = (
        jax.ShapeDtypeStruct((bsz, T, D_MODEL), F32),
        jax.ShapeDtypeStruct((bsz, SUB, D_CONV), F32),
        jax.ShapeDtypeStruct((bsz, HEADS, DK, DV), F32),
    )
    out_specs = (
        pl.BlockSpec((ns, tm, D_MODEL), lambda b, t: (b, t, 0)),
        pl.BlockSpec((ns, SUB, D_CONV), lambda b, t: (b, 0, 0)),
        pl.BlockSpec((ns, HEADS, DK, DV), lambda b, t: (b, 0, 0, 0)),
    )
    scratch = [
        pltpu.VMEM((ns, 2, tm, D_MODEL), BF16),
        pltpu.VMEM((ns, tm + SUB, D_CONV), F32),
        pltpu.VMEM((ns, HEADS, DV, DK), F32),
        pltpu.VMEM((ns, tm, KDIM), F32),
        pltpu.VMEM((ns, tm, KDIM), F32),
        pltpu.VMEM((ns, tm, KDIM), F32),
        pltpu.VMEM((ns, tm, VDIM), BF16),
        pltpu.VMEM((ns, tm, VDIM), F32),
        pltpu.VMEM((ns, tm, D_CONV), BF16),
        pltpu.VMEM((ns, tm, D_MODEL), F32),
        pltpu.VMEM((ns, tm, VDIM), F32),
        pltpu.VMEM((ns, tm, D_MODEL), F32),
        pltpu.VMEM((ns, tm, D_MODEL), F32),
    ]
    return pl.pallas_call(
        _mixer_seq_kernel,
        grid=(bsz // ns, nt),
        in_specs=in_specs,
        out_specs=out_specs,
        out_shape=out_shape,
        scratch_shapes=scratch,
        compiler_params=pltpu.CompilerParams(
            dimension_semantics=("arbitrary", "arbitrary"),
            vmem_limit_bytes=VMEM_LIMIT),
        name="mixer_seq",
    )(x, x, cprev8, s0, *params, lmat)


def _mixer_step_kernel(x_ref, cp_ref, s_ref, wm_ref, wg_ref, wz_ref,
                       wa_ref, wb_ref, wo_ref, wgk_ref,
                       nmix_ref, convw_ref, bgk_ref, gain_ref, *rest):
    xo_ref, cnew_ref, sout_ref, qt_s, kt_s, at_s, v_s, o_s, ya_s = rest[-9:]
    i = pl.program_id(0)
    n = pl.num_programs(0)
    nseq = x_ref.shape[0]

    def xn_bf16():
        return _rms(x_ref[...], nmix_ref[...]).astype(BF16)

    @pl.when(i == 0)
    def _dense_in():
        xn = xn_bf16()
        u = _dot(xn, wm_ref[:, C_GC:C_GC + D_CONV]) * _dot(xn, wm_ref[:, C_H:C_H + D_CONV])
        cw = convw_ref[...]
        p1 = cp_ref[:, D_CONV:2 * D_CONV]
        yconv = cw[0:1] * cp_ref[:, 0:D_CONV] + cw[1:2] * p1 + cw[2:3] * u
        cnew_ref[:, 0:D_CONV] = p1
        cnew_ref[:, D_CONV:2 * D_CONV] = u
        ca = (_dot(xn, wm_ref[:, C_GB:C_GB + D_CONV]) * yconv).astype(BF16)
        ya_s[...] = _dot(ca, wa_ref[...])
        v_s[...] = _dot(xn, wm_ref[:, C_V:C_V + VDIM])
        qt_s[...] = (_dot(xn, wm_ref[:, C_Q:C_Q + KDIM]) * (DK ** -0.5)).T
        kt_s[...] = _dot(xn, wm_ref[:, C_K:C_K + KDIM]).T
        z = _dot(xn, wz_ref[...]).astype(BF16)
        g = _log_sigmoid(_dot(z, wgk_ref[...]) + bgk_ref[...]) * (1.0 / GATE_NORMALIZER)
        at_s[...] = jnp.exp(g).T

    shift = (nseq - i * SUB) % nseq
    lead = slice(0, 2 * SUB)
    qt = pltpu.roll(qt_s[...], shift, 1)[:, lead].astype(BF16)
    kt = pltpu.roll(kt_s[...], shift, 1)[:, lead].astype(BF16)
    at = jnp.concatenate(_split3(pltpu.roll(at_s[...], shift, 1)[:, lead]), axis=1)
    rid = lax.broadcasted_iota(jnp.int32, (2 * SUB, SUB * LANES), 0)
    cid = lax.broadcasted_iota(jnp.int32, (2 * SUB, SUB * LANES), 1)
    sel = (rid == (cid >> 7)).astype(BF16)
    sel3 = jnp.concatenate([sel, sel, sel], axis=0)
    for h in range(HEADS):
        rk = slice(h * DK, (h + 1) * DK)
        cv = slice(h * DV, (h + 1) * DV)
        qb = _dot(qt[rk], sel)
        kb = _dot(kt[rk], sel)
        ab = _dot(at[rk], sel3)
        for j in range(SUB):
            row = pl.ds(i * SUB + j, 1)
            grp = slice(j * LANES, (j + 1) * LANES)
            both = lambda c: jnp.concatenate([c[:, grp], c[:, grp]], axis=1)
            sn = s_ref[j, h] * both(ab) + both(kb) * v_s[row, cv]
            sout_ref[j, h] = sn
            o_s[row, cv] = jnp.sum(sn * both(qb), axis=0, keepdims=True)

    @pl.when(i == n - 1)
    def _dense_out():
        xn = xn_bf16()
        og = _dot(xn, wm_ref[:, C_OG:C_OG + VDIM])
        yb_in = (_head_norm(o_s[...], gain_ref[...]) * (og * _sigmoid(og))).astype(BF16)
        yb = _dot(yb_in, wb_ref[...])
        m = (_sigmoid(_dot(xn, wg_ref[:, 0:D_MODEL])) * ya_s[...]
             + _sigmoid(_dot(xn, wg_ref[:, D_MODEL:2 * D_MODEL])) * yb)
        xo_ref[...] = x_ref[...] + _dot(m.astype(BF16), wo_ref[...])


def _mixer_step(x, conv_all, s_all, s_out_prev, W, l):
    nseq = x.shape[0]
    assert nseq == LANES
    params = [W["w_main"], W["w_gate"], W["w_z"], W["w_a"], W["w_b"],
              W["w_o"], W["w_gk"], W["norm_mix"], W["conv_w"], W["b_gk"], W["gain"]]
    state_blk = (None, SUB, HEADS, DK, DV)
    in_specs = [
        _const_spec(x.shape),
        _layer_spec(conv_all, l),
        pl.BlockSpec(state_blk, lambda i: (l, i, 0, 0, 0)),
    ] + [_layer_spec(p, l) for p in params]
    args = [x, conv_all, s_all, *params]
    aliases = {}
    if s_out_prev is not None:
        in_specs.append(pl.BlockSpec(memory_space=pl.ANY))
        aliases = {len(args): 2}
        args.append(s_out_prev)
    out_shape = (
        jax.ShapeDtypeStruct(x.shape, F32),
        jax.ShapeDtypeStruct((nseq, 2 * D_CONV), F32),
        jax.ShapeDtypeStruct(s_all.shape, F32),
    )
    out_specs = (
        pl.BlockSpec(x.shape, lambda i: (0, 0)),
        pl.BlockSpec((nseq, 2 * D_CONV), lambda i: (0, 0)),
        pl.BlockSpec(state_blk, lambda i: (l, i, 0, 0, 0)),
    )
    scratch = [
        pltpu.VMEM((KDIM, nseq), F32),
        pltpu.VMEM((KDIM, nseq), F32),
        pltpu.VMEM((KDIM, nseq), F32),
        pltpu.VMEM((nseq, VDIM), F32),
        pltpu.VMEM((nseq, VDIM), F32),
        pltpu.VMEM((nseq, D_MODEL), F32),
    ]
    return pl.pallas_call(
        _mixer_step_kernel,
        grid=(nseq // SUB,),
        in_specs=in_specs,
        out_specs=out_specs,
        out_shape=out_shape,
        scratch_shapes=scratch,
        input_output_aliases=aliases,
        compiler_params=pltpu.CompilerParams(
            dimension_semantics=("arbitrary",),
            vmem_limit_bytes=VMEM_LIMIT),
        name="mixer_step",
    )(*args)


def _ffn_kernel(*refs, final, rb, n_small):
    x_ref, small_refs = refs[0], refs[1:1 + n_small]
    nffn_ref, wgu_ref, wdn_ref, fin_ref = refs[1 + n_small:5 + n_small]
    o_ref, small_out_refs = refs[5 + n_small], refs[6 + n_small:]

    def block(src_ref, dst_ref, r0, rows):
        x = src_ref[r0:r0 + rows, :]
        xn = _rms(x, nffn_ref[...]).astype(BF16)
        g = _dot(xn, wgu_ref[:, 0:D_FF])
        u = _dot(xn, wgu_ref[:, D_FF:2 * D_FF])
        hid = (g * _sigmoid(g) * u).astype(BF16)
        y = x + _dot(hid, wdn_ref[...])
        if final:
            y = _rms(y, fin_ref[...])
        dst_ref[r0:r0 + rows, :] = y

    for r0 in range(0, x_ref.shape[0], rb):
        block(x_ref, o_ref, r0, rb)

    @pl.when(pl.program_id(0) == pl.num_programs(0) - 1)
    def _small_groups():
        for src_ref, dst_ref in zip(small_refs, small_out_refs):
            block(src_ref, dst_ref, 0, src_ref.shape[0])


def _ffn(x, small, W, l, final, tm):
    n = x.shape[0]
    assert n % tm == 0
    params = [W["norm_ffn"], W["w_gu"], W["w_down"]]
    fin = W["final_norm"]
    small_specs = [pl.BlockSpec(s.shape, lambda i: (0, 0)) for s in small]
    return pl.pallas_call(
        functools.partial(_ffn_kernel, final=final, rb=min(tm, FFN_ROW_BLOCK),
                          n_small=len(small)),
        grid=(n // tm,),
        in_specs=[pl.BlockSpec((tm, D_MODEL), lambda i: (i, 0))] + small_specs
        + [_layer_spec(p, l) for p in params] + [_const_spec(fin.shape)],
        out_specs=[pl.BlockSpec((tm, D_MODEL), lambda i: (i, 0))] + small_specs,
        out_shape=[jax.ShapeDtypeStruct((n, D_MODEL), F32)]
        + [jax.ShapeDtypeStruct(s.shape, F32) for s in small],
        compiler_params=pltpu.CompilerParams(
            dimension_semantics=("arbitrary",),
            vmem_limit_bytes=VMEM_LIMIT),
        name="ffn",
    )(x, *small, *params, fin)


def _transpose_cast_kernel(wt_ref, o_ref):
    o_ref[...] = wt_ref[0].T.astype(BF16)


def _transpose_cast(wt, row0, ncols, rb=512):
    depth, _, d = wt.shape
    return pl.pallas_call(
        _transpose_cast_kernel,
        grid=(depth, ncols // rb),
        in_specs=[pl.BlockSpec((pl.Element(1), pl.Element(rb), pl.Element(d)),
                               lambda l, i: (l, pl.multiple_of(row0 + i * rb, SUB), 0))],
        out_specs=pl.BlockSpec((None, d, rb), lambda l, i: (l, 0, i)),
        out_shape=jax.ShapeDtypeStruct((depth, d, ncols), BF16),
        compiler_params=pltpu.CompilerParams(
            dimension_semantics=("arbitrary", "arbitrary"),
            vmem_limit_bytes=VMEM_LIMIT),
        name="transpose_cast",
    )(wt)


def _prep_weights(w_in, conv_w, w_gk2, b_gk2, gla_gain, w_a_out, w_b_out, w_o,
                  norm_mix, norm_ffn, w_gu, w_down, final_norm):
    depth = w_in.shape[0]
    w_in_t = jnp.swapaxes(w_in, 1, 2)
    w_main = _transpose_cast(w_in_t, 0, D_MAIN)
    w_gate = _transpose_cast(w_in_t, Z_HI, 2 * D_MODEL)
    w_z = jnp.pad(w_in[:, :, Z_LO:Z_HI], ((0, 0), (0, 0), (0, LANES - GATE_RANK))).astype(BF16)
    w_gk = jnp.pad(w_gk2, ((0, 0), (0, LANES - GATE_RANK), (0, 0))).astype(BF16)
    return {
        "w_main": w_main,
        "w_gate": w_gate,
        "w_z": w_z,
        "w_gk": w_gk,
        "b_gk": b_gk2.reshape(depth, 1, KDIM),
        "gain": gla_gain.reshape(depth, 1, DV),
        "conv_w": conv_w,
        "w_a": w_a_out.astype(BF16),
        "w_b": w_b_out.astype(BF16),
        "w_o": w_o.astype(BF16),
        "norm_mix": norm_mix.reshape(depth, 1, D_MODEL),
        "norm_ffn": norm_ffn.reshape(depth, 1, D_MODEL),
        "w_gu": w_gu.astype(BF16),
        "w_down": w_down.astype(BF16),
        "final_norm": final_norm.reshape(1, D_MODEL),
    }


def kernel(x_prompt, x_sample, state_conv, state_gla, meta_tokens, w_in, conv_w, w_gk2, b_gk2,
           gla_gain, w_a_out, w_b_out, w_o, norm_mix, norm_ffn, w_gu, w_down, final_norm):
    depth = w_in.shape[0]
    bsz, seq, _ = x_prompt.shape
    nsmp = x_sample.shape[0]
    W = _prep_weights(w_in, conv_w, w_gk2, b_gk2, gla_gain, w_a_out, w_b_out, w_o,
                      norm_mix, norm_ffn, w_gu, w_down, final_norm)

    xm = jnp.pad(meta_tokens.astype(F32), ((CHUNK - N_META, 0), (0, 0)))[None]
    xp = x_prompt
    xs = x_sample.reshape(nsmp, D_MODEL)
    conv_all = state_conv.reshape(depth, nsmp, (CONV_W - 1) * D_CONV)
    zero_conv = jnp.zeros((1, SUB, D_CONV), F32)
    zero_state = jnp.zeros((1, HEADS, DK, DV), F32)

    p_conv, p_gla, s_conv = [], [], []
    s_gla = None
    for l in range(depth):
        last = l == depth - 1
        xm, m_conv, m_state = _mixer_seq(xm, zero_conv, zero_state, W, l, CHUNK)
        xp, pc, ps = _mixer_seq(xp, m_conv, m_state, W, l, PROMPT_TILE)
        xs, sc, s_gla = _mixer_step(xs, conv_all, state_gla, s_gla, W, l)
        small = xs if last else jnp.concatenate([xs, xm[0]], axis=0)
        outs = _ffn(xp.reshape(bsz * seq, D_MODEL), [small], W, l, last, 1024)
        xp, xs = outs[0].reshape(bsz, seq, D_MODEL), outs[1][:nsmp]
        if not last:
            xm = outs[1][nsmp:][None]
        p_conv.append(pc[:, SUB - (CONV_W - 1):, :])
        p_gla.append(ps)
        s_conv.append(sc.reshape(nsmp, CONV_W - 1, D_CONV))

    return (xp, xs.reshape(nsmp, 1, D_MODEL), jnp.stack(p_conv), jnp.stack(p_gla),
            jnp.stack(s_conv), s_gla)
```

```python
import functools

import jax
import jax.numpy as jnp
from jax import lax
from jax.experimental import pallas as pl
from jax.experimental.pallas import tpu as pltpu

F32 = jnp.float32
BF16 = jnp.bfloat16

D_MODEL = 1024
N_META = 16
D_CONV = D_MODEL
CONV_W = 3
HEADS = 4
DK = 128
DV = 256
KDIM = HEADS * DK
VDIM = HEADS * DV
GATE_RANK = 16
GATE_NORMALIZER = 16.0
D_FF = 2816
EPS = 1e-6
LOG2_E = 1.4426950408889634

CHUNK = 64
SUB = 8
NB = CHUNK // SUB
LANES = 128
COLB = 512
CUMSUM_ROWS = 256
FFN_ROW_BLOCK = 256
FFN_SMALL_ROWS = 16
PROMPT_TILE = 256
SEQS_PER_STEP = 1
VMEM_LIMIT = 60 * 1024 * 1024

C_GB, C_GC, C_H = 0, 1024, 2048
C_Q, C_K, C_V, C_OG = 3072, 3584, 4096, 5120
D_MAIN = 6144
Z_LO = D_MAIN
Z_HI = Z_LO + GATE_RANK


def _rms(x, g):
    ms = jnp.mean(x * x, axis=-1, keepdims=True)
    return x * lax.rsqrt(ms + EPS) * g


def _log_sigmoid(x):
    return jnp.minimum(x, 0.0) - jnp.log(1.0 + jnp.exp(-jnp.abs(x)))


def _sigmoid(x):
    return 1.0 / (1.0 + jnp.exp(-x))


def _dot(a, b):
    return jnp.dot(a, b, preferred_element_type=F32)


def _dot_nt(a, b):
    return lax.dot_general(a, b, (((1,), (1,)), ((), ())), preferred_element_type=F32)


def _dot_tn(a, b):
    return lax.dot_general(a, b, (((0,), (0,)), ((), ())), preferred_element_type=F32)


def _split3(x):
    x1 = x.astype(BF16)
    r1 = x - x1.astype(F32)
    x2 = r1.astype(BF16)
    r2 = r1 - x2.astype(F32)
    return x1, x2, r2.astype(BF16)


def _head_norm(o, gain):
    return jnp.concatenate(
        [_rms(o[:, h * DV:(h + 1) * DV], gain) for h in range(HEADS)], axis=1)


def _interleave(slots, pieces):
    done = 0
    for i, slot_fn in enumerate(slots):
        slot_fn()
        upto = ((i + 1) * len(pieces) + len(slots) - 1) // len(slots)
        while done < upto:
            pieces[done]()
            done += 1


def _mixer_seq_kernel(x_ref, xnext_ref, cprev_ref, s0_ref, wm_ref, wg_ref, wz_ref, wa_ref, wb_ref,
                      wo_ref, wgk_ref, nmix_ref, convw_ref, bgk_ref, gain_ref, lmat_ref,
                      xo_ref, cnew_ref, sout_ref,
                      xn_s, ubuf, st_s, q_s, k_s, b_s, v_s, o_s, ca_s, ya_s, og_s, ga_s, gb_s):
    t = pl.program_id(1)
    nt = pl.num_programs(1)
    nseq, tm = x_ref.shape[0], x_ref.shape[1]
    xn_buf = lax.bitwise_and(t, 1)
    xn_next_buf = lax.bitwise_and(t + 1, 1)

    @pl.when(t == 0)
    def _init():
        for p in range(nseq):
            ubuf[p, 0:SUB, :] = cprev_ref[0]
            xn_s[p, 0] = _rms(x_ref[p], nmix_ref[...]).astype(BF16)
            for h in range(HEADS):
                st_s[p, h] = s0_ref[0, h].T

    ti = lax.broadcasted_iota(jnp.int32, (CHUNK, CHUNK), 0)
    si = lax.broadcasted_iota(jnp.int32, (CHUNK, CHUNK), 1)
    diag_sel = [(si == ((ti >> 3) << 3) + j) & ((ti & 7) >= j) for j in range(SUB)]
    lane_blk = lax.broadcasted_iota(jnp.int32, (SUB, CHUNK), 1) >> 3
    stack_off = [sum(CHUNK - SUB * (i + 1) for i in range(j)) for j in range(NB)]
    ncb = D_MODEL // COLB

    def cols_of(cb):
        return slice(cb * COLB, (cb + 1) * COLB)

    def make_stream(p):
        xn = xn_s[p, xn_buf]
        vals = {}

        def proj(w_ref, c0, n):
            return _dot(xn, w_ref[:, c0:c0 + n])

        def head_gate():
            z = proj(wz_ref, 0, LANES).astype(BF16)
            g = (_log_sigmoid(_dot(z, wgk_ref[...]) + bgk_ref[...])
                 * (LOG2_E / GATE_NORMALIZER))
            vals["g"] = _split3(g)

        def head_q():
            q_s[p] = proj(wm_ref, C_Q, KDIM) * (DK ** -0.5)

        def head_k():
            k_s[p] = proj(wm_ref, C_K, KDIM)

        def head_v():
            v_s[p] = proj(wm_ref, C_V, VDIM).astype(BF16)

        def head_cumsum():
            g1, g2, g3 = vals.pop("g")
            cr = lmat_ref.shape[0]
            for r0 in range(0, tm, cr):
                b_s[p, r0:r0 + cr, :] = _dot(lmat_ref[...], jnp.concatenate(
                    [g1[r0:r0 + cr], g2[r0:r0 + cr], g3[r0:r0 + cr]], axis=0))

        head = [head_gate, head_q, head_k, head_v, head_cumsum]

        def mid_u(cb):
            u = proj(wm_ref, C_GC + cb * COLB, COLB) * proj(wm_ref, C_H + cb * COLB, COLB)
            ubuf[p, SUB:SUB + tm, cols_of(cb)] = u

        def mid_conv(cb):
            cols = cols_of(cb)
            cw = convw_ref[:, cols]
            yconv = (cw[0:1] * ubuf[p, SUB - 2:SUB - 2 + tm, cols]
                     + cw[1:2] * ubuf[p, SUB - 1:SUB - 1 + tm, cols]
                     + cw[2:3] * ubuf[p, SUB:SUB + tm, cols])
            ca_s[p, :, cols] = (proj(wm_ref, C_GB + cb * COLB, COLB) * yconv).astype(BF16)
            tail = ubuf[p, tm:tm + SUB, cols]
            ubuf[p, 0:SUB, cols] = tail
            cnew_ref[p, :, cols] = tail

        def mid_ya(cb):
            ya_s[p, :, cols_of(cb)] = _dot(ca_s[p], wa_ref[:, cols_of(cb)])

        def mid_norm_next():
            xn_s[p, xn_next_buf] = _rms(xnext_ref[p], nmix_ref[...]).astype(BF16)

        def mid_og(cb):
            og = proj(wm_ref, C_OG + cb * COLB, COLB)
            og_s[p, :, cols_of(cb)] = og * _sigmoid(og)

        def mid_ga(cb):
            ga_s[p, :, cols_of(cb)] = _sigmoid(proj(wg_ref, cb * COLB, COLB))

        def mid_gb(cb):
            gb_s[p, :, cols_of(cb)] = _sigmoid(proj(wg_ref, D_MODEL + cb * COLB, COLB))

        mid = []
        for cb in range(ncb):
            mid += [functools.partial(mid_u, cb), functools.partial(mid_conv, cb)]
        mid.insert(2, mid_norm_next)
        for fn in (mid_ya, mid_og, mid_ga, mid_gb):
            mid += [functools.partial(fn, cb) for cb in range(ncb)]

        def gla_diag(c, h):
            r0 = c * CHUNK
            rows = slice(r0, r0 + CHUNK)
            kcols = slice(h * DK, (h + 1) * DK)
            qc = q_s[p, rows, kcols]
            bc = b_s[p, rows, kcols]

            def row_of_each_block(ref, j):
                return jnp.concatenate(
                    [jnp.broadcast_to(ref[p, r0 + SUB * i + j:r0 + SUB * i + j + 1, kcols],
                                      (SUB, DK)) for i in range(NB)], axis=0)

            acc = jnp.zeros((CHUNK, CHUNK), F32)
            for j in range(SUB):
                kj = row_of_each_block(k_s, j)
                bj = row_of_each_block(b_s, j)
                rj = jnp.sum(qc * kj * jnp.exp2(bc - bj), axis=-1, keepdims=True)
                acc = jnp.where(diag_sel[j], rj, acc)
            vals[("diag", c, h)] = acc

        def gla_rest(c, h):
            r0 = c * CHUNK
            rows = slice(r0, r0 + CHUNK)
            kcols = slice(h * DK, (h + 1) * DK)
            vcols = slice(h * DV, (h + 1) * DV)
            qc = q_s[p, rows, kcols]
            kc = k_s[p, rows, kcols]
            bc = b_s[p, rows, kcols]
            vc = v_s[p, rows, vcols]
            st = st_s[p, h]
            bend = [b_s[p, r0 + SUB * j + SUB - 1:r0 + SUB * (j + 1), kcols] for j in range(NB)]
            bend_rows = jnp.concatenate([jnp.broadcast_to(e, (SUB, DK)) for e in bend], axis=0)
            blast = bend[NB - 1]
            qhat = (qc * jnp.exp2(bc)).astype(BF16)
            o = _dot_nt(qhat, st.astype(BF16))
            kt = (kc * jnp.exp2(bend_rows - bc)).astype(BF16)
            qs = [qc[SUB * (j + 1):] * jnp.exp2(bc[SUB * (j + 1):] - bend[j])
                  for j in range(NB - 1)]
            rm = _dot_nt(jnp.concatenate(qs, axis=0).astype(BF16), kt)
            arows = [jnp.zeros((SUB, CHUNK), F32)]
            for i in range(1, NB):
                acc = jnp.zeros((SUB, CHUNK), F32)
                for j in range(i):
                    s0 = stack_off[j] + SUB * (i - j - 1)
                    acc = jnp.where(lane_blk == j, rm[s0:s0 + SUB], acc)
                arows.append(acc)
            a = jnp.concatenate(arows, axis=0) + vals.pop(("diag", c, h))
            o_s[p, rows, vcols] = o + _dot(a.astype(BF16), vc)
            khat = (kc * jnp.exp2(blast - bc)).astype(BF16)
            st_s[p, h] = st * jnp.exp2(blast) + _dot_tn(vc, khat)

        gla = []
        for c in range(tm // CHUNK):
            for h in range(HEADS):
                gla += [functools.partial(gla_diag, c, h), functools.partial(gla_rest, c, h)]

        def tail_yb(rs):
            yb_in = (_head_norm(o_s[p, rs, :], gain_ref[...]) * og_s[p, rs, :]).astype(BF16)
            vals[("yb", rs.start)] = _dot(yb_in, wb_ref[...])

        def tail_out(rs):
            m = ga_s[p, rs, :] * ya_s[p, rs, :] + gb_s[p, rs, :] * vals.pop(("yb", rs.start))
            xo_ref[p, rs, :] = x_ref[p, rs, :] + _dot(m.astype(BF16), wo_ref[...])

        def tail(rs):
            return [functools.partial(tail_yb, rs), functools.partial(tail_out, rs)]

        return head, mid, gla, tail

    for p in range(nseq):
        head, mid, gla, tail = make_stream(p)
        for fn in head:
            fn()
        _interleave(gla, mid)
        for fn in tail(slice(0, tm)):
            fn()

    @pl.when(t == nt - 1)
    def _fin():
        for p in range(nseq):
            for h in range(HEADS):
                sout_ref[p, h] = st_s[p, h].T


def _const_spec(shape):
    nd = len(shape)
    return pl.BlockSpec(shape, lambda *_: (0,) * nd, pipeline_mode=pl.Buffered(1))


def _layer_spec(arr, l):
    nd = arr.ndim
    return pl.BlockSpec((None,) + arr.shape[1:], lambda *_: (l,) + (0,) * (nd - 1),
                        pipeline_mode=pl.Buffered(1))


def _mixer_seq(x, cprev8, s0, W, l, tm):
    bsz, T, _ = x.shape
    assert T % tm == 0 and tm % CHUNK == 0
    ns = SEQS_PER_STEP if bsz % SEQS_PER_STEP == 0 else 1
    ridx = jnp.arange(min(tm, CUMSUM_ROWS))
    lmat = ((ridx[:, None] // CHUNK == ridx[None, :] // CHUNK)
            & (ridx[None, :] <= ridx[:, None])).astype(BF16)
    lmat = jnp.concatenate([lmat, lmat, lmat], axis=1)
    params = [W["w_main"], W["w_gate"], W["w_z"], W["w_a"], W["w_b"], W["w_o"], W["w_gk"],
              W["norm_mix"], W["conv_w"], W["b_gk"], W["gain"]]
    nt = T // tm
    in_specs = [
        pl.BlockSpec((ns, tm, D_MODEL), lambda b, t: (b, t, 0)),
        pl.BlockSpec((ns, tm, D_MODEL), lambda b, t: (b, jnp.minimum(t + 1, nt - 1), 0)),
        pl.BlockSpec((1, SUB, D_CONV), lambda b, t: (0, 0, 0)),
        pl.BlockSpec((1, HEADS, DK, DV), lambda b, t: (0, 0, 0, 0)),
    ] + [_layer_spec(p, l) for p in params] + [_const_spec(lmat.shape)]
    out_shape = (
        jax.ShapeDtypeStruct((bsz, T, D_MODEL), F32),
        jax.ShapeDtypeStruct((bsz, SUB, D_CONV), F32),
        jax.ShapeDtypeStruct((bsz, HEADS, DK, DV), F32),
    )
    out_specs = (
        pl.BlockSpec((ns, tm, D_MODEL), lambda b, t: (b, t, 0)),
        pl.BlockSpec((ns, SUB, D_CONV), lambda b, t: (b, 0, 0)),
        pl.BlockSpec((ns, HEADS, DK, DV), lambda b, t: (b, 0, 0, 0)),
    )
    scratch = [
        pltpu.VMEM((ns, 2, tm, D_MODEL), BF16),
        pltpu.VMEM((ns, tm + SUB, D_CONV), F32),
        pltpu.VMEM((ns, HEADS, DV, DK), F32),
        pltpu.VMEM((ns, tm, KDIM), F32),
        pltpu.VMEM((ns, tm, KDIM), F32),
        pltpu.VMEM((ns, tm, KDIM), F32),
        pltpu.VMEM((ns, tm, VDIM), BF16),
        pltpu.VMEM((ns, tm, VDIM), F32),
        pltpu.VMEM((ns, tm, D_CONV), BF16),
        pltpu.VMEM((ns, tm, D_MODEL), F32),
        pltpu.VMEM((ns, tm, VDIM), F32),
        pltpu.VMEM((ns, tm, D_MODEL), F32),
        pltpu.VMEM((ns, tm, D_MODEL), F32),
    ]
    return pl.pallas_call(
        _mixer_seq_kernel,
        grid=(bsz // ns, nt),
        in_specs=in_specs,
        out_specs=out_specs,
        out_shape=out_shape,
        scratch_shapes=scratch,
        compiler_params=pltpu.CompilerParams(
            dimension_semantics=("arbitrary", "arbitrary"),
            vmem_limit_bytes=VMEM_LIMIT),
        name="mixer_seq",
    )(x, x, cprev8, s0, *params, lmat)


def _mixer_step_kernel(x_ref, cp_ref, s_ref, wm_ref, wg_ref, wz_ref,
                       wa_ref, wb_ref, wo_ref, wgk_ref,
                       nmix_ref, convw_ref, bgk_ref, gain_ref, *rest):
    xo_ref, cnew_ref, sout_ref, qt_s, kt_s, at_s, v_s, o_s, ya_s = rest[-9:]
    i = pl.program_id(0)
    n = pl.num_programs(0)
    nseq = x_ref.shape[0]

    def xn_bf16():
        return _rms(x_ref[...], nmix_ref[...]).astype(BF16)

    @pl.when(i == 0)
    def _dense_in():
        xn = xn_bf16()
        u = _dot(xn, wm_ref[:, C_GC:C_GC + D_CONV]) * _dot(xn, wm_ref[:, C_H:C_H + D_CONV])
        cw = convw_ref[...]
        p1 = cp_ref[:, D_CONV:2 * D_CONV]
        yconv = cw[0:1] * cp_ref[:, 0:D_CONV] + cw[1:2] * p1 + cw[2:3] * u
        cnew_ref[:, 0:D_CONV] = p1
        cnew_ref[:, D_CONV:2 * D_CONV] = u
        ca = (_dot(xn, wm_ref[:, C_GB:C_GB + D_CONV]) * yconv).astype(BF16)
        ya_s[...] = _dot(ca, wa_ref[...])
        v_s[...] = _dot(xn, wm_ref[:, C_V:C_V + VDIM])
        qt_s[...] = (_dot(xn, wm_ref[:, C_Q:C_Q + KDIM]) * (DK ** -0.5)).T
        kt_s[...] = _dot(xn, wm_ref[:, C_K:C_K + KDIM]).T
        z = _dot(xn, wz_ref[...]).astype(BF16)
        g = _log_sigmoid(_dot(z, wgk_ref[...]) + bgk_ref[...]) * (1.0 / GATE_NORMALIZER)
        at_s[...] = jnp.exp(g).T

    shift = (nseq - i * SUB) % nseq
    lead = slice(0, 2 * SUB)
    qt = pltpu.roll(qt_s[...], shift, 1)[:, lead].astype(BF16)
    kt = pltpu.roll(kt_s[...], shift, 1)[:, lead].astype(BF16)
    at = jnp.concatenate(_split3(pltpu.roll(at_s[...], shift, 1)[:, lead]), axis=1)
    rid = lax.broadcasted_iota(jnp.int32, (2 * SUB, SUB * LANES), 0)
    cid = lax.broadcasted_iota(jnp.int32, (2 * SUB, SUB * LANES), 1)
    sel = (rid == (cid >> 7)).astype(BF16)
    sel3 = jnp.concatenate([sel, sel, sel], axis=0)
    for h in range(HEADS):
        rk = slice(h * DK, (h + 1) * DK)
        cv = slice(h * DV, (h + 1) * DV)
        qb = _dot(qt[rk], sel)
        kb = _dot(kt[rk], sel)
        ab = _dot(at[rk], sel3)
        for j in range(SUB):
            row = pl.ds(i * SUB + j, 1)
            grp = slice(j * LANES, (j + 1) * LANES)
            both = lambda c: jnp.concatenate([c[:, grp], c[:, grp]], axis=1)
            sn = s_ref[j, h] * both(ab) + both(kb) * v_s[row, cv]
            sout_ref[j, h] = sn
            o_s[row, cv] = jnp.sum(sn * both(qb), axis=0, keepdims=True)

    @pl.when(i == n - 1)
    def _dense_out():
        xn = xn_bf16()
        og = _dot(xn, wm_ref[:, C_OG:C_OG + VDIM])
        yb_in = (_head_norm(o_s[...], gain_ref[...]) * (og * _sigmoid(og))).astype(BF16)
        yb = _dot(yb_in, wb_ref[...])
        m = (_sigmoid(_dot(xn, wg_ref[:, 0:D_MODEL])) * ya_s[...]
             + _sigmoid(_dot(xn, wg_ref[:, D_MODEL:2 * D_MODEL])) * yb)
        xo_ref[...] = x_ref[...] + _dot(m.astype(BF16), wo_ref[...])


def _mixer_step(x, conv_all, s_all, s_out_prev, W, l):
    nseq = x.shape[0]
    assert nseq == LANES
    params = [W["w_main"], W["w_gate"], W["w_z"], W["w_a"], W["w_b"],
              W["w_o"], W["w_gk"], W["norm_mix"], W["conv_w"], W["b_gk"], W["gain"]]
    state_blk = (None, SUB, HEADS, DK, DV)
    in_specs = [
        _const_spec(x.shape),
        _layer_spec(conv_all, l),
        pl.BlockSpec(state_blk, lambda i: (l, i, 0, 0, 0)),
    ] + [_layer_spec(p, l) for p in params]
    args = [x, conv_all, s_all, *params]
    aliases = {}
    if s_out_prev is not None:
        in_specs.append(pl.BlockSpec(memory_space=pl.ANY))
        aliases = {len(args): 2}
        args.append(s_out_prev)
    out_shape = (
        jax.ShapeDtypeStruct(x.shape, F32),
        jax.ShapeDtypeStruct((nseq, 2 * D_CONV), F32),
        jax.ShapeDtypeStruct(s_all.shape, F32),
    )
    out_specs = (
        pl.BlockSpec(x.shape, lambda i: (0, 0)),
        pl.BlockSpec((nseq, 2 * D_CONV), lambda i: (0, 0)),
        pl.BlockSpec(state_blk, lambda i: (l, i, 0, 0, 0)),
    )
    scratch = [
        pltpu.VMEM((KDIM, nseq), F32),
        pltpu.VMEM((KDIM, nseq), F32),
        pltpu.VMEM((KDIM, nseq), F32),
        pltpu.VMEM((nseq, VDIM), F32),
        pltpu.VMEM((nseq, VDIM), F32),
        pltpu.VMEM((nseq, D_MODEL), F32),
    ]
    return pl.pallas_call(
        _mixer_step_kernel,
        grid=(nseq // SUB,),
        in_specs=in_specs,
        out_specs=out_specs,
        out_shape=out_shape,
        scratch_shapes=scratch,
        input_output_aliases=aliases,
        compiler_params=pltpu.CompilerParams(
            dimension_semantics=("arbitrary",),
            vmem_limit_bytes=VMEM_LIMIT),
        name="mixer_step",
    )(*args)


def _ffn_kernel(x_ref, small_ref, nffn_ref, wgu_ref, wdn_ref, fin_ref, o_ref, small_out_ref,
                *, final, rb):
    def ffn_rows(x):
        xn = _rms(x, nffn_ref[...]).astype(BF16)
        g = _dot(xn, wgu_ref[:, 0:D_FF])
        u = _dot(xn, wgu_ref[:, D_FF:2 * D_FF])
        hid = (g * _sigmoid(g) * u).astype(BF16)
        y = x + _dot(hid, wdn_ref[...])
        return _rms(y, fin_ref[...]) if final else y

    y0 = ffn_rows(jnp.concatenate([x_ref[0:rb, :], small_ref[...]], axis=0))
    o_ref[0:rb, :] = y0[:rb]
    small_out_ref[...] = y0[rb:]
    for r0 in range(rb, x_ref.shape[0], rb):
        o_ref[r0:r0 + rb, :] = ffn_rows(x_ref[r0:r0 + rb, :])


def _ffn(x, small, W, l, final, tm):
    n = x.shape[0]
    assert n % tm == 0
    steps = n // tm
    per_step = -(-small.shape[0] // (steps * FFN_SMALL_ROWS)) * FFN_SMALL_ROWS
    small_p = jnp.pad(small, ((0, steps * per_step - small.shape[0]), (0, 0)))
    params = [W["norm_ffn"], W["w_gu"], W["w_down"]]
    fin = W["final_norm"]
    small_spec = pl.BlockSpec((per_step, D_MODEL), lambda i: (i, 0))
    y, y_small = pl.pallas_call(
        functools.partial(_ffn_kernel, final=final, rb=min(tm, FFN_ROW_BLOCK)),
        grid=(steps,),
        in_specs=[pl.BlockSpec((tm, D_MODEL), lambda i: (i, 0)), small_spec]
        + [_layer_spec(p, l) for p in params] + [_const_spec(fin.shape)],
        out_specs=[pl.BlockSpec((tm, D_MODEL), lambda i: (i, 0)), small_spec],
        out_shape=[jax.ShapeDtypeStruct((n, D_MODEL), F32),
                   jax.ShapeDtypeStruct(small_p.shape, F32)],
        compiler_params=pltpu.CompilerParams(
            dimension_semantics=("arbitrary",),
            vmem_limit_bytes=VMEM_LIMIT),
        name="ffn",
    )(x, small_p, *params, fin)
    return y, y_small[:small.shape[0]]


def _transpose_cast_kernel(wt_ref, o_ref):
    o_ref[...] = wt_ref[0].T.astype(BF16)


def _transpose_cast(wt, row0, ncols, rb=512):
    depth, _, d = wt.shape
    return pl.pallas_call(
        _transpose_cast_kernel,
        grid=(depth, ncols // rb),
        in_specs=[pl.BlockSpec((pl.Element(1), pl.Element(rb), pl.Element(d)),
                               lambda l, i: (l, pl.multiple_of(row0 + i * rb, SUB), 0))],
        out_specs=pl.BlockSpec((None, d, rb), lambda l, i: (l, 0, i)),
        out_shape=jax.ShapeDtypeStruct((depth, d, ncols), BF16),
        compiler_params=pltpu.CompilerParams(
            dimension_semantics=("arbitrary", "arbitrary"),
            vmem_limit_bytes=VMEM_LIMIT),
        name="transpose_cast",
    )(wt)


def _prep_weights(w_in, conv_w, w_gk2, b_gk2, gla_gain, w_a_out, w_b_out, w_o,
                  norm_mix, norm_ffn, w_gu, w_down, final_norm):
    depth = w_in.shape[0]
    w_in_t = jnp.swapaxes(w_in, 1, 2)
    w_main = _transpose_cast(w_in_t, 0, D_MAIN)
    w_gate = _transpose_cast(w_in_t, Z_HI, 2 * D_MODEL)
    w_z = jnp.pad(w_in[:, :, Z_LO:Z_HI], ((0, 0), (0, 0), (0, LANES - GATE_RANK))).astype(BF16)
    w_gk = jnp.pad(w_gk2, ((0, 0), (0, LANES - GATE_RANK), (0, 0))).astype(BF16)
    return {
        "w_main": w_main,
        "w_gate": w_gate,
        "w_z": w_z,
        "w_gk": w_gk,
        "b_gk": b_gk2.reshape(depth, 1, KDIM),
        "gain": gla_gain.reshape(depth, 1, DV),
        "conv_w": conv_w,
        "w_a": w_a_out.astype(BF16),
        "w_b": w_b_out.astype(BF16),
        "w_o": w_o.astype(BF16),
        "norm_mix": norm_mix.reshape(depth, 1, D_MODEL),
        "norm_ffn": norm_ffn.reshape(depth, 1, D_MODEL),
        "w_gu": w_gu.astype(BF16),
        "w_down": w_down.astype(BF16),
        "final_norm": final_norm.reshape(1, D_MODEL),
    }


def kernel(x_prompt, x_sample, state_conv, state_gla, meta_tokens, w_in, conv_w, w_gk2, b_gk2,
           gla_gain, w_a_out, w_b_out, w_o, norm_mix, norm_ffn, w_gu, w_down, final_norm):
    depth = w_in.shape[0]
    bsz, seq, _ = x_prompt.shape
    nsmp = x_sample.shape[0]
    W = _prep_weights(w_in, conv_w, w_gk2, b_gk2, gla_gain, w_a_out, w_b_out, w_o,
                      norm_mix, norm_ffn, w_gu, w_down, final_norm)

    xm = jnp.pad(meta_tokens.astype(F32), ((CHUNK - N_META, 0), (0, 0)))[None]
    xp = x_prompt
    xs = x_sample.reshape(nsmp, D_MODEL)
    conv_all = state_conv.reshape(depth, nsmp, (CONV_W - 1) * D_CONV)
    zero_conv = jnp.zeros((1, SUB, D_CONV), F32)
    zero_state = jnp.zeros((1, HEADS, DK, DV), F32)

    p_conv, p_gla, s_conv = [], [], []
    s_gla = None
    for l in range(depth):
        last = l == depth - 1
        xm, m_conv, m_state = _mixer_seq(xm, zero_conv, zero_state, W, l, CHUNK)
        xp, pc, ps = _mixer_seq(xp, m_conv, m_state, W, l, PROMPT_TILE)
        xs, sc, s_gla = _mixer_step(xs, conv_all, state_gla, s_gla, W, l)
        small = xs if last else jnp.concatenate([xs, xm[0]], axis=0)
        yp, ysmall = _ffn(xp.reshape(bsz * seq, D_MODEL), small, W, l, last, 1024)
        xp, xs = yp.reshape(bsz, seq, D_MODEL), ysmall[:nsmp]
        if not last:
            xm = ysmall[nsmp:][None]
        p_conv.append(pc[:, SUB - (CONV_W - 1):, :])
        p_gla.append(ps)
        s_conv.append(sc.reshape(nsmp, CONV_W - 1, D_CONV))

    return (xp, xs.reshape(nsmp, 1, D_MODEL), jnp.stack(p_conv), jnp.stack(p_gla),
            jnp.stack(s_conv), s_gla)
```

```python
import functools

import jax
import jax.numpy as jnp
from jax import lax
from jax.experimental import pallas as pl
from jax.experimental.pallas import tpu as pltpu

F32 = jnp.float32
BF16 = jnp.bfloat16

D_MODEL = 1024
N_META = 16
D_CONV = D_MODEL
CONV_W = 3
HEADS = 4
DK = 128
DV = 256
KDIM = HEADS * DK
VDIM = HEADS * DV
GATE_RANK = 16
GATE_NORMALIZER = 16.0
D_FF = 2816
EPS = 1e-6
LOG2_E = 1.4426950408889634

CHUNK = 64
SUB = 8
NB = CHUNK // SUB
LANES = 128
COLB = 512
FFN_ROW_BLOCK = 256
FFN_SMALL_ROWS = 16
PROMPT_TILE = 256
SEQS_PER_STEP = 1
VMEM_LIMIT = 60 * 1024 * 1024

C_GB, C_GC, C_H = 0, 1024, 2048
C_Q, C_K, C_V, C_OG = 3072, 3584, 4096, 5120
D_MAIN = 6144
Z_LO = D_MAIN
Z_HI = Z_LO + GATE_RANK


def _rms(x, g):
    ms = jnp.mean(x * x, axis=-1, keepdims=True)
    return x * lax.rsqrt(ms + EPS) * g


def _log_sigmoid(x):
    return jnp.minimum(x, 0.0) - jnp.log(1.0 + jnp.exp(-jnp.abs(x)))


def _sigmoid(x):
    return 1.0 / (1.0 + jnp.exp(-x))


def _dot(a, b):
    return jnp.dot(a, b, preferred_element_type=F32)


def _dot_nt(a, b):
    return lax.dot_general(a, b, (((1,), (1,)), ((), ())), preferred_element_type=F32)


def _dot_tn(a, b):
    return lax.dot_general(a, b, (((0,), (0,)), ((), ())), preferred_element_type=F32)


def _split3(x):
    x1 = x.astype(BF16)
    r1 = x - x1.astype(F32)
    x2 = r1.astype(BF16)
    r2 = r1 - x2.astype(F32)
    return x1, x2, r2.astype(BF16)


def _head_norm(o, gain):
    return jnp.concatenate(
        [_rms(o[:, h * DV:(h + 1) * DV], gain) for h in range(HEADS)], axis=1)


def _interleave(slots, pieces):
    done = 0
    for i, slot_fn in enumerate(slots):
        slot_fn()
        upto = ((i + 1) * len(pieces) + len(slots) - 1) // len(slots)
        while done < upto:
            pieces[done]()
            done += 1


def _mixer_seq_kernel(x_ref, xnext_ref, cprev_ref, s0_ref, wm_ref, wg_ref, wz_ref, wa_ref, wb_ref,
                      wo_ref, wgk_ref, nmix_ref, convw_ref, bgk_ref, gain_ref,
                      xo_ref, cnew_ref, sout_ref,
                      xn_s, ubuf, st_s, q_s, k_s, b_s, v_s, o_s, ca_s, ya_s, og_s, ga_s, gb_s):
    t = pl.program_id(1)
    nt = pl.num_programs(1)
    nseq, tm = x_ref.shape[0], x_ref.shape[1]
    xn_buf = lax.bitwise_and(t, 1)
    xn_next_buf = lax.bitwise_and(t + 1, 1)

    @pl.when(t == 0)
    def _init():
        for p in range(nseq):
            ubuf[p, 0:SUB, :] = cprev_ref[0]
            xn_s[p, 0] = _rms(x_ref[p], nmix_ref[...]).astype(BF16)
            for h in range(HEADS):
                st_s[p, h] = s0_ref[0, h].T

    ti = lax.broadcasted_iota(jnp.int32, (CHUNK, CHUNK), 0)
    si = lax.broadcasted_iota(jnp.int32, (CHUNK, CHUNK), 1)
    diag_sel = [(si == ((ti >> 3) << 3) + j) & ((ti & 7) >= j) for j in range(SUB)]
    lane_blk = lax.broadcasted_iota(jnp.int32, (SUB, CHUNK), 1) >> 3
    stack_off = [sum(CHUNK - SUB * (i + 1) for i in range(j)) for j in range(NB)]
    ncb = D_MODEL // COLB

    def cols_of(cb):
        return slice(cb * COLB, (cb + 1) * COLB)

    def make_stream(p):
        xn = xn_s[p, xn_buf]
        vals = {}

        def proj(w_ref, c0, n):
            return _dot(xn, w_ref[:, c0:c0 + n])

        def head_gate():
            z = proj(wz_ref, 0, LANES).astype(BF16)
            g = (_log_sigmoid(_dot(z, wgk_ref[...]) + bgk_ref[...])
                 * (LOG2_E / GATE_NORMALIZER))
            row = lax.broadcasted_iota(jnp.int32, g.shape, 0) & (CHUNK - 1)
            step = 1
            while step < CHUNK:
                g = g + jnp.where(row >= step, pltpu.roll(g, step, 0), 0.0)
                step *= 2
            b_s[p] = g

        def head_q():
            q_s[p] = proj(wm_ref, C_Q, KDIM) * (DK ** -0.5)

        def head_k():
            k_s[p] = proj(wm_ref, C_K, KDIM)

        def head_v():
            v_s[p] = proj(wm_ref, C_V, VDIM).astype(BF16)

        head = [head_gate, head_q, head_k, head_v]

        def mid_u(cb):
            u = proj(wm_ref, C_GC + cb * COLB, COLB) * proj(wm_ref, C_H + cb * COLB, COLB)
            ubuf[p, SUB:SUB + tm, cols_of(cb)] = u

        def mid_conv(cb):
            cols = cols_of(cb)
            cw = convw_ref[:, cols]
            yconv = (cw[0:1] * ubuf[p, SUB - 2:SUB - 2 + tm, cols]
                     + cw[1:2] * ubuf[p, SUB - 1:SUB - 1 + tm, cols]
                     + cw[2:3] * ubuf[p, SUB:SUB + tm, cols])
            ca_s[p, :, cols] = (proj(wm_ref, C_GB + cb * COLB, COLB) * yconv).astype(BF16)
            tail = ubuf[p, tm:tm + SUB, cols]
            ubuf[p, 0:SUB, cols] = tail
            cnew_ref[p, :, cols] = tail

        def mid_ya(cb):
            ya_s[p, :, cols_of(cb)] = _dot(ca_s[p], wa_ref[:, cols_of(cb)])

        def mid_norm_next():
            xn_s[p, xn_next_buf] = _rms(xnext_ref[p], nmix_ref[...]).astype(BF16)

        def mid_og(cb):
            og = proj(wm_ref, C_OG + cb * COLB, COLB)
            og_s[p, :, cols_of(cb)] = og * _sigmoid(og)

        def mid_ga(cb):
            ga_s[p, :, cols_of(cb)] = _sigmoid(proj(wg_ref, cb * COLB, COLB))

        def mid_gb(cb):
            gb_s[p, :, cols_of(cb)] = _sigmoid(proj(wg_ref, D_MODEL + cb * COLB, COLB))

        mid = []
        for cb in range(ncb):
            mid += [functools.partial(mid_u, cb), functools.partial(mid_conv, cb)]
        mid.insert(2, mid_norm_next)
        for fn in (mid_ya, mid_og, mid_ga, mid_gb):
            mid += [functools.partial(fn, cb) for cb in range(ncb)]

        def gla_diag(c, h):
            r0 = c * CHUNK
            rows = slice(r0, r0 + CHUNK)
            kcols = slice(h * DK, (h + 1) * DK)
            qc = q_s[p, rows, kcols]
            bc = b_s[p, rows, kcols]

            def row_of_each_block(ref, j):
                return jnp.concatenate(
                    [jnp.broadcast_to(ref[p, r0 + SUB * i + j:r0 + SUB * i + j + 1, kcols],
                                      (SUB, DK)) for i in range(NB)], axis=0)

            acc = jnp.zeros((CHUNK, CHUNK), F32)
            for j in range(SUB):
                kj = row_of_each_block(k_s, j)
                bj = row_of_each_block(b_s, j)
                rj = jnp.sum(qc * kj * jnp.exp2(bc - bj), axis=-1, keepdims=True)
                acc = jnp.where(diag_sel[j], rj, acc)
            vals[("diag", c, h)] = acc

        def gla_rest(c, h):
            r0 = c * CHUNK
            rows = slice(r0, r0 + CHUNK)
            kcols = slice(h * DK, (h + 1) * DK)
            vcols = slice(h * DV, (h + 1) * DV)
            qc = q_s[p, rows, kcols]
            kc = k_s[p, rows, kcols]
            bc = b_s[p, rows, kcols]
            vc = v_s[p, rows, vcols]
            st = st_s[p, h]
            bend = [b_s[p, r0 + SUB * j + SUB - 1:r0 + SUB * (j + 1), kcols] for j in range(NB)]
            bend_rows = jnp.concatenate([jnp.broadcast_to(e, (SUB, DK)) for e in bend], axis=0)
            blast = bend[NB - 1]
            qhat = (qc * jnp.exp2(bc)).astype(BF16)
            o = _dot_nt(qhat, st.astype(BF16))
            kt = (kc * jnp.exp2(bend_rows - bc)).astype(BF16)
            qs = [qc[SUB * (j + 1):] * jnp.exp2(bc[SUB * (j + 1):] - bend[j])
                  for j in range(NB - 1)]
            rm = _dot_nt(jnp.concatenate(qs, axis=0).astype(BF16), kt)
            arows = [jnp.zeros((SUB, CHUNK), F32)]
            for i in range(1, NB):
                acc = jnp.zeros((SUB, CHUNK), F32)
                for j in range(i):
                    s0 = stack_off[j] + SUB * (i - j - 1)
                    acc = jnp.where(lane_blk == j, rm[s0:s0 + SUB], acc)
                arows.append(acc)
            a = jnp.concatenate(arows, axis=0) + vals.pop(("diag", c, h))
            o_s[p, rows, vcols] = o + _dot(a.astype(BF16), vc)
            khat = (kc * jnp.exp2(blast - bc)).astype(BF16)
            st_s[p, h] = st * jnp.exp2(blast) + _dot_tn(vc, khat)

        gla = []
        for c in range(tm // CHUNK):
            for h in range(HEADS):
                gla += [functools.partial(gla_diag, c, h), functools.partial(gla_rest, c, h)]

        def tail_yb(rs):
            yb_in = (_head_norm(o_s[p, rs, :], gain_ref[...]) * og_s[p, rs, :]).astype(BF16)
            vals[("yb", rs.start)] = _dot(yb_in, wb_ref[...])

        def tail_out(rs):
            m = ga_s[p, rs, :] * ya_s[p, rs, :] + gb_s[p, rs, :] * vals.pop(("yb", rs.start))
            xo_ref[p, rs, :] = x_ref[p, rs, :] + _dot(m.astype(BF16), wo_ref[...])

        def tail(rs):
            return [functools.partial(tail_yb, rs), functools.partial(tail_out, rs)]

        return head, mid, gla, tail

    for p in range(nseq):
        head, mid, gla, tail = make_stream(p)
        for fn in head:
            fn()
        _interleave(gla, mid)
        for fn in tail(slice(0, tm)):
            fn()

    @pl.when(t == nt - 1)
    def _fin():
        for p in range(nseq):
            for h in range(HEADS):
                sout_ref[p, h] = st_s[p, h].T


def _const_spec(shape):
    nd = len(shape)
    return pl.BlockSpec(shape, lambda *_: (0,) * nd, pipeline_mode=pl.Buffered(1))


def _layer_spec(arr, l):
    nd = arr.ndim
    return pl.BlockSpec((None,) + arr.shape[1:], lambda *_: (l,) + (0,) * (nd - 1),
                        pipeline_mode=pl.Buffered(1))


def _mixer_seq(x, cprev8, s0, W, l, tm):
    bsz, T, _ = x.shape
    assert T % tm == 0 and tm % CHUNK == 0
    ns = SEQS_PER_STEP if bsz % SEQS_PER_STEP == 0 else 1
    params = [W["w_main"], W["w_gate"], W["w_z"], W["w_a"], W["w_b"], W["w_o"], W["w_gk"],
              W["norm_mix"], W["conv_w"], W["b_gk"], W["gain"]]
    nt = T // tm
    in_specs = [
        pl.BlockSpec((ns, tm, D_MODEL), lambda b, t: (b, t, 0)),
        pl.BlockSpec((ns, tm, D_MODEL), lambda b, t: (b, jnp.minimum(t + 1, nt - 1), 0)),
        pl.BlockSpec((1, SUB, D_CONV), lambda b, t: (0, 0, 0)),
        pl.BlockSpec((1, HEADS, DK, DV), lambda b, t: (0, 0, 0, 0)),
    ] + [_layer_spec(p, l) for p in params]
    out_shape = (
        jax.ShapeDtypeStruct((bsz, T, D_MODEL), F32),
        jax.ShapeDtypeStruct((bsz, SUB, D_CONV), F32),
        jax.ShapeDtypeStruct((bsz, HEADS, DK, DV), F32),
    )
    out_specs = (
        pl.BlockSpec((ns, tm, D_MODEL), lambda b, t: (b, t, 0)),
        pl.BlockSpec((ns, SUB, D_CONV), lambda b, t: (b, 0, 0)),
        pl.BlockSpec((ns, HEADS, DK, DV), lambda b, t: (b, 0, 0, 0)),
    )
    scratch = [
        pltpu.VMEM((ns, 2, tm, D_MODEL), BF16),
        pltpu.VMEM((ns, tm + SUB, D_CONV), F32),
        pltpu.VMEM((ns, HEADS, DV, DK), F32),
        pltpu.VMEM((ns, tm, KDIM), F32),
        pltpu.VMEM((ns, tm, KDIM), F32),
        pltpu.VMEM((ns, tm, KDIM), F32),
        pltpu.VMEM((ns, tm, VDIM), BF16),
        pltpu.VMEM((ns, tm, VDIM), F32),
        pltpu.VMEM((ns, tm, D_CONV), BF16),
        pltpu.VMEM((ns, tm, D_MODEL), F32),
        pltpu.VMEM((ns, tm, VDIM), F32),
        pltpu.VMEM((ns, tm, D_MODEL), F32),
        pltpu.VMEM((ns, tm, D_MODEL), F32),
    ]
    return pl.pallas_call(
        _mixer_seq_kernel,
        grid=(bsz // ns, nt),
        in_specs=in_specs,
        out_specs=out_specs,
        out_shape=out_shape,
        scratch_shapes=scratch,
        compiler_params=pltpu.CompilerParams(
            dimension_semantics=("arbitrary", "arbitrary"),
            vmem_limit_bytes=VMEM_LIMIT),
        name="mixer_seq",
    )(x, x, cprev8, s0, *params)


def _mixer_step_kernel(x_ref, cp_ref, s_ref, wm_ref, wg_ref, wz_ref,
                       wa_ref, wb_ref, wo_ref, wgk_ref,
                       nmix_ref, convw_ref, bgk_ref, gain_ref, *rest):
    xo_ref, cnew_ref, sout_ref, qt_s, kt_s, at_s, v_s, o_s, ya_s = rest[-9:]
    i = pl.program_id(0)
    n = pl.num_programs(0)
    nseq = x_ref.shape[0]

    def xn_bf16():
        return _rms(x_ref[...], nmix_ref[...]).astype(BF16)

    @pl.when(i == 0)
    def _dense_in():
        xn = xn_bf16()
        u = _dot(xn, wm_ref[:, C_GC:C_GC + D_CONV]) * _dot(xn, wm_ref[:, C_H:C_H + D_CONV])
        cw = convw_ref[...]
        p1 = cp_ref[:, D_CONV:2 * D_CONV]
        yconv = cw[0:1] * cp_ref[:, 0:D_CONV] + cw[1:2] * p1 + cw[2:3] * u
        cnew_ref[:, 0:D_CONV] = p1
        cnew_ref[:, D_CONV:2 * D_CONV] = u
        ca = (_dot(xn, wm_ref[:, C_GB:C_GB + D_CONV]) * yconv).astype(BF16)
        ya_s[...] = _dot(ca, wa_ref[...])
        v_s[...] = _dot(xn, wm_ref[:, C_V:C_V + VDIM])
        qt_s[...] = (_dot(xn, wm_ref[:, C_Q:C_Q + KDIM]) * (DK ** -0.5)).T
        kt_s[...] = _dot(xn, wm_ref[:, C_K:C_K + KDIM]).T
        z = _dot(xn, wz_ref[...]).astype(BF16)
        g = _log_sigmoid(_dot(z, wgk_ref[...]) + bgk_ref[...]) * (1.0 / GATE_NORMALIZER)
        at_s[...] = jnp.exp(g).T

    shift = (nseq - i * SUB) % nseq
    lead = slice(0, 2 * SUB)
    qt = pltpu.roll(qt_s[...], shift, 1)[:, lead].astype(BF16)
    kt = pltpu.roll(kt_s[...], shift, 1)[:, lead].astype(BF16)
    at = jnp.concatenate(_split3(pltpu.roll(at_s[...], shift, 1)[:, lead]), axis=1)
    rid = lax.broadcasted_iota(jnp.int32, (2 * SUB, SUB * LANES), 0)
    cid = lax.broadcasted_iota(jnp.int32, (2 * SUB, SUB * LANES), 1)
    sel = (rid == (cid >> 7)).astype(BF16)
    sel3 = jnp.concatenate([sel, sel, sel], axis=0)
    for h in range(HEADS):
        rk = slice(h * DK, (h + 1) * DK)
        cv = slice(h * DV, (h + 1) * DV)
        qb = _dot(qt[rk], sel)
        kb = _dot(kt[rk], sel)
        ab = _dot(at[rk], sel3)
        for j in range(SUB):
            row = pl.ds(i * SUB + j, 1)
            grp = slice(j * LANES, (j + 1) * LANES)
            both = lambda c: jnp.concatenate([c[:, grp], c[:, grp]], axis=1)
            sn = s_ref[j, h] * both(ab) + both(kb) * v_s[row, cv]
            sout_ref[j, h] = sn
            o_s[row, cv] = jnp.sum(sn * both(qb), axis=0, keepdims=True)

    @pl.when(i == n - 1)
    def _dense_out():
        xn = xn_bf16()
        og = _dot(xn, wm_ref[:, C_OG:C_OG + VDIM])
        yb_in = (_head_norm(o_s[...], gain_ref[...]) * (og * _sigmoid(og))).astype(BF16)
        yb = _dot(yb_in, wb_ref[...])
        m = (_sigmoid(_dot(xn, wg_ref[:, 0:D_MODEL])) * ya_s[...]
             + _sigmoid(_dot(xn, wg_ref[:, D_MODEL:2 * D_MODEL])) * yb)
        xo_ref[...] = x_ref[...] + _dot(m.astype(BF16), wo_ref[...])


def _mixer_step(x, conv_all, s_all, s_out_prev, W, l):
    nseq = x.shape[0]
    assert nseq == LANES
    params = [W["w_main"], W["w_gate"], W["w_z"], W["w_a"], W["w_b"],
              W["w_o"], W["w_gk"], W["norm_mix"], W["conv_w"], W["b_gk"], W["gain"]]
    state_blk = (None, SUB, HEADS, DK, DV)
    in_specs = [
        _const_spec(x.shape),
        _layer_spec(conv_all, l),
        pl.BlockSpec(state_blk, lambda i: (l, i, 0, 0, 0)),
    ] + [_layer_spec(p, l) for p in params]
    args = [x, conv_all, s_all, *params]
    aliases = {}
    if s_out_prev is not None:
        in_specs.append(pl.BlockSpec(memory_space=pl.ANY))
        aliases = {len(args): 2}
        args.append(s_out_prev)
    out_shape = (
        jax.ShapeDtypeStruct(x.shape, F32),
        jax.ShapeDtypeStruct((nseq, 2 * D_CONV), F32),
        jax.ShapeDtypeStruct(s_all.shape, F32),
    )
    out_specs = (
        pl.BlockSpec(x.shape, lambda i: (0, 0)),
        pl.BlockSpec((nseq, 2 * D_CONV), lambda i: (0, 0)),
        pl.BlockSpec(state_blk, lambda i: (l, i, 0, 0, 0)),
    )
    scratch = [
        pltpu.VMEM((KDIM, nseq), F32),
        pltpu.VMEM((KDIM, nseq), F32),
        pltpu.VMEM((KDIM, nseq), F32),
        pltpu.VMEM((nseq, VDIM), F32),
        pltpu.VMEM((nseq, VDIM), F32),
        pltpu.VMEM((nseq, D_MODEL), F32),
    ]
    return pl.pallas_call(
        _mixer_step_kernel,
        grid=(nseq // SUB,),
        in_specs=in_specs,
        out_specs=out_specs,
        out_shape=out_shape,
        scratch_shapes=scratch,
        input_output_aliases=aliases,
        compiler_params=pltpu.CompilerParams(
            dimension_semantics=("arbitrary",),
            vmem_limit_bytes=VMEM_LIMIT),
        name="mixer_step",
    )(*args)


def _ffn_kernel(x_ref, small_ref, nffn_ref, wgu_ref, wdn_ref, fin_ref, o_ref, small_out_ref,
                *, final, rb):
    def ffn_rows(x):
        xn = _rms(x, nffn_ref[...]).astype(BF16)
        g = _dot(xn, wgu_ref[:, 0:D_FF])
        u = _dot(xn, wgu_ref[:, D_FF:2 * D_FF])
        hid = (g * _sigmoid(g) * u).astype(BF16)
        y = x + _dot(hid, wdn_ref[...])
        return _rms(y, fin_ref[...]) if final else y

    y0 = ffn_rows(jnp.concatenate([x_ref[0:rb, :], small_ref[...]], axis=0))
    o_ref[0:rb, :] = y0[:rb]
    small_out_ref[...] = y0[rb:]
    for r0 in range(rb, x_ref.shape[0], rb):
        o_ref[r0:r0 + rb, :] = ffn_rows(x_ref[r0:r0 + rb, :])


def _ffn(x, small, W, l, final, tm):
    n = x.shape[0]
    assert n % tm == 0
    steps = n // tm
    per_step = -(-small.shape[0] // (steps * FFN_SMALL_ROWS)) * FFN_SMALL_ROWS
    small_p = jnp.pad(small, ((0, steps * per_step - small.shape[0]), (0, 0)))
    params = [W["norm_ffn"], W["w_gu"], W["w_down"]]
    fin = W["final_norm"]
    small_spec = pl.BlockSpec((per_step, D_MODEL), lambda i: (i, 0))
    y, y_small = pl.pallas_call(
        functools.partial(_ffn_kernel, final=final, rb=min(tm, FFN_ROW_BLOCK)),
        grid=(steps,),
        in_specs=[pl.BlockSpec((tm, D_MODEL), lambda i: (i, 0)), small_spec]
        + [_layer_spec(p, l) for p in params] + [_const_spec(fin.shape)],
        out_specs=[pl.BlockSpec((tm, D_MODEL), lambda i: (i, 0)), small_spec],
        out_shape=[jax.ShapeDtypeStruct((n, D_MODEL), F32),
                   jax.ShapeDtypeStruct(small_p.shape, F32)],
        compiler_params=pltpu.CompilerParams(
            dimension_semantics=("arbitrary",),
            vmem_limit_bytes=VMEM_LIMIT),
        name="ffn",
    )(x, small_p, *params, fin)
    return y, y_small[:small.shape[0]]


def _transpose_cast_kernel(wt_ref, o_ref):
    o_ref[...] = wt_ref[0].T.astype(BF16)


def _transpose_cast(wt, row0, ncols, rb=512):
    depth, _, d = wt.shape
    return pl.pallas_call(
        _transpose_cast_kernel,
        grid=(depth, ncols // rb),
        in_specs=[pl.BlockSpec((pl.Element(1), pl.Element(rb), pl.Element(d)),
                               lambda l, i: (l, pl.multiple_of(row0 + i * rb, SUB), 0))],
        out_specs=pl.BlockSpec((None, d, rb), lambda l, i: (l, 0, i)),
        out_shape=jax.ShapeDtypeStruct((depth, d, ncols), BF16),
        compiler_params=pltpu.CompilerParams(
            dimension_semantics=("arbitrary", "arbitrary"),
            vmem_limit_bytes=VMEM_LIMIT),
        name="transpose_cast",
    )(wt)


def _prep_weights(w_in, conv_w, w_gk2, b_gk2, gla_gain, w_a_out, w_b_out, w_o,
                  norm_mix, norm_ffn, w_gu, w_down, final_norm):
    depth = w_in.shape[0]
    w_in_t = jnp.swapaxes(w_in, 1, 2)
    w_main = _transpose_cast(w_in_t, 0, D_MAIN)
    w_gate = _transpose_cast(w_in_t, Z_HI, 2 * D_MODEL)
    w_z = jnp.pad(w_in[:, :, Z_LO:Z_HI], ((0, 0), (0, 0), (0, LANES - GATE_RANK))).astype(BF16)
    w_gk = jnp.pad(w_gk2, ((0, 0), (0, LANES - GATE_RANK), (0, 0))).astype(BF16)
    return {
        "w_main": w_main,
        "w_gate": w_gate,
        "w_z": w_z,
        "w_gk": w_gk,
        "b_gk": b_gk2.reshape(depth, 1, KDIM),
        "gain": gla_gain.reshape(depth, 1, DV),
        "conv_w": conv_w,
        "w_a": w_a_out.astype(BF16),
        "w_b": w_b_out.astype(BF16),
        "w_o": w_o.astype(BF16),
        "norm_mix": norm_mix.reshape(depth, 1, D_MODEL),
        "norm_ffn": norm_ffn.reshape(depth, 1, D_MODEL),
        "w_gu": w_gu.astype(BF16),
        "w_down": w_down.astype(BF16),
        "final_norm": final_norm.reshape(1, D_MODEL),
    }


def kernel(x_prompt, x_sample, state_conv, state_gla, meta_tokens, w_in, conv_w, w_gk2, b_gk2,
           gla_gain, w_a_out, w_b_out, w_o, norm_mix, norm_ffn, w_gu, w_down, final_norm):
    depth = w_in.shape[0]
    bsz, seq, _ = x_prompt.shape
    nsmp = x_sample.shape[0]
    W = _prep_weights(w_in, conv_w, w_gk2, b_gk2, gla_gain, w_a_out, w_b_out, w_o,
                      norm_mix, norm_ffn, w_gu, w_down, final_norm)

    xm = jnp.pad(meta_tokens.astype(F32), ((CHUNK - N_META, 0), (0, 0)))[None]
    xp = x_prompt
    xs = x_sample.reshape(nsmp, D_MODEL)
    conv_all = state_conv.reshape(depth, nsmp, (CONV_W - 1) * D_CONV)
    zero_conv = jnp.zeros((1, SUB, D_CONV), F32)
    zero_state = jnp.zeros((1, HEADS, DK, DV), F32)

    p_conv, p_gla, s_conv = [], [], []
    s_gla = None
    for l in range(depth):
        last = l == depth - 1
        xm, m_conv, m_state = _mixer_seq(xm, zero_conv, zero_state, W, l, CHUNK)
        xp, pc, ps = _mixer_seq(xp, m_conv, m_state, W, l, PROMPT_TILE)
        xs, sc, s_gla = _mixer_step(xs, conv_all, state_gla, s_gla, W, l)
        small = xs if last else jnp.concatenate([xs, xm[0]], axis=0)
        yp, ysmall = _ffn(xp.reshape(bsz * seq, D_MODEL), small, W, l, last, 1024)
        xp, xs = yp.reshape(bsz, seq, D_MODEL), ysmall[:nsmp]
        if not last:
            xm = ysmall[nsmp:][None]
        p_conv.append(pc[:, SUB - (CONV_W - 1):, :])
        p_gla.append(ps)
        s_conv.append(sc.reshape(nsmp, CONV_W - 1, D_CONV))

    return (xp, xs.reshape(nsmp, 1, D_MODEL), jnp.stack(p_conv), jnp.stack(p_gla),
            jnp.stack(s_conv), s_gla)
```

```python
import functools

import jax
import jax.numpy as jnp
from jax import lax
from jax.experimental import pallas as pl
from jax.experimental.pallas import tpu as pltpu

F32 = jnp.float32
BF16 = jnp.bfloat16

D_MODEL = 1024
N_META = 16
D_CONV = D_MODEL
CONV_W = 3
HEADS = 4
DK = 128
DV = 256
KDIM = HEADS * DK
VDIM = HEADS * DV
GATE_RANK = 16
GATE_NORMALIZER = 16.0
D_FF = 2816
EPS = 1e-6
LOG2_E = 1.4426950408889634

CHUNK = 64
SUB = 8
NB = CHUNK // SUB
LANES = 128
COLB = 512
FFN_ROW_BLOCK = 256
FFN_SMALL_ROWS = 16
PROMPT_TILE = 256
SEQS_PER_STEP = 1
VMEM_LIMIT = 60 * 1024 * 1024

C_GB, C_GC, C_H = 0, 1024, 2048
C_Q, C_K, C_V, C_OG = 3072, 3584, 4096, 5120
D_MAIN = 6144
Z_LO = D_MAIN
Z_HI = Z_LO + GATE_RANK


def _rms(x, g):
    ms = jnp.mean(x * x, axis=-1, keepdims=True)
    return x * lax.rsqrt(ms + EPS) * g


def _log_sigmoid(x):
    return jnp.minimum(x, 0.0) - jnp.log(1.0 + jnp.exp(-jnp.abs(x)))


def _sigmoid(x):
    return 1.0 / (1.0 + jnp.exp(-x))


def _dot(a, b):
    return jnp.dot(a, b, preferred_element_type=F32)


def _dot_nt(a, b):
    return lax.dot_general(a, b, (((1,), (1,)), ((), ())), preferred_element_type=F32)


def _dot_tn(a, b):
    return lax.dot_general(a, b, (((0,), (0,)), ((), ())), preferred_element_type=F32)


def _split3(x):
    x1 = x.astype(BF16)
    r1 = x - x1.astype(F32)
    x2 = r1.astype(BF16)
    r2 = r1 - x2.astype(F32)
    return x1, x2, r2.astype(BF16)


def _head_norm(o, gain):
    return jnp.concatenate(
        [_rms(o[:, h * DV:(h + 1) * DV], gain) for h in range(HEADS)], axis=1)


def _interleave(slots, pieces):
    done = 0
    for i, slot_fn in enumerate(slots):
        slot_fn()
        upto = ((i + 1) * len(pieces) + len(slots) - 1) // len(slots)
        while done < upto:
            pieces[done]()
            done += 1


def _ride_along(step, cast_in, cast_out, turn_in, turn_out):
    for src, dst in zip(cast_in, cast_out):
        dst[...] = src[...].astype(BF16)
    for src, dst in zip(turn_in, turn_out):
        dst[...] = src[0].T.astype(BF16)


def _ride_specs(steps, step_of, cast_jobs, turn_jobs):
    in_specs, args, out_specs, out_shape = [], [], [], []
    for arr, layer in cast_jobs:
        _, rows, cols = arr.shape
        assert rows % (steps * 2 * SUB) == 0
        blk = (None, rows // steps, cols)
        in_specs.append(pl.BlockSpec(blk, lambda *g, layer=layer: (layer, step_of(*g), 0)))
        out_specs.append(pl.BlockSpec(blk, lambda *g: (0, step_of(*g), 0)))
        out_shape.append(jax.ShapeDtypeStruct((1, rows, cols), BF16))
        args.append(arr)
    turn_specs, turn_shape = [], []
    for wt, layer in turn_jobs:
        d = wt.shape[2]
        width = (D_MAIN + 2 * D_MODEL) // steps
        assert width % LANES == 0 and D_MAIN % width == 0
        n_main = D_MAIN // width

        def row_of(*g, n_main=n_main, width=width):
            s = step_of(*g)
            return pl.multiple_of(jnp.where(s < n_main, s * width, Z_HI + (s - n_main) * width), SUB)

        in_specs.append(pl.BlockSpec(
            (pl.Element(1), pl.Element(width), pl.Element(d)),
            lambda *g, layer=layer, row_of=row_of: (layer, row_of(*g), 0)))
        turn_specs.append(pl.BlockSpec((None, d, width), lambda *g: (0, 0, step_of(*g))))
        turn_shape.append(jax.ShapeDtypeStruct((1, d, D_MAIN + 2 * D_MODEL), BF16))
        args.append(wt)
    return in_specs, args, out_specs + turn_specs, out_shape + turn_shape


def _mixer_seq_kernel(*refs, n_cast, n_turn):
    (x_ref, xnext_ref, cprev_ref, s0_ref, wm_ref, wg_ref, wz_ref, wa_ref, wb_ref,
     wo_ref, wgk_ref, nmix_ref, convw_ref, bgk_ref, gain_ref) = refs[:15]
    n_in = 15 + n_cast + n_turn
    cast_in, turn_in = refs[15:15 + n_cast], refs[15 + n_cast:n_in]
    xo_ref, cnew_ref, sout_ref = refs[n_in:n_in + 3]
    cast_out = refs[n_in + 3:n_in + 3 + n_cast]
    turn_out = refs[n_in + 3 + n_cast:n_in + 3 + n_cast + n_turn]
    (xn_s, ubuf, st_s, q_s, k_s, b_s, v_s, o_s, ca_s, ya_s, og_s, ga_s,
     gb_s) = refs[n_in + 3 + n_cast + n_turn:]
    t = pl.program_id(1)
    nt = pl.num_programs(1)
    nseq, tm = x_ref.shape[0], x_ref.shape[1]
    xn_buf = lax.bitwise_and(t, 1)
    xn_next_buf = lax.bitwise_and(t + 1, 1)


    @pl.when(t == 0)
    def _init():
        for p in range(nseq):
            ubuf[p, 0:SUB, :] = cprev_ref[0]
            xn_s[p, 0] = _rms(x_ref[p], nmix_ref[...]).astype(BF16)
            for h in range(HEADS):
                st_s[p, h] = s0_ref[0, h].T

    ti = lax.broadcasted_iota(jnp.int32, (CHUNK, CHUNK), 0)
    si = lax.broadcasted_iota(jnp.int32, (CHUNK, CHUNK), 1)
    diag_sel = [(si == ((ti >> 3) << 3) + j) & ((ti & 7) >= j) for j in range(SUB)]
    lane_blk = lax.broadcasted_iota(jnp.int32, (SUB, CHUNK), 1) >> 3
    stack_off = [sum(CHUNK - SUB * (i + 1) for i in range(j)) for j in range(NB)]
    ncb = D_MODEL // COLB

    def cols_of(cb):
        return slice(cb * COLB, (cb + 1) * COLB)

    def make_stream(p):
        xn = xn_s[p, xn_buf]
        vals = {}

        def proj(w_ref, c0, n):
            return _dot(xn, w_ref[:, c0:c0 + n])

        def head_gate():
            z = proj(wz_ref, 0, LANES).astype(BF16)
            g = (_log_sigmoid(_dot(z, wgk_ref[...]) + bgk_ref[...])
                 * (LOG2_E / GATE_NORMALIZER))
            row = lax.broadcasted_iota(jnp.int32, g.shape, 0) & (CHUNK - 1)
            step = 1
            while step < CHUNK:
                g = g + jnp.where(row >= step, pltpu.roll(g, step, 0), 0.0)
                step *= 2
            b_s[p] = g

        def head_q():
            q_s[p] = proj(wm_ref, C_Q, KDIM) * (DK ** -0.5)

        def head_k():
            k_s[p] = proj(wm_ref, C_K, KDIM)

        def head_v():
            v_s[p] = proj(wm_ref, C_V, VDIM).astype(BF16)

        head = [head_gate, head_q, head_k, head_v]

        def mid_u(cb):
            u = proj(wm_ref, C_GC + cb * COLB, COLB) * proj(wm_ref, C_H + cb * COLB, COLB)
            ubuf[p, SUB:SUB + tm, cols_of(cb)] = u

        def mid_conv(cb):
            cols = cols_of(cb)
            cw = convw_ref[:, cols]
            yconv = (cw[0:1] * ubuf[p, SUB - 2:SUB - 2 + tm, cols]
                     + cw[1:2] * ubuf[p, SUB - 1:SUB - 1 + tm, cols]
                     + cw[2:3] * ubuf[p, SUB:SUB + tm, cols])
            ca_s[p, :, cols] = (proj(wm_ref, C_GB + cb * COLB, COLB) * yconv).astype(BF16)
            tail = ubuf[p, tm:tm + SUB, cols]
            ubuf[p, 0:SUB, cols] = tail
            cnew_ref[p, :, cols] = tail

        def mid_ya(cb):
            ya_s[p, :, cols_of(cb)] = _dot(ca_s[p], wa_ref[:, cols_of(cb)])

        def mid_norm_next():
            xn_s[p, xn_next_buf] = _rms(xnext_ref[p], nmix_ref[...]).astype(BF16)

        def mid_og(cb):
            og = proj(wm_ref, C_OG + cb * COLB, COLB)
            og_s[p, :, cols_of(cb)] = og * _sigmoid(og)

        def mid_ga(cb):
            ga_s[p, :, cols_of(cb)] = _sigmoid(proj(wg_ref, cb * COLB, COLB))

        def mid_gb(cb):
            gb_s[p, :, cols_of(cb)] = _sigmoid(proj(wg_ref, D_MODEL + cb * COLB, COLB))

        mid = []
        for cb in range(ncb):
            mid += [functools.partial(mid_u, cb), functools.partial(mid_conv, cb)]
        mid.insert(2, mid_norm_next)
        for fn in (mid_ya, mid_og, mid_ga, mid_gb):
            mid += [functools.partial(fn, cb) for cb in range(ncb)]

        def gla_diag(c, h):
            r0 = c * CHUNK
            rows = slice(r0, r0 + CHUNK)
            kcols = slice(h * DK, (h + 1) * DK)
            qc = q_s[p, rows, kcols]
            bc = b_s[p, rows, kcols]

            def row_of_each_block(ref, j):
                return jnp.concatenate(
                    [jnp.broadcast_to(ref[p, r0 + SUB * i + j:r0 + SUB * i + j + 1, kcols],
                                      (SUB, DK)) for i in range(NB)], axis=0)

            acc = jnp.zeros((CHUNK, CHUNK), F32)
            for j in range(SUB):
                kj = row_of_each_block(k_s, j)
                bj = row_of_each_block(b_s, j)
                rj = jnp.sum(qc * kj * jnp.exp2(bc - bj), axis=-1, keepdims=True)
                acc = jnp.where(diag_sel[j], rj, acc)
            vals[("diag", c, h)] = acc

        def gla_rest(c, h):
            r0 = c * CHUNK
            rows = slice(r0, r0 + CHUNK)
            kcols = slice(h * DK, (h + 1) * DK)
            vcols = slice(h * DV, (h + 1) * DV)
            qc = q_s[p, rows, kcols]
            kc = k_s[p, rows, kcols]
            bc = b_s[p, rows, kcols]
            vc = v_s[p, rows, vcols]
            st = st_s[p, h]
            bend = [b_s[p, r0 + SUB * j + SUB - 1:r0 + SUB * (j + 1), kcols] for j in range(NB)]
            bend_rows = jnp.concatenate([jnp.broadcast_to(e, (SUB, DK)) for e in bend], axis=0)
            blast = bend[NB - 1]
            qhat = (qc * jnp.exp2(bc)).astype(BF16)
            o = _dot_nt(qhat, st.astype(BF16))
            kt = (kc * jnp.exp2(bend_rows - bc)).astype(BF16)
            qs = [qc[SUB * (j + 1):] * jnp.exp2(bc[SUB * (j + 1):] - bend[j])
                  for j in range(NB - 1)]
            rm = _dot_nt(jnp.concatenate(qs, axis=0).astype(BF16), kt)
            arows = [jnp.zeros((SUB, CHUNK), F32)]
            for i in range(1, NB):
                acc = jnp.zeros((SUB, CHUNK), F32)
                for j in range(i):
                    s0 = stack_off[j] + SUB * (i - j - 1)
                    acc = jnp.where(lane_blk == j, rm[s0:s0 + SUB], acc)
                arows.append(acc)
            a = jnp.concatenate(arows, axis=0) + vals.pop(("diag", c, h))
            o_s[p, rows, vcols] = o + _dot(a.astype(BF16), vc)
            khat = (kc * jnp.exp2(blast - bc)).astype(BF16)
            st_s[p, h] = st * jnp.exp2(blast) + _dot_tn(vc, khat)

        gla = []
        for c in range(tm // CHUNK):
            for h in range(HEADS):
                gla += [functools.partial(gla_diag, c, h), functools.partial(gla_rest, c, h)]

        def tail_yb(rs):
            yb_in = (_head_norm(o_s[p, rs, :], gain_ref[...]) * og_s[p, rs, :]).astype(BF16)
            vals[("yb", rs.start)] = _dot(yb_in, wb_ref[...])

        def tail_out(rs):
            m = ga_s[p, rs, :] * ya_s[p, rs, :] + gb_s[p, rs, :] * vals.pop(("yb", rs.start))
            xo_ref[p, rs, :] = x_ref[p, rs, :] + _dot(m.astype(BF16), wo_ref[...])

        def tail(rs):
            return [functools.partial(tail_yb, rs), functools.partial(tail_out, rs)]

        return head, mid, gla, tail

    for p in range(nseq):
        head, mid, gla, tail = make_stream(p)
        for fn in head:
            fn()
        if p == 0:
            _ride_along(pl.program_id(0) * nt + t, cast_in, cast_out, turn_in, turn_out)
        _interleave(gla, mid)
        for fn in tail(slice(0, tm)):
            fn()

    @pl.when(t == nt - 1)
    def _fin():
        for p in range(nseq):
            for h in range(HEADS):
                sout_ref[p, h] = st_s[p, h].T


def _const_spec(shape):
    nd = len(shape)
    return pl.BlockSpec(shape, lambda *_: (0,) * nd, pipeline_mode=pl.Buffered(1))


def _layer_spec(entry):
    arr, idx = entry[:2]
    nd = arr.ndim
    if len(entry) == 4:
        ncols, cblk = entry[2:]
        return pl.BlockSpec((None, arr.shape[1], ncols), lambda *_: (idx, 0, cblk),
                            pipeline_mode=pl.Buffered(1))
    return pl.BlockSpec((None,) + arr.shape[1:], lambda *_: (idx,) + (0,) * (nd - 1),
                        pipeline_mode=pl.Buffered(1))


def _mixer_seq(x, cprev8, s0, W, tm, cast_jobs=(), turn_jobs=()):
    bsz, T, _ = x.shape
    assert T % tm == 0 and tm % CHUNK == 0
    ns = SEQS_PER_STEP if bsz % SEQS_PER_STEP == 0 else 1
    params = [W["w_main"], W["w_gate"], W["w_z"], W["w_a"], W["w_b"], W["w_o"], W["w_gk"],
              W["norm_mix"], W["conv_w"], W["b_gk"], W["gain"]]
    nt = T // tm
    ride_in, ride_args, ride_out, ride_shape = _ride_specs(
        (bsz // ns) * nt, lambda b, t: b * nt + t, cast_jobs, turn_jobs)
    in_specs = [
        pl.BlockSpec((ns, tm, D_MODEL), lambda b, t: (b, t, 0)),
        pl.BlockSpec((ns, tm, D_MODEL), lambda b, t: (b, jnp.minimum(t + 1, nt - 1), 0)),
        pl.BlockSpec((1, SUB, D_CONV), lambda b, t: (0, 0, 0)),
        pl.BlockSpec((1, HEADS, DK, DV), lambda b, t: (0, 0, 0, 0)),
    ] + [_layer_spec(p) for p in params] + ride_in
    out_shape = [
        jax.ShapeDtypeStruct((bsz, T, D_MODEL), F32),
        jax.ShapeDtypeStruct((bsz, SUB, D_CONV), F32),
        jax.ShapeDtypeStruct((bsz, HEADS, DK, DV), F32),
    ] + ride_shape
    out_specs = [
        pl.BlockSpec((ns, tm, D_MODEL), lambda b, t: (b, t, 0)),
        pl.BlockSpec((ns, SUB, D_CONV), lambda b, t: (b, 0, 0)),
        pl.BlockSpec((ns, HEADS, DK, DV), lambda b, t: (b, 0, 0, 0)),
    ] + ride_out
    scratch = [
        pltpu.VMEM((ns, 2, tm, D_MODEL), BF16),
        pltpu.VMEM((ns, tm + SUB, D_CONV), F32),
        pltpu.VMEM((ns, HEADS, DV, DK), F32),
        pltpu.VMEM((ns, tm, KDIM), F32),
        pltpu.VMEM((ns, tm, KDIM), F32),
        pltpu.VMEM((ns, tm, KDIM), F32),
        pltpu.VMEM((ns, tm, VDIM), BF16),
        pltpu.VMEM((ns, tm, VDIM), F32),
        pltpu.VMEM((ns, tm, D_CONV), BF16),
        pltpu.VMEM((ns, tm, D_MODEL), F32),
        pltpu.VMEM((ns, tm, VDIM), F32),
        pltpu.VMEM((ns, tm, D_MODEL), F32),
        pltpu.VMEM((ns, tm, D_MODEL), F32),
    ]
    outs = pl.pallas_call(
        functools.partial(_mixer_seq_kernel, n_cast=len(cast_jobs), n_turn=len(turn_jobs)),
        grid=(bsz // ns, nt),
        in_specs=in_specs,
        out_specs=out_specs,
        out_shape=out_shape,
        scratch_shapes=scratch,
        compiler_params=pltpu.CompilerParams(
            dimension_semantics=("arbitrary", "arbitrary"),
            vmem_limit_bytes=VMEM_LIMIT),
        name="mixer_seq",
    )(x, x, cprev8, s0, *[p[0] for p in params], *ride_args)
    return outs[0], outs[1], outs[2], outs[3:]


def _mixer_step_kernel(x_ref, cp_ref, s_ref, wm_ref, wg_ref, wz_ref,
                       wa_ref, wb_ref, wo_ref, wgk_ref,
                       nmix_ref, convw_ref, bgk_ref, gain_ref, *rest):
    xo_ref, cnew_ref, sout_ref, qt_s, kt_s, at_s, v_s, o_s, ya_s = rest[-9:]
    i = pl.program_id(0)
    n = pl.num_programs(0)
    nseq = x_ref.shape[0]

    def xn_bf16():
        return _rms(x_ref[...], nmix_ref[...]).astype(BF16)

    @pl.when(i == 0)
    def _dense_in():
        xn = xn_bf16()
        u = _dot(xn, wm_ref[:, C_GC:C_GC + D_CONV]) * _dot(xn, wm_ref[:, C_H:C_H + D_CONV])
        cw = convw_ref[...]
        p1 = cp_ref[:, D_CONV:2 * D_CONV]
        yconv = cw[0:1] * cp_ref[:, 0:D_CONV] + cw[1:2] * p1 + cw[2:3] * u
        cnew_ref[:, 0:D_CONV] = p1
        cnew_ref[:, D_CONV:2 * D_CONV] = u
        ca = (_dot(xn, wm_ref[:, C_GB:C_GB + D_CONV]) * yconv).astype(BF16)
        ya_s[...] = _dot(ca, wa_ref[...])
        v_s[...] = _dot(xn, wm_ref[:, C_V:C_V + VDIM])
        qt_s[...] = (_dot(xn, wm_ref[:, C_Q:C_Q + KDIM]) * (DK ** -0.5)).T
        kt_s[...] = _dot(xn, wm_ref[:, C_K:C_K + KDIM]).T
        z = _dot(xn, wz_ref[...]).astype(BF16)
        g = _log_sigmoid(_dot(z, wgk_ref[...]) + bgk_ref[...]) * (1.0 / GATE_NORMALIZER)
        at_s[...] = jnp.exp(g).T

    shift = (nseq - i * SUB) % nseq
    lead = slice(0, 2 * SUB)
    qt = pltpu.roll(qt_s[...], shift, 1)[:, lead].astype(BF16)
    kt = pltpu.roll(kt_s[...], shift, 1)[:, lead].astype(BF16)
    at = jnp.concatenate(_split3(pltpu.roll(at_s[...], shift, 1)[:, lead]), axis=1)
    rid = lax.broadcasted_iota(jnp.int32, (2 * SUB, SUB * LANES), 0)
    cid = lax.broadcasted_iota(jnp.int32, (2 * SUB, SUB * LANES), 1)
    sel = (rid == (cid >> 7)).astype(BF16)
    sel3 = jnp.concatenate([sel, sel, sel], axis=0)
    for h in range(HEADS):
        rk = slice(h * DK, (h + 1) * DK)
        cv = slice(h * DV, (h + 1) * DV)
        qb = _dot(qt[rk], sel)
        kb = _dot(kt[rk], sel)
        ab = _dot(at[rk], sel3)
        for j in range(SUB):
            row = pl.ds(i * SUB + j, 1)
            grp = slice(j * LANES, (j + 1) * LANES)
            both = lambda c: jnp.concatenate([c[:, grp], c[:, grp]], axis=1)
            sn = s_ref[j, h] * both(ab) + both(kb) * v_s[row, cv]
            sout_ref[j, h] = sn
            o_s[row, cv] = jnp.sum(sn * both(qb), axis=0, keepdims=True)

    @pl.when(i == n - 1)
    def _dense_out():
        xn = xn_bf16()
        og = _dot(xn, wm_ref[:, C_OG:C_OG + VDIM])
        yb_in = (_head_norm(o_s[...], gain_ref[...]) * (og * _sigmoid(og))).astype(BF16)
        yb = _dot(yb_in, wb_ref[...])
        m = (_sigmoid(_dot(xn, wg_ref[:, 0:D_MODEL])) * ya_s[...]
             + _sigmoid(_dot(xn, wg_ref[:, D_MODEL:2 * D_MODEL])) * yb)
        xo_ref[...] = x_ref[...] + _dot(m.astype(BF16), wo_ref[...])


def _mixer_step(x, conv_all, s_all, s_out_prev, W, l):
    nseq = x.shape[0]
    assert nseq == LANES
    params = [W["w_main"], W["w_gate"], W["w_z"], W["w_a"], W["w_b"],
              W["w_o"], W["w_gk"], W["norm_mix"], W["conv_w"], W["b_gk"], W["gain"]]
    state_blk = (None, SUB, HEADS, DK, DV)
    in_specs = [
        _const_spec(x.shape),
        _layer_spec((conv_all, l)),
        pl.BlockSpec(state_blk, lambda i: (l, i, 0, 0, 0)),
    ] + [_layer_spec(p) for p in params]
    args = [x, conv_all, s_all, *[p[0] for p in params]]
    aliases = {}
    if s_out_prev is not None:
        in_specs.append(pl.BlockSpec(memory_space=pl.ANY))
        aliases = {len(args): 2}
        args.append(s_out_prev)
    out_shape = (
        jax.ShapeDtypeStruct(x.shape, F32),
        jax.ShapeDtypeStruct((nseq, 2 * D_CONV), F32),
        jax.ShapeDtypeStruct(s_all.shape, F32),
    )
    out_specs = (
        pl.BlockSpec(x.shape, lambda i: (0, 0)),
        pl.BlockSpec((nseq, 2 * D_CONV), lambda i: (0, 0)),
        pl.BlockSpec(state_blk, lambda i: (l, i, 0, 0, 0)),
    )
    scratch = [
        pltpu.VMEM((KDIM, nseq), F32),
        pltpu.VMEM((KDIM, nseq), F32),
        pltpu.VMEM((KDIM, nseq), F32),
        pltpu.VMEM((nseq, VDIM), F32),
        pltpu.VMEM((nseq, VDIM), F32),
        pltpu.VMEM((nseq, D_MODEL), F32),
    ]
    return pl.pallas_call(
        _mixer_step_kernel,
        grid=(nseq // SUB,),
        in_specs=in_specs,
        out_specs=out_specs,
        out_shape=out_shape,
        scratch_shapes=scratch,
        input_output_aliases=aliases,
        compiler_params=pltpu.CompilerParams(
            dimension_semantics=("arbitrary",),
            vmem_limit_bytes=VMEM_LIMIT),
        name="mixer_step",
    )(*args)


def _ffn_kernel(*refs, final, rb, n_cast):
    x_ref, small_ref, nffn_ref, wgu_ref, wdn_ref, fin_ref = refs[:6]
    cast_in = refs[6:6 + n_cast]
    o_ref, small_out_ref = refs[6 + n_cast:8 + n_cast]
    cast_out = refs[8 + n_cast:]
    _ride_along(pl.program_id(0), cast_in, cast_out, (), ())

    def ffn_rows(x):
        xn = _rms(x, nffn_ref[...]).astype(BF16)
        g = _dot(xn, wgu_ref[:, 0:D_FF])
        u = _dot(xn, wgu_ref[:, D_FF:2 * D_FF])
        hid = (g * _sigmoid(g) * u).astype(BF16)
        y = x + _dot(hid, wdn_ref[...])
        return _rms(y, fin_ref[...]) if final else y

    y0 = ffn_rows(jnp.concatenate([x_ref[0:rb, :], small_ref[...]], axis=0))
    o_ref[0:rb, :] = y0[:rb]
    small_out_ref[...] = y0[rb:]
    for r0 in range(rb, x_ref.shape[0], rb):
        o_ref[r0:r0 + rb, :] = ffn_rows(x_ref[r0:r0 + rb, :])


def _ffn(x, small, W, final, tm, cast_jobs=()):
    n = x.shape[0]
    assert n % tm == 0
    steps = n // tm
    per_step = -(-small.shape[0] // (steps * FFN_SMALL_ROWS)) * FFN_SMALL_ROWS
    small_p = jnp.pad(small, ((0, steps * per_step - small.shape[0]), (0, 0)))
    params = [W["norm_ffn"], W["w_gu"], W["w_down"]]
    fin = W["final_norm"]
    small_spec = pl.BlockSpec((per_step, D_MODEL), lambda i: (i, 0))
    ride_in, ride_args, ride_out, ride_shape = _ride_specs(steps, lambda i: i, cast_jobs, ())
    outs = pl.pallas_call(
        functools.partial(_ffn_kernel, final=final, rb=min(tm, FFN_ROW_BLOCK),
                          n_cast=len(cast_jobs)),
        grid=(steps,),
        in_specs=[pl.BlockSpec((tm, D_MODEL), lambda i: (i, 0)), small_spec]
        + [_layer_spec(p) for p in params] + [_const_spec(fin.shape)] + ride_in,
        out_specs=[pl.BlockSpec((tm, D_MODEL), lambda i: (i, 0)), small_spec] + ride_out,
        out_shape=[jax.ShapeDtypeStruct((n, D_MODEL), F32),
                   jax.ShapeDtypeStruct(small_p.shape, F32)] + ride_shape,
        compiler_params=pltpu.CompilerParams(
            dimension_semantics=("arbitrary",),
            vmem_limit_bytes=VMEM_LIMIT),
        name="ffn",
    )(x, small_p, *[p[0] for p in params], fin, *ride_args)
    return outs[0], outs[1][:small.shape[0]], outs[2:]


def _transpose_cast_kernel(wt_ref, o_ref):
    o_ref[...] = wt_ref[0].T.astype(BF16)


def _transpose_cast(wt, layer, rb=512):
    d = wt.shape[2]
    n_main = D_MAIN // rb

    def row_of(i):
        return pl.multiple_of(jnp.where(i < n_main, i * rb, Z_HI + (i - n_main) * rb), SUB)

    return pl.pallas_call(
        _transpose_cast_kernel,
        grid=((D_MAIN + 2 * D_MODEL) // rb,),
        in_specs=[pl.BlockSpec((pl.Element(1), pl.Element(rb), pl.Element(d)),
                               lambda i: (layer, row_of(i), 0))],
        out_specs=pl.BlockSpec((None, d, rb), lambda i: (0, 0, i)),
        out_shape=jax.ShapeDtypeStruct((1, d, D_MAIN + 2 * D_MODEL), BF16),
        compiler_params=pltpu.CompilerParams(
            dimension_semantics=("arbitrary",),
            vmem_limit_bytes=VMEM_LIMIT),
        name="transpose_cast",
    )(wt)


def _layer_params(l, big, w_in, conv_w, w_gk2, b_gk2, gla_gain, norm_mix, norm_ffn, final_norm):
    depth = w_in.shape[0]
    w_z = jnp.pad(w_in[:, :, Z_LO:Z_HI], ((0, 0), (0, 0), (0, LANES - GATE_RANK))).astype(BF16)
    w_gk = jnp.pad(w_gk2, ((0, 0), (0, LANES - GATE_RANK), (0, 0))).astype(BF16)
    W = {name: (arr, 0) for name, arr in big.items() if name != "w_all"}
    W.update({
        "w_main": (big["w_all"], 0, D_MAIN, 0),
        "w_gate": (big["w_all"], 0, 2 * D_MODEL, D_MAIN // (2 * D_MODEL)),
        "w_z": (w_z, l),
        "w_gk": (w_gk, l),
        "b_gk": (b_gk2.reshape(depth, 1, KDIM), l),
        "gain": (gla_gain.reshape(depth, 1, DV), l),
        "conv_w": (conv_w, l),
        "norm_mix": (norm_mix.reshape(depth, 1, D_MODEL), l),
        "norm_ffn": (norm_ffn.reshape(depth, 1, D_MODEL), l),
    })
    W["final_norm"] = final_norm.reshape(1, D_MODEL)
    return W


def kernel(x_prompt, x_sample, state_conv, state_gla, meta_tokens, w_in, conv_w, w_gk2, b_gk2,
           gla_gain, w_a_out, w_b_out, w_o, norm_mix, norm_ffn, w_gu, w_down, final_norm):
    depth = w_in.shape[0]
    bsz, seq, _ = x_prompt.shape
    nsmp = x_sample.shape[0]
    w_in_t = jnp.swapaxes(w_in, 1, 2)
    big = {
        "w_all": _transpose_cast(w_in_t, 0),
        "w_a": w_a_out[:1].astype(BF16),
        "w_b": w_b_out[:1].astype(BF16),
        "w_o": w_o[:1].astype(BF16),
        "w_gu": w_gu[:1].astype(BF16),
        "w_down": w_down[:1].astype(BF16),
    }
    small_params = (w_in, conv_w, w_gk2, b_gk2, gla_gain, norm_mix, norm_ffn, final_norm)

    xm = jnp.pad(meta_tokens.astype(F32), ((CHUNK - N_META, 0), (0, 0)))[None]
    xp = x_prompt
    xs = x_sample.reshape(nsmp, D_MODEL)
    conv_all = state_conv.reshape(depth, nsmp, (CONV_W - 1) * D_CONV)
    zero_conv = jnp.zeros((1, SUB, D_CONV), F32)
    zero_state = jnp.zeros((1, HEADS, DK, DV), F32)

    p_conv, p_gla, s_conv = [], [], []
    s_gla = None
    for l in range(depth):
        last = l == depth - 1
        W = _layer_params(l, big, *small_params)
        mixer_casts = [] if last else [(w_gu, l + 1), (w_a_out, l + 1), (w_b_out, l + 1),
                                       (w_o, l + 1)]
        mixer_turns = [] if last else [(w_in_t, l + 1)]
        ffn_casts = [] if last else [(w_down, l + 1)]
        xm, m_conv, m_state, _ = _mixer_seq(xm, zero_conv, zero_state, W, CHUNK)
        xp, pc, ps, ride_m = _mixer_seq(xp, m_conv, m_state, W, PROMPT_TILE,
                                        mixer_casts, mixer_turns)
        xs, sc, s_gla = _mixer_step(xs, conv_all, state_gla, s_gla, W, l)
        small = xs if last else jnp.concatenate([xs, xm[0]], axis=0)
        yp, ysmall, ride_f = _ffn(xp.reshape(bsz * seq, D_MODEL), small, W, last, 1024,
                                  ffn_casts)
        xp, xs = yp.reshape(bsz, seq, D_MODEL), ysmall[:nsmp]
        if not last:
            xm = ysmall[nsmp:][None]
            big = dict(zip(("w_gu", "w_a", "w_b", "w_o", "w_all"), ride_m), w_down=ride_f[0])
        p_conv.append(pc[:, SUB - (CONV_W - 1):, :])
        p_gla.append(ps)
        s_conv.append(sc.reshape(nsmp, CONV_W - 1, D_CONV))

    return (xp, xs.reshape(nsmp, 1, D_MODEL), jnp.stack(p_conv), jnp.stack(p_gla),
            jnp.stack(s_conv), s_gla)
```

```python
import functools

import jax
import jax.numpy as jnp
from jax import lax
from jax.experimental import pallas as pl
from jax.experimental.pallas import tpu as pltpu

F32 = jnp.float32
BF16 = jnp.bfloat16

D_MODEL = 1024
N_META = 16
D_CONV = D_MODEL
CONV_W = 3
HEADS = 4
DK = 128
DV = 256
KDIM = HEADS * DK
VDIM = HEADS * DV
GATE_RANK = 16
GATE_NORMALIZER = 16.0
D_FF = 2816
EPS = 1e-6
LOG2_E = 1.4426950408889634

CHUNK = 64
SUB = 8
NB = CHUNK // SUB
LANES = 128
COLB = 512
FFN_ROW_BLOCK = 256
FFN_SMALL_ROWS = 16
PROMPT_TILE = 256
SEQS_PER_STEP = 1
VMEM_LIMIT = 60 * 1024 * 1024

C_GB, C_GC, C_H = 0, 1024, 2048
C_Q, C_K, C_V, C_OG = 3072, 3584, 4096, 5120
D_MAIN = 6144
Z_LO = D_MAIN
Z_HI = Z_LO + GATE_RANK


def _rms(x, g):
    ms = jnp.mean(x * x, axis=-1, keepdims=True)
    return x * lax.rsqrt(ms + EPS) * g


def _log_sigmoid(x):
    return jnp.minimum(x, 0.0) - jnp.log(1.0 + jnp.exp(-jnp.abs(x)))


def _sigmoid(x):
    return 1.0 / (1.0 + jnp.exp(-x))


def _dot(a, b):
    return jnp.dot(a, b, preferred_element_type=F32)


def _dot_nt(a, b):
    return lax.dot_general(a, b, (((1,), (1,)), ((), ())), preferred_element_type=F32)


def _dot_tn(a, b):
    return lax.dot_general(a, b, (((0,), (0,)), ((), ())), preferred_element_type=F32)


def _split3(x):
    x1 = x.astype(BF16)
    r1 = x - x1.astype(F32)
    x2 = r1.astype(BF16)
    r2 = r1 - x2.astype(F32)
    return x1, x2, r2.astype(BF16)


def _head_norm(o, gain):
    return jnp.concatenate(
        [_rms(o[:, h * DV:(h + 1) * DV], gain) for h in range(HEADS)], axis=1)


def _interleave(slots, pieces):
    done = 0
    for i, slot_fn in enumerate(slots):
        slot_fn()
        upto = ((i + 1) * len(pieces) + len(slots) - 1) // len(slots)
        while done < upto:
            pieces[done]()
            done += 1


def _ride_along(step, cast_in, cast_out, turn_in, turn_out):
    for src, dst in zip(cast_in, cast_out):
        dst[...] = src[...].astype(BF16)
    for src, dst in zip(turn_in, turn_out):
        dst[...] = src[0].T.astype(BF16)


def _ride_specs(steps, step_of, cast_jobs, turn_jobs):
    in_specs, args, out_specs, out_shape = [], [], [], []
    for arr, layer in cast_jobs:
        _, rows, cols = arr.shape
        assert rows % (steps * 2 * SUB) == 0
        blk = (None, rows // steps, cols)
        in_specs.append(pl.BlockSpec(blk, lambda *g, layer=layer: (layer, step_of(*g), 0)))
        out_specs.append(pl.BlockSpec(blk, lambda *g: (0, step_of(*g), 0)))
        out_shape.append(jax.ShapeDtypeStruct((1, rows, cols), BF16))
        args.append(arr)
    turn_specs, turn_shape = [], []
    for wt, layer in turn_jobs:
        d = wt.shape[2]
        width = (D_MAIN + 2 * D_MODEL) // steps
        assert width % LANES == 0 and D_MAIN % width == 0
        n_main = D_MAIN // width

        def row_of(*g, n_main=n_main, width=width):
            s = step_of(*g)
            return pl.multiple_of(jnp.where(s < n_main, s * width, Z_HI + (s - n_main) * width), SUB)

        in_specs.append(pl.BlockSpec(
            (pl.Element(1), pl.Element(width), pl.Element(d)),
            lambda *g, layer=layer, row_of=row_of: (layer, row_of(*g), 0)))
        turn_specs.append(pl.BlockSpec((None, d, width), lambda *g: (0, 0, step_of(*g))))
        turn_shape.append(jax.ShapeDtypeStruct((1, d, D_MAIN + 2 * D_MODEL), BF16))
        args.append(wt)
    return in_specs, args, out_specs + turn_specs, out_shape + turn_shape


def _mixer_seq_kernel(*refs, n_cast, n_turn):
    (x_ref, xnext_ref, cprev_ref, s0_ref, wm_ref, wg_ref, wz_ref, wa_ref, wb_ref,
     wo_ref, wgk_ref, nmix_ref, convw_ref, bgk_ref, gain_ref) = refs[:15]
    n_in = 15 + n_cast + n_turn
    cast_in, turn_in = refs[15:15 + n_cast], refs[15 + n_cast:n_in]
    xo_ref, cnew_ref, sout_ref = refs[n_in:n_in + 3]
    cast_out = refs[n_in + 3:n_in + 3 + n_cast]
    turn_out = refs[n_in + 3 + n_cast:n_in + 3 + n_cast + n_turn]
    (xn_s, ubuf, st_s, q_s, k_s, b_s, v_s, o_s, ca_s, ya_s, og_s, ga_s,
     gb_s) = refs[n_in + 3 + n_cast + n_turn:]
    t = pl.program_id(1)
    nt = pl.num_programs(1)
    nseq, tm = x_ref.shape[0], x_ref.shape[1]
    xn_buf = lax.bitwise_and(t, 1)
    xn_next_buf = lax.bitwise_and(t + 1, 1)


    @pl.when(t == 0)
    def _init():
        for p in range(nseq):
            ubuf[p, 0:SUB, :] = cprev_ref[0]
            xn_s[p, 0] = _rms(x_ref[p], nmix_ref[...]).astype(BF16)
            for h in range(HEADS):
                st_s[p, h] = s0_ref[0, h].T

    ti = lax.broadcasted_iota(jnp.int32, (CHUNK, CHUNK), 0)
    si = lax.broadcasted_iota(jnp.int32, (CHUNK, CHUNK), 1)
    diag_sel = [(si == ((ti >> 3) << 3) + j) & ((ti & 7) >= j) for j in range(SUB)]
    lane_blk = lax.broadcasted_iota(jnp.int32, (SUB, CHUNK), 1) >> 3
    stack_off = [sum(CHUNK - SUB * (i + 1) for i in range(j)) for j in range(NB)]
    ncb = D_MODEL // COLB

    def cols_of(cb):
        return slice(cb * COLB, (cb + 1) * COLB)

    def make_stream(p):
        xn = xn_s[p, xn_buf]
        vals = {}

        def proj(w_ref, c0, n):
            return _dot(xn, w_ref[:, c0:c0 + n])

        def head_gate():
            z = proj(wz_ref, 0, LANES).astype(BF16)
            g = (_log_sigmoid(_dot(z, wgk_ref[...]) + bgk_ref[...])
                 * (LOG2_E / GATE_NORMALIZER))
            row = lax.broadcasted_iota(jnp.int32, g.shape, 0) & (CHUNK - 1)
            step = 1
            while step < CHUNK:
                g = g + jnp.where(row >= step, pltpu.roll(g, step, 0), 0.0)
                step *= 2
            b_s[p] = g

        def head_q():
            q_s[p] = proj(wm_ref, C_Q, KDIM) * (DK ** -0.5)

        def head_k():
            k_s[p] = proj(wm_ref, C_K, KDIM)

        def head_v():
            v_s[p] = proj(wm_ref, C_V, VDIM).astype(BF16)

        head = [head_gate, head_q, head_k, head_v]

        def mid_u(cb):
            u = proj(wm_ref, C_GC + cb * COLB, COLB) * proj(wm_ref, C_H + cb * COLB, COLB)
            ubuf[p, SUB:SUB + tm, cols_of(cb)] = u

        def mid_conv(cb):
            cols = cols_of(cb)
            cw = convw_ref[:, cols]
            yconv = (cw[0:1] * ubuf[p, SUB - 2:SUB - 2 + tm, cols]
                     + cw[1:2] * ubuf[p, SUB - 1:SUB - 1 + tm, cols]
                     + cw[2:3] * ubuf[p, SUB:SUB + tm, cols])
            ca_s[p, :, cols] = (proj(wm_ref, C_GB + cb * COLB, COLB) * yconv).astype(BF16)
            tail = ubuf[p, tm:tm + SUB, cols]
            ubuf[p, 0:SUB, cols] = tail
            cnew_ref[p, :, cols] = tail

        def mid_ya(cb):
            ya_s[p, :, cols_of(cb)] = _dot(ca_s[p], wa_ref[:, cols_of(cb)])

        def mid_norm_next():
            xn_s[p, xn_next_buf] = _rms(xnext_ref[p], nmix_ref[...]).astype(BF16)

        def mid_og(cb):
            og = proj(wm_ref, C_OG + cb * COLB, COLB)
            og_s[p, :, cols_of(cb)] = og * _sigmoid(og)

        def mid_ga(cb):
            ga_s[p, :, cols_of(cb)] = _sigmoid(proj(wg_ref, cb * COLB, COLB))

        def mid_gb(cb):
            gb_s[p, :, cols_of(cb)] = _sigmoid(proj(wg_ref, D_MODEL + cb * COLB, COLB))

        mid = []
        for cb in range(ncb):
            mid += [functools.partial(mid_u, cb), functools.partial(mid_conv, cb)]
        mid.insert(2, mid_norm_next)
        for fn in (mid_ya, mid_og, mid_ga, mid_gb):
            mid += [functools.partial(fn, cb) for cb in range(ncb)]

        def gla_diag(c, h):
            r0 = c * CHUNK
            rows = slice(r0, r0 + CHUNK)
            kcols = slice(h * DK, (h + 1) * DK)
            qc = q_s[p, rows, kcols]
            bc = b_s[p, rows, kcols]

            def row_of_each_block(ref, j):
                return jnp.concatenate(
                    [jnp.broadcast_to(ref[p, r0 + SUB * i + j:r0 + SUB * i + j + 1, kcols],
                                      (SUB, DK)) for i in range(NB)], axis=0)

            acc = jnp.zeros((CHUNK, CHUNK), F32)
            for j in range(SUB):
                kj = row_of_each_block(k_s, j)
                bj = row_of_each_block(b_s, j)
                rj = jnp.sum(qc * kj * jnp.exp2(bc - bj), axis=-1, keepdims=True)
                acc = jnp.where(diag_sel[j], rj, acc)
            vals[("diag", c, h)] = acc

        def gla_rest(c, h):
            r0 = c * CHUNK
            rows = slice(r0, r0 + CHUNK)
            kcols = slice(h * DK, (h + 1) * DK)
            vcols = slice(h * DV, (h + 1) * DV)
            qc = q_s[p, rows, kcols]
            kc = k_s[p, rows, kcols]
            bc = b_s[p, rows, kcols]
            vc = v_s[p, rows, vcols]
            st = st_s[p, h]
            bend = [b_s[p, r0 + SUB * j + SUB - 1:r0 + SUB * (j + 1), kcols] for j in range(NB)]
            bend_rows = jnp.concatenate([jnp.broadcast_to(e, (SUB, DK)) for e in bend], axis=0)
            blast = bend[NB - 1]
            qhat = (qc * jnp.exp2(bc)).astype(BF16)
            o = _dot_nt(qhat, st.astype(BF16))
            kt = (kc * jnp.exp2(bend_rows - bc)).astype(BF16)
            qs = [qc[SUB * (j + 1):] * jnp.exp2(bc[SUB * (j + 1):] - bend[j])
                  for j in range(NB - 1)]
            rm = _dot_nt(jnp.concatenate(qs, axis=0).astype(BF16), kt)
            arows = [jnp.zeros((SUB, CHUNK), F32)]
            for i in range(1, NB):
                acc = jnp.zeros((SUB, CHUNK), F32)
                for j in range(i):
                    s0 = stack_off[j] + SUB * (i - j - 1)
                    acc = jnp.where(lane_blk == j, rm[s0:s0 + SUB], acc)
                arows.append(acc)
            a = jnp.concatenate(arows, axis=0) + vals.pop(("diag", c, h))
            o_s[p, rows, vcols] = o + _dot(a.astype(BF16), vc)
            khat = (kc * jnp.exp2(blast - bc)).astype(BF16)
            st_s[p, h] = st * jnp.exp2(blast) + _dot_tn(vc, khat)

        gla = []
        for c in range(tm // CHUNK):
            for h in range(HEADS):
                gla += [functools.partial(gla_diag, c, h), functools.partial(gla_rest, c, h)]

        def tail_yb(rs):
            yb_in = (_head_norm(o_s[p, rs, :], gain_ref[...]) * og_s[p, rs, :]).astype(BF16)
            vals[("yb", rs.start)] = _dot(yb_in, wb_ref[...])

        def tail_out(rs):
            m = ga_s[p, rs, :] * ya_s[p, rs, :] + gb_s[p, rs, :] * vals.pop(("yb", rs.start))
            xo_ref[p, rs, :] = x_ref[p, rs, :] + _dot(m.astype(BF16), wo_ref[...])

        def tail(rs):
            return [functools.partial(tail_yb, rs), functools.partial(tail_out, rs)]

        return head, mid, gla, tail

    for p in range(nseq):
        head, mid, gla, tail = make_stream(p)
        for fn in head:
            fn()
        if p == 0:
            _ride_along(pl.program_id(0) * nt + t, cast_in, cast_out, turn_in, turn_out)
        _interleave(gla, mid)
        for fn in tail(slice(0, tm)):
            fn()

    @pl.when(t == nt - 1)
    def _fin():
        for p in range(nseq):
            for h in range(HEADS):
                sout_ref[p, h] = st_s[p, h].T


def _const_spec(shape):
    nd = len(shape)
    return pl.BlockSpec(shape, lambda *_: (0,) * nd, pipeline_mode=pl.Buffered(1))


def _layer_spec(entry):
    arr, idx = entry[:2]
    nd = arr.ndim
    if len(entry) == 4:
        ncols, cblk = entry[2:]
        return pl.BlockSpec((None, arr.shape[1], ncols), lambda *_: (idx, 0, cblk),
                            pipeline_mode=pl.Buffered(1))
    return pl.BlockSpec((None,) + arr.shape[1:], lambda *_: (idx,) + (0,) * (nd - 1),
                        pipeline_mode=pl.Buffered(1))


def _mixer_seq(x, cprev8, s0, W, tm, cast_jobs=(), turn_jobs=()):
    bsz, T, _ = x.shape
    assert T % tm == 0 and tm % CHUNK == 0
    ns = SEQS_PER_STEP if bsz % SEQS_PER_STEP == 0 else 1
    params = [W["w_main"], W["w_gate"], W["w_z"], W["w_a"], W["w_b"], W["w_o"], W["w_gk"],
              W["norm_mix"], W["conv_w"], W["b_gk"], W["gain"]]
    nt = T // tm
    ride_in, ride_args, ride_out, ride_shape = _ride_specs(
        (bsz // ns) * nt, lambda b, t: b * nt + t, cast_jobs, turn_jobs)
    in_specs = [
        pl.BlockSpec((ns, tm, D_MODEL), lambda b, t: (b, t, 0)),
        pl.BlockSpec((ns, tm, D_MODEL), lambda b, t: (b, jnp.minimum(t + 1, nt - 1), 0)),
        pl.BlockSpec((1, SUB, D_CONV), lambda b, t: (0, 0, 0)),
        pl.BlockSpec((1, HEADS, DK, DV), lambda b, t: (0, 0, 0, 0)),
    ] + [_layer_spec(p) for p in params] + ride_in
    out_shape = [
        jax.ShapeDtypeStruct((bsz, T, D_MODEL), F32),
        jax.ShapeDtypeStruct((bsz, SUB, D_CONV), F32),
        jax.ShapeDtypeStruct((bsz, HEADS, DK, DV), F32),
    ] + ride_shape
    out_specs = [
        pl.BlockSpec((ns, tm, D_MODEL), lambda b, t: (b, t, 0)),
        pl.BlockSpec((ns, SUB, D_CONV), lambda b, t: (b, 0, 0)),
        pl.BlockSpec((ns, HEADS, DK, DV), lambda b, t: (b, 0, 0, 0)),
    ] + ride_out
    scratch = [
        pltpu.VMEM((ns, 2, tm, D_MODEL), BF16),
        pltpu.VMEM((ns, tm + SUB, D_CONV), F32),
        pltpu.VMEM((ns, HEADS, DV, DK), F32),
        pltpu.VMEM((ns, tm, KDIM), F32),
        pltpu.VMEM((ns, tm, KDIM), F32),
        pltpu.VMEM((ns, tm, KDIM), F32),
        pltpu.VMEM((ns, tm, VDIM), BF16),
        pltpu.VMEM((ns, tm, VDIM), F32),
        pltpu.VMEM((ns, tm, D_CONV), BF16),
        pltpu.VMEM((ns, tm, D_MODEL), F32),
        pltpu.VMEM((ns, tm, VDIM), F32),
        pltpu.VMEM((ns, tm, D_MODEL), F32),
        pltpu.VMEM((ns, tm, D_MODEL), F32),
    ]
    outs = pl.pallas_call(
        functools.partial(_mixer_seq_kernel, n_cast=len(cast_jobs), n_turn=len(turn_jobs)),
        grid=(bsz // ns, nt),
        in_specs=in_specs,
        out_specs=out_specs,
        out_shape=out_shape,
        scratch_shapes=scratch,
        compiler_params=pltpu.CompilerParams(
            dimension_semantics=("arbitrary", "arbitrary"),
            vmem_limit_bytes=VMEM_LIMIT),
        name="mixer_seq",
    )(x, x, cprev8, s0, *[p[0] for p in params], *ride_args)
    return outs[0], outs[1], outs[2], outs[3:]


def _mixer_step_kernel(x_ref, cp_ref, s_ref, wm_ref, wg_ref, wz_ref,
                       wa_ref, wb_ref, wo_ref, wgk_ref,
                       nmix_ref, convw_ref, bgk_ref, gain_ref, *rest):
    xo_ref, cnew_ref, sout_ref, qt_s, kt_s, at_s, v_s, o_s, ya_s = rest[-9:]
    i = pl.program_id(0)
    n = pl.num_programs(0)
    nseq = x_ref.shape[0]

    def xn_bf16():
        return _rms(x_ref[...], nmix_ref[...]).astype(BF16)

    @pl.when(i == 0)
    def _dense_in():
        xn = xn_bf16()
        u = _dot(xn, wm_ref[:, C_GC:C_GC + D_CONV]) * _dot(xn, wm_ref[:, C_H:C_H + D_CONV])
        cw = convw_ref[...]
        p1 = cp_ref[:, D_CONV:2 * D_CONV]
        yconv = cw[0:1] * cp_ref[:, 0:D_CONV] + cw[1:2] * p1 + cw[2:3] * u
        cnew_ref[:, 0:D_CONV] = p1
        cnew_ref[:, D_CONV:2 * D_CONV] = u
        ca = (_dot(xn, wm_ref[:, C_GB:C_GB + D_CONV]) * yconv).astype(BF16)
        ya_s[...] = _dot(ca, wa_ref[...])
        v_s[...] = _dot(xn, wm_ref[:, C_V:C_V + VDIM])
        qt_s[...] = (_dot(xn, wm_ref[:, C_Q:C_Q + KDIM]) * (DK ** -0.5)).T
        kt_s[...] = _dot(xn, wm_ref[:, C_K:C_K + KDIM]).T
        z = _dot(xn, wz_ref[...]).astype(BF16)
        g = _log_sigmoid(_dot(z, wgk_ref[...]) + bgk_ref[...]) * (1.0 / GATE_NORMALIZER)
        at_s[...] = jnp.exp(g).T

    shift = (nseq - i * SUB) % nseq
    lead = slice(0, 2 * SUB)
    qt = pltpu.roll(qt_s[...], shift, 1)[:, lead].astype(BF16)
    kt = pltpu.roll(kt_s[...], shift, 1)[:, lead].astype(BF16)
    at = jnp.concatenate(_split3(pltpu.roll(at_s[...], shift, 1)[:, lead]), axis=1)
    rid = lax.broadcasted_iota(jnp.int32, (2 * SUB, SUB * LANES), 0)
    cid = lax.broadcasted_iota(jnp.int32, (2 * SUB, SUB * LANES), 1)
    sel = (rid == (cid >> 7)).astype(BF16)
    sel3 = jnp.concatenate([sel, sel, sel], axis=0)
    for h in range(HEADS):
        rk = slice(h * DK, (h + 1) * DK)
        cv = slice(h * DV, (h + 1) * DV)
        qb = _dot(qt[rk], sel)
        kb = _dot(kt[rk], sel)
        ab = _dot(at[rk], sel3)
        for j in range(SUB):
            row = pl.ds(i * SUB + j, 1)
            grp = slice(j * LANES, (j + 1) * LANES)
            both = lambda c: jnp.concatenate([c[:, grp], c[:, grp]], axis=1)
            sn = s_ref[j, h] * both(ab) + both(kb) * v_s[row, cv]
            sout_ref[j, h] = sn
            o_s[row, cv] = jnp.sum(sn * both(qb), axis=0, keepdims=True)

    @pl.when(i == n - 1)
    def _dense_out():
        xn = xn_bf16()
        og = _dot(xn, wm_ref[:, C_OG:C_OG + VDIM])
        yb_in = (_head_norm(o_s[...], gain_ref[...]) * (og * _sigmoid(og))).astype(BF16)
        yb = _dot(yb_in, wb_ref[...])
        m = (_sigmoid(_dot(xn, wg_ref[:, 0:D_MODEL])) * ya_s[...]
             + _sigmoid(_dot(xn, wg_ref[:, D_MODEL:2 * D_MODEL])) * yb)
        xo_ref[...] = x_ref[...] + _dot(m.astype(BF16), wo_ref[...])


def _mixer_step(x, conv_all, s_all, s_out_prev, W, l):
    nseq = x.shape[0]
    assert nseq == LANES
    params = [W["w_main"], W["w_gate"], W["w_z"], W["w_a"], W["w_b"],
              W["w_o"], W["w_gk"], W["norm_mix"], W["conv_w"], W["b_gk"], W["gain"]]
    state_blk = (None, SUB, HEADS, DK, DV)
    in_specs = [
        _const_spec(x.shape),
        _layer_spec((conv_all, l)),
        pl.BlockSpec(state_blk, lambda i: (l, i, 0, 0, 0)),
    ] + [_layer_spec(p) for p in params]
    args = [x, conv_all, s_all, *[p[0] for p in params]]
    aliases = {}
    if s_out_prev is not None:
        in_specs.append(pl.BlockSpec(memory_space=pl.ANY))
        aliases = {len(args): 2}
        args.append(s_out_prev)
    out_shape = (
        jax.ShapeDtypeStruct(x.shape, F32),
        jax.ShapeDtypeStruct((nseq, 2 * D_CONV), F32),
        jax.ShapeDtypeStruct(s_all.shape, F32),
    )
    out_specs = (
        pl.BlockSpec(x.shape, lambda i: (0, 0)),
        pl.BlockSpec((nseq, 2 * D_CONV), lambda i: (0, 0)),
        pl.BlockSpec(state_blk, lambda i: (l, i, 0, 0, 0)),
    )
    scratch = [
        pltpu.VMEM((KDIM, nseq), F32),
        pltpu.VMEM((KDIM, nseq), F32),
        pltpu.VMEM((KDIM, nseq), F32),
        pltpu.VMEM((nseq, VDIM), F32),
        pltpu.VMEM((nseq, VDIM), F32),
        pltpu.VMEM((nseq, D_MODEL), F32),
    ]
    return pl.pallas_call(
        _mixer_step_kernel,
        grid=(nseq // SUB,),
        in_specs=in_specs,
        out_specs=out_specs,
        out_shape=out_shape,
        scratch_shapes=scratch,
        input_output_aliases=aliases,
        compiler_params=pltpu.CompilerParams(
            dimension_semantics=("arbitrary",),
            vmem_limit_bytes=VMEM_LIMIT),
        name="mixer_step",
    )(*args)


def _ffn_kernel(*refs, final, rb, n_cast):
    x_ref, small_ref, nffn_ref, wgu_ref, wdn_ref, fin_ref = refs[:6]
    cast_in = refs[6:6 + n_cast]
    o_ref, small_out_ref = refs[6 + n_cast:8 + n_cast]
    cast_out = refs[8 + n_cast:]
    _ride_along(pl.program_id(0), cast_in, cast_out, (), ())

    def ffn_rows(x):
        xn = _rms(x, nffn_ref[...]).astype(BF16)
        g = _dot(xn, wgu_ref[:, 0:D_FF])
        u = _dot(xn, wgu_ref[:, D_FF:2 * D_FF])
        hid = (g * _sigmoid(g) * u).astype(BF16)
        y = x + _dot(hid, wdn_ref[...])
        return _rms(y, fin_ref[...]) if final else y

    y0 = ffn_rows(jnp.concatenate([x_ref[0:rb, :], small_ref[...]], axis=0))
    o_ref[0:rb, :] = y0[:rb]
    small_out_ref[...] = y0[rb:]
    for r0 in range(rb, x_ref.shape[0], rb):
        o_ref[r0:r0 + rb, :] = ffn_rows(x_ref[r0:r0 + rb, :])


def _ffn(x, small, W, final, tm, cast_jobs=()):
    n = x.shape[0]
    assert n % tm == 0
    steps = n // tm
    per_step = -(-small.shape[0] // (steps * FFN_SMALL_ROWS)) * FFN_SMALL_ROWS
    small_p = jnp.pad(small, ((0, steps * per_step - small.shape[0]), (0, 0)))
    params = [W["norm_ffn"], W["w_gu"], W["w_down"]]
    fin = W["final_norm"]
    small_spec = pl.BlockSpec((per_step, D_MODEL), lambda i: (i, 0))
    ride_in, ride_args, ride_out, ride_shape = _ride_specs(steps, lambda i: i, cast_jobs, ())
    outs = pl.pallas_call(
        functools.partial(_ffn_kernel, final=final, rb=min(tm, FFN_ROW_BLOCK),
                          n_cast=len(cast_jobs)),
        grid=(steps,),
        in_specs=[pl.BlockSpec((tm, D_MODEL), lambda i: (i, 0)), small_spec]
        + [_layer_spec(p) for p in params] + [_const_spec(fin.shape)] + ride_in,
        out_specs=[pl.BlockSpec((tm, D_MODEL), lambda i: (i, 0)), small_spec] + ride_out,
        out_shape=[jax.ShapeDtypeStruct((n, D_MODEL), F32),
                   jax.ShapeDtypeStruct(small_p.shape, F32)] + ride_shape,
        compiler_params=pltpu.CompilerParams(
            dimension_semantics=("arbitrary",),
            vmem_limit_bytes=VMEM_LIMIT),
        name="ffn",
    )(x, small_p, *[p[0] for p in params], fin, *ride_args)
    return outs[0], outs[1][:small.shape[0]], outs[2:]


def _transpose_cast_kernel(wt_ref, o_ref):
    o_ref[...] = wt_ref[0].T.astype(BF16)


def _transpose_cast(wt, layer, rb=512):
    d = wt.shape[2]
    n_main = D_MAIN // rb

    def row_of(i):
        return pl.multiple_of(jnp.where(i < n_main, i * rb, Z_HI + (i - n_main) * rb), SUB)

    return pl.pallas_call(
        _transpose_cast_kernel,
        grid=((D_MAIN + 2 * D_MODEL) // rb,),
        in_specs=[pl.BlockSpec((pl.Element(1), pl.Element(rb), pl.Element(d)),
                               lambda i: (layer, row_of(i), 0))],
        out_specs=pl.BlockSpec((None, d, rb), lambda i: (0, 0, i)),
        out_shape=jax.ShapeDtypeStruct((1, d, D_MAIN + 2 * D_MODEL), BF16),
        compiler_params=pltpu.CompilerParams(
            dimension_semantics=("arbitrary",),
            vmem_limit_bytes=VMEM_LIMIT),
        name="transpose_cast",
    )(wt)


def _layer_params(l, big, w_in, conv_w, w_gk2, b_gk2, gla_gain, norm_mix, norm_ffn, final_norm):
    depth = w_in.shape[0]
    w_z = jnp.pad(w_in[:, :, Z_LO:Z_HI], ((0, 0), (0, 0), (0, LANES - GATE_RANK))).astype(BF16)
    w_gk = jnp.pad(w_gk2, ((0, 0), (0, LANES - GATE_RANK), (0, 0))).astype(BF16)
    W = {name: (arr, 0) for name, arr in big.items() if name != "w_all"}
    W.update({
        "w_main": (big["w_all"], 0, D_MAIN, 0),
        "w_gate": (big["w_all"], 0, 2 * D_MODEL, D_MAIN // (2 * D_MODEL)),
        "w_z": (w_z, l),
        "w_gk": (w_gk, l),
        "b_gk": (b_gk2.reshape(depth, 1, KDIM), l),
        "gain": (gla_gain.reshape(depth, 1, DV), l),
        "conv_w": (conv_w, l),
        "norm_mix": (norm_mix.reshape(depth, 1, D_MODEL), l),
        "norm_ffn": (norm_ffn.reshape(depth, 1, D_MODEL), l),
    })
    W["final_norm"] = final_norm.reshape(1, D_MODEL)
    return W


def kernel(x_prompt, x_sample, state_conv, state_gla, meta_tokens, w_in, conv_w, w_gk2, b_gk2,
           gla_gain, w_a_out, w_b_out, w_o, norm_mix, norm_ffn, w_gu, w_down, final_norm):
    depth = w_in.shape[0]
    bsz, seq, _ = x_prompt.shape
    nsmp = x_sample.shape[0]
    w_in_t = jnp.swapaxes(w_in, 1, 2)
    big = {
        "w_all": _transpose_cast(w_in_t, 0),
        "w_a": w_a_out[:1].astype(BF16),
        "w_b": w_b_out[:1].astype(BF16),
        "w_o": w_o[:1].astype(BF16),
        "w_down": w_down[:1].astype(BF16),
    }
    small_params = (w_in, conv_w, w_gk2, b_gk2, gla_gain, norm_mix, norm_ffn, final_norm)

    xm = jnp.pad(meta_tokens.astype(F32), ((CHUNK - N_META, 0), (0, 0)))[None]
    xp = x_prompt
    xs = x_sample.reshape(nsmp, D_MODEL)
    conv_all = state_conv.reshape(depth, nsmp, (CONV_W - 1) * D_CONV)
    zero_conv = jnp.zeros((1, SUB, D_CONV), F32)
    zero_state = jnp.zeros((1, HEADS, DK, DV), F32)

    p_conv, p_gla, s_conv = [], [], []
    s_gla = None
    for l in range(depth):
        last = l == depth - 1
        W = _layer_params(l, big, *small_params)
        own_casts = [] if "w_gu" in big else [(w_gu, l)]
        mixer_casts = own_casts + ([] if last else [(w_gu, l + 1), (w_a_out, l + 1),
                                                    (w_b_out, l + 1), (w_o, l + 1)])
        mixer_turns = [] if last else [(w_in_t, l + 1)]
        ffn_casts = [] if last else [(w_down, l + 1)]
        xm, m_conv, m_state, _ = _mixer_seq(xm, zero_conv, zero_state, W, CHUNK)
        xp, pc, ps, ride_m = _mixer_seq(xp, m_conv, m_state, W, PROMPT_TILE,
                                        mixer_casts, mixer_turns)
        if own_casts:
            W["w_gu"], ride_m = (ride_m[0], 0), ride_m[1:]
        xs, sc, s_gla = _mixer_step(xs, conv_all, state_gla, s_gla, W, l)
        small = xs if last else jnp.concatenate([xs, xm[0]], axis=0)
        yp, ysmall, ride_f = _ffn(xp.reshape(bsz * seq, D_MODEL), small, W, last, 1024,
                                  ffn_casts)
        xp, xs = yp.reshape(bsz, seq, D_MODEL), ysmall[:nsmp]
        if not last:
            xm = ysmall[nsmp:][None]
            big = dict(zip(("w_gu", "w_a", "w_b", "w_o", "w_all"), ride_m), w_down=ride_f[0])
        p_conv.append(pc[:, SUB - (CONV_W - 1):, :])
        p_gla.append(ps)
        s_conv.append(sc.reshape(nsmp, CONV_W - 1, D_CONV))

    return (xp, xs.reshape(nsmp, 1, D_MODEL), jnp.stack(p_conv), jnp.stack(p_gla),
            jnp.stack(s_conv), s_gla)
```

```python
import functools

import jax
import jax.numpy as jnp
from jax import lax
from jax.experimental import pallas as pl
from jax.experimental.pallas import tpu as pltpu

F32 = jnp.float32
BF16 = jnp.bfloat16

D_MODEL = 1024
N_META = 16
D_CONV = D_MODEL
CONV_W = 3
HEADS = 4
DK = 128
DV = 256
KDIM = HEADS * DK
VDIM = HEADS * DV
GATE_RANK = 16
GATE_NORMALIZER = 16.0
D_FF = 2816
EPS = 1e-6
LOG2_E = 1.4426950408889634

CHUNK = 64
SUB = 8
NB = CHUNK // SUB
LANES = 128
COLB = 512
FFN_TILE = 1024
FFN_ROW_BLOCK = 256
FFN_SMALL_ROWS = 16
PROMPT_TILE = 256
SEQS_PER_STEP = 1
VMEM_LIMIT = 60 * 1024 * 1024

C_GB, C_GC, C_H = 0, 1024, 2048
C_Q, C_K, C_V, C_OG = 3072, 3584, 4096, 5120
D_MAIN = 6144
Z_LO = D_MAIN
Z_HI = Z_LO + GATE_RANK


def _rms(x, g):
    ms = jnp.mean(x * x, axis=-1, keepdims=True)
    return x * lax.rsqrt(ms + EPS) * g


def _log_sigmoid(x):
    return jnp.minimum(x, 0.0) - jnp.log(1.0 + jnp.exp(-jnp.abs(x)))


def _sigmoid(x):
    return 1.0 / (1.0 + jnp.exp(-x))


def _dot(a, b):
    return jnp.dot(a, b, preferred_element_type=F32)


def _dot_nt(a, b):
    return lax.dot_general(a, b, (((1,), (1,)), ((), ())), preferred_element_type=F32)


def _dot_tn(a, b):
    return lax.dot_general(a, b, (((0,), (0,)), ((), ())), preferred_element_type=F32)


def _split3(x):
    x1 = x.astype(BF16)
    r1 = x - x1.astype(F32)
    x2 = r1.astype(BF16)
    r2 = r1 - x2.astype(F32)
    return x1, x2, r2.astype(BF16)


def _head_norm(o, gain):
    return jnp.concatenate(
        [_rms(o[:, h * DV:(h + 1) * DV], gain) for h in range(HEADS)], axis=1)


def _interleave(slots, pieces):
    done = 0
    for i, slot_fn in enumerate(slots):
        slot_fn()
        upto = ((i + 1) * len(pieces) + len(slots) - 1) // len(slots)
        while done < upto:
            pieces[done]()
            done += 1


def _ride_along(cast_in, cast_out, turn_in, turn_out):
    for src, dst in zip(cast_in, cast_out):
        dst[...] = src[...].astype(BF16)
    for src, dst in zip(turn_in, turn_out):
        dst[...] = src[0].T.astype(BF16)


def _ride_specs(steps, step_of, cast_jobs, turn_jobs):
    in_specs, args, out_specs, out_shape = [], [], [], []
    for arr, layer in cast_jobs:
        _, rows, cols = arr.shape
        assert rows % (steps * 2 * SUB) == 0
        blk = (None, rows // steps, cols)
        in_specs.append(pl.BlockSpec(blk, lambda *g, layer=layer: (layer, step_of(*g), 0)))
        out_specs.append(pl.BlockSpec(blk, lambda *g: (0, step_of(*g), 0)))
        out_shape.append(jax.ShapeDtypeStruct((1, rows, cols), BF16))
        args.append(arr)
    turn_specs, turn_shape = [], []
    for wt, layer in turn_jobs:
        d = wt.shape[2]
        width = (D_MAIN + 2 * D_MODEL) // steps
        assert width % LANES == 0 and D_MAIN % width == 0
        n_main = D_MAIN // width

        def row_of(*g, n_main=n_main, width=width):
            s = step_of(*g)
            return pl.multiple_of(jnp.where(s < n_main, s * width, Z_HI + (s - n_main) * width), SUB)

        in_specs.append(pl.BlockSpec(
            (pl.Element(1), pl.Element(width), pl.Element(d)),
            lambda *g, layer=layer, row_of=row_of: (layer, row_of(*g), 0)))
        turn_specs.append(pl.BlockSpec((None, d, width), lambda *g: (0, 0, step_of(*g))))
        turn_shape.append(jax.ShapeDtypeStruct((1, d, D_MAIN + 2 * D_MODEL), BF16))
        args.append(wt)
    return in_specs, args, out_specs + turn_specs, out_shape + turn_shape


def _mixer_seq_kernel(*refs, n_cast, n_turn):
    (x_ref, xnext_ref, cprev_ref, s0_ref, wm_ref, wg_ref, wz_ref, wa_ref, wb_ref,
     wo_ref, wgk_ref, nmix_ref, convw_ref, bgk_ref, gain_ref) = refs[:15]
    n_in = 15 + n_cast + n_turn
    cast_in, turn_in = refs[15:15 + n_cast], refs[15 + n_cast:n_in]
    xo_ref, cnew_ref, sout_ref = refs[n_in:n_in + 3]
    cast_out = refs[n_in + 3:n_in + 3 + n_cast]
    turn_out = refs[n_in + 3 + n_cast:n_in + 3 + n_cast + n_turn]
    (xn_s, ubuf, st_s, q_s, k_s, b_s, v_s, o_s, ca_s, ya_s, og_s, ga_s,
     gb_s) = refs[n_in + 3 + n_cast + n_turn:]
    t = pl.program_id(1)
    nt = pl.num_programs(1)
    nseq, tm = x_ref.shape[0], x_ref.shape[1]
    xn_buf = lax.bitwise_and(t, 1)
    xn_next_buf = lax.bitwise_and(t + 1, 1)

    @pl.when(t == 0)
    def _init():
        for p in range(nseq):
            ubuf[p, 0:SUB, :] = cprev_ref[0]
            xn_s[p, 0] = _rms(x_ref[p], nmix_ref[...]).astype(BF16)
            for h in range(HEADS):
                st_s[p, h] = s0_ref[0, h].T

    ti = lax.broadcasted_iota(jnp.int32, (CHUNK, CHUNK), 0)
    si = lax.broadcasted_iota(jnp.int32, (CHUNK, CHUNK), 1)
    diag_sel = [(si == ((ti >> 3) << 3) + j) & ((ti & 7) >= j) for j in range(SUB)]
    lane_blk = lax.broadcasted_iota(jnp.int32, (SUB, CHUNK), 1) >> 3
    stack_off = [sum(CHUNK - SUB * (i + 1) for i in range(j)) for j in range(NB)]
    ncb = D_MODEL // COLB

    def cols_of(cb):
        return slice(cb * COLB, (cb + 1) * COLB)

    def make_stream(p):
        xn = xn_s[p, xn_buf]
        vals = {}

        def proj(w_ref, c0, n):
            return _dot(xn, w_ref[:, c0:c0 + n])

        def head_gate():
            z = proj(wz_ref, 0, LANES).astype(BF16)
            g = (_log_sigmoid(_dot(z, wgk_ref[...]) + bgk_ref[...])
                 * (LOG2_E / GATE_NORMALIZER))
            row = lax.broadcasted_iota(jnp.int32, g.shape, 0) & (CHUNK - 1)
            step = 1
            while step < CHUNK:
                g = g + jnp.where(row >= step, pltpu.roll(g, step, 0), 0.0)
                step *= 2
            b_s[p] = g

        def head_q():
            q_s[p] = proj(wm_ref, C_Q, KDIM) * (DK ** -0.5)

        def head_k():
            k_s[p] = proj(wm_ref, C_K, KDIM)

        def head_v():
            v_s[p] = proj(wm_ref, C_V, VDIM).astype(BF16)

        head = [head_gate, head_q, head_k, head_v]

        def mid_u(cb):
            u = proj(wm_ref, C_GC + cb * COLB, COLB) * proj(wm_ref, C_H + cb * COLB, COLB)
            ubuf[p, SUB:SUB + tm, cols_of(cb)] = u

        def mid_conv(cb):
            cols = cols_of(cb)
            cw = convw_ref[:, cols]
            yconv = (cw[0:1] * ubuf[p, SUB - 2:SUB - 2 + tm, cols]
                     + cw[1:2] * ubuf[p, SUB - 1:SUB - 1 + tm, cols]
                     + cw[2:3] * ubuf[p, SUB:SUB + tm, cols])
            ca_s[p, :, cols] = (proj(wm_ref, C_GB + cb * COLB, COLB) * yconv).astype(BF16)
            tail = ubuf[p, tm:tm + SUB, cols]
            ubuf[p, 0:SUB, cols] = tail
            cnew_ref[p, :, cols] = tail

        def mid_ya(cb):
            ya_s[p, :, cols_of(cb)] = _dot(ca_s[p], wa_ref[:, cols_of(cb)])

        def mid_norm_next():
            xn_s[p, xn_next_buf] = _rms(xnext_ref[p], nmix_ref[...]).astype(BF16)

        def mid_og(cb):
            og = proj(wm_ref, C_OG + cb * COLB, COLB)
            og_s[p, :, cols_of(cb)] = og * _sigmoid(og)

        def mid_ga(cb):
            ga_s[p, :, cols_of(cb)] = _sigmoid(proj(wg_ref, cb * COLB, COLB))

        def mid_gb(cb):
            gb_s[p, :, cols_of(cb)] = _sigmoid(proj(wg_ref, D_MODEL + cb * COLB, COLB))

        mid = []
        for cb in range(ncb):
            mid += [functools.partial(mid_u, cb), functools.partial(mid_conv, cb)]
        mid.insert(2, mid_norm_next)
        for fn in (mid_ya, mid_og, mid_ga, mid_gb):
            mid += [functools.partial(fn, cb) for cb in range(ncb)]

        def gla_diag(c, h):
            r0 = c * CHUNK
            rows = slice(r0, r0 + CHUNK)
            kcols = slice(h * DK, (h + 1) * DK)
            qc = q_s[p, rows, kcols]
            bc = b_s[p, rows, kcols]

            def row_of_each_block(ref, j):
                return jnp.concatenate(
                    [jnp.broadcast_to(ref[p, r0 + SUB * i + j:r0 + SUB * i + j + 1, kcols],
                                      (SUB, DK)) for i in range(NB)], axis=0)

            acc = jnp.zeros((CHUNK, CHUNK), F32)
            for j in range(SUB):
                kj = row_of_each_block(k_s, j)
                bj = row_of_each_block(b_s, j)
                rj = jnp.sum(qc * kj * jnp.exp2(bc - bj), axis=-1, keepdims=True)
                acc = jnp.where(diag_sel[j], rj, acc)
            vals[("diag", c, h)] = acc

        def gla_rest(c, h):
            r0 = c * CHUNK
            rows = slice(r0, r0 + CHUNK)
            kcols = slice(h * DK, (h + 1) * DK)
            vcols = slice(h * DV, (h + 1) * DV)
            qc = q_s[p, rows, kcols]
            kc = k_s[p, rows, kcols]
            bc = b_s[p, rows, kcols]
            vc = v_s[p, rows, vcols]
            st = st_s[p, h]
            bend = [b_s[p, r0 + SUB * j + SUB - 1:r0 + SUB * (j + 1), kcols] for j in range(NB)]
            bend_rows = jnp.concatenate([jnp.broadcast_to(e, (SUB, DK)) for e in bend], axis=0)
            blast = bend[NB - 1]
            qhat = (qc * jnp.exp2(bc)).astype(BF16)
            o = _dot_nt(qhat, st.astype(BF16))
            kt = (kc * jnp.exp2(bend_rows - bc)).astype(BF16)
            qs = [qc[SUB * (j + 1):] * jnp.exp2(bc[SUB * (j + 1):] - bend[j])
                  for j in range(NB - 1)]
            rm = _dot_nt(jnp.concatenate(qs, axis=0).astype(BF16), kt)
            arows = [jnp.zeros((SUB, CHUNK), F32)]
            for i in range(1, NB):
                acc = jnp.zeros((SUB, CHUNK), F32)
                for j in range(i):
                    s0 = stack_off[j] + SUB * (i - j - 1)
                    acc = jnp.where(lane_blk == j, rm[s0:s0 + SUB], acc)
                arows.append(acc)
            a = jnp.concatenate(arows, axis=0) + vals.pop(("diag", c, h))
            o_s[p, rows, vcols] = o + _dot(a.astype(BF16), vc)
            khat = (kc * jnp.exp2(blast - bc)).astype(BF16)
            st_s[p, h] = st * jnp.exp2(blast) + _dot_tn(vc, khat)

        gla = []
        for c in range(tm // CHUNK):
            for h in range(HEADS):
                gla += [functools.partial(gla_diag, c, h), functools.partial(gla_rest, c, h)]

        def tail_yb(rs):
            yb_in = (_head_norm(o_s[p, rs, :], gain_ref[...]) * og_s[p, rs, :]).astype(BF16)
            vals[("yb", rs.start)] = _dot(yb_in, wb_ref[...])

        def tail_out(rs):
            m = ga_s[p, rs, :] * ya_s[p, rs, :] + gb_s[p, rs, :] * vals.pop(("yb", rs.start))
            xo_ref[p, rs, :] = x_ref[p, rs, :] + _dot(m.astype(BF16), wo_ref[...])

        def tail(rs):
            return [functools.partial(tail_yb, rs), functools.partial(tail_out, rs)]

        return head, mid, gla, tail

    for p in range(nseq):
        head, mid, gla, tail = make_stream(p)
        for fn in head:
            fn()
        if p == 0:
            _ride_along(cast_in, cast_out, turn_in, turn_out)
        _interleave(gla, mid)
        for fn in tail(slice(0, tm)):
            fn()

    @pl.when(t == nt - 1)
    def _fin():
        for p in range(nseq):
            for h in range(HEADS):
                sout_ref[p, h] = st_s[p, h].T


def _const_spec(shape):
    nd = len(shape)
    return pl.BlockSpec(shape, lambda *_: (0,) * nd, pipeline_mode=pl.Buffered(1))


def _layer_spec(entry):
    arr, idx = entry[:2]
    nd = arr.ndim
    if len(entry) == 4:
        ncols, cblk = entry[2:]
        return pl.BlockSpec((None, arr.shape[1], ncols), lambda *_: (idx, 0, cblk),
                            pipeline_mode=pl.Buffered(1))
    return pl.BlockSpec((None,) + arr.shape[1:], lambda *_: (idx,) + (0,) * (nd - 1),
                        pipeline_mode=pl.Buffered(1))


def _mixer_seq(x, cprev8, s0, W, tm, cast_jobs=(), turn_jobs=()):
    bsz, T, _ = x.shape
    assert T % tm == 0 and tm % CHUNK == 0
    ns = SEQS_PER_STEP if bsz % SEQS_PER_STEP == 0 else 1
    params = [W["w_main"], W["w_gate"], W["w_z"], W["w_a"], W["w_b"], W["w_o"], W["w_gk"],
              W["norm_mix"], W["conv_w"], W["b_gk"], W["gain"]]
    nt = T // tm
    ride_in, ride_args, ride_out, ride_shape = _ride_specs(
        (bsz // ns) * nt, lambda b, t: b * nt + t, cast_jobs, turn_jobs)
    in_specs = [
        pl.BlockSpec((ns, tm, D_MODEL), lambda b, t: (b, t, 0)),
        pl.BlockSpec((ns, tm, D_MODEL), lambda b, t: (b, jnp.minimum(t + 1, nt - 1), 0)),
        pl.BlockSpec((1, SUB, D_CONV), lambda b, t: (0, 0, 0)),
        pl.BlockSpec((1, HEADS, DK, DV), lambda b, t: (0, 0, 0, 0)),
    ] + [_layer_spec(p) for p in params] + ride_in
    out_shape = [
        jax.ShapeDtypeStruct((bsz, T, D_MODEL), F32),
        jax.ShapeDtypeStruct((bsz, SUB, D_CONV), F32),
        jax.ShapeDtypeStruct((bsz, HEADS, DK, DV), F32),
    ] + ride_shape
    out_specs = [
        pl.BlockSpec((ns, tm, D_MODEL), lambda b, t: (b, t, 0)),
        pl.BlockSpec((ns, SUB, D_CONV), lambda b, t: (b, 0, 0)),
        pl.BlockSpec((ns, HEADS, DK, DV), lambda b, t: (b, 0, 0, 0)),
    ] + ride_out
    scratch = [
        pltpu.VMEM((ns, 2, tm, D_MODEL), BF16),
        pltpu.VMEM((ns, tm + SUB, D_CONV), F32),
        pltpu.VMEM((ns, HEADS, DV, DK), F32),
        pltpu.VMEM((ns, tm, KDIM), F32),
        pltpu.VMEM((ns, tm, KDIM), F32),
        pltpu.VMEM((ns, tm, KDIM), F32),
        pltpu.VMEM((ns, tm, VDIM), BF16),
        pltpu.VMEM((ns, tm, VDIM), F32),
        pltpu.VMEM((ns, tm, D_CONV), BF16),
        pltpu.VMEM((ns, tm, D_MODEL), F32),
        pltpu.VMEM((ns, tm, VDIM), F32),
        pltpu.VMEM((ns, tm, D_MODEL), F32),
        pltpu.VMEM((ns, tm, D_MODEL), F32),
    ]
    outs = pl.pallas_call(
        functools.partial(_mixer_seq_kernel, n_cast=len(cast_jobs), n_turn=len(turn_jobs)),
        grid=(bsz // ns, nt),
        in_specs=in_specs,
        out_specs=out_specs,
        out_shape=out_shape,
        scratch_shapes=scratch,
        compiler_params=pltpu.CompilerParams(
            dimension_semantics=("arbitrary", "arbitrary"),
            vmem_limit_bytes=VMEM_LIMIT),
        name="mixer_seq",
    )(x, x, cprev8, s0, *[p[0] for p in params], *ride_args)
    return outs[0], outs[1], outs[2], outs[3:]


def _mixer_step_kernel(x_ref, cp_ref, s_ref, wm_ref, wg_ref, wz_ref,
                       wa_ref, wb_ref, wo_ref, wgk_ref,
                       nmix_ref, convw_ref, bgk_ref, gain_ref, *rest):
    xo_ref, cnew_ref, sout_ref, qt_s, kt_s, at_s, v_s, o_s, ya_s = rest[-9:]
    i = pl.program_id(0)
    n = pl.num_programs(0)
    nseq = x_ref.shape[0]

    def xn_bf16():
        return _rms(x_ref[...], nmix_ref[...]).astype(BF16)

    @pl.when(i == 0)
    def _dense_in():
        xn = xn_bf16()
        u = _dot(xn, wm_ref[:, C_GC:C_GC + D_CONV]) * _dot(xn, wm_ref[:, C_H:C_H + D_CONV])
        cw = convw_ref[...]
        p1 = cp_ref[:, D_CONV:2 * D_CONV]
        yconv = cw[0:1] * cp_ref[:, 0:D_CONV] + cw[1:2] * p1 + cw[2:3] * u
        cnew_ref[:, 0:D_CONV] = p1
        cnew_ref[:, D_CONV:2 * D_CONV] = u
        ca = (_dot(xn, wm_ref[:, C_GB:C_GB + D_CONV]) * yconv).astype(BF16)
        ya_s[...] = _dot(ca, wa_ref[...])
        v_s[...] = _dot(xn, wm_ref[:, C_V:C_V + VDIM])
        qt_s[...] = (_dot(xn, wm_ref[:, C_Q:C_Q + KDIM]) * (DK ** -0.5)).T
        kt_s[...] = _dot(xn, wm_ref[:, C_K:C_K + KDIM]).T
        z = _dot(xn, wz_ref[...]).astype(BF16)
        g = _log_sigmoid(_dot(z, wgk_ref[...]) + bgk_ref[...]) * (1.0 / GATE_NORMALIZER)
        at_s[...] = jnp.exp(g).T

    shift = (nseq - i * SUB) % nseq
    lead = slice(0, 2 * SUB)
    qt = pltpu.roll(qt_s[...], shift, 1)[:, lead].astype(BF16)
    kt = pltpu.roll(kt_s[...], shift, 1)[:, lead].astype(BF16)
    at = jnp.concatenate(_split3(pltpu.roll(at_s[...], shift, 1)[:, lead]), axis=1)
    rid = lax.broadcasted_iota(jnp.int32, (2 * SUB, SUB * LANES), 0)
    cid = lax.broadcasted_iota(jnp.int32, (2 * SUB, SUB * LANES), 1)
    sel = (rid == (cid >> 7)).astype(BF16)
    sel3 = jnp.concatenate([sel, sel, sel], axis=0)
    for h in range(HEADS):
        rk = slice(h * DK, (h + 1) * DK)
        cv = slice(h * DV, (h + 1) * DV)
        qb = _dot(qt[rk], sel)
        kb = _dot(kt[rk], sel)
        ab = _dot(at[rk], sel3)
        for j in range(SUB):
            row = pl.ds(i * SUB + j, 1)
            grp = slice(j * LANES, (j + 1) * LANES)
            both = lambda c: jnp.concatenate([c[:, grp], c[:, grp]], axis=1)
            sn = s_ref[j, h] * both(ab) + both(kb) * v_s[row, cv]
            sout_ref[j, h] = sn
            o_s[row, cv] = jnp.sum(sn * both(qb), axis=0, keepdims=True)

    @pl.when(i == n - 1)
    def _dense_out():
        xn = xn_bf16()
        og = _dot(xn, wm_ref[:, C_OG:C_OG + VDIM])
        yb_in = (_head_norm(o_s[...], gain_ref[...]) * (og * _sigmoid(og))).astype(BF16)
        yb = _dot(yb_in, wb_ref[...])
        m = (_sigmoid(_dot(xn, wg_ref[:, 0:D_MODEL])) * ya_s[...]
             + _sigmoid(_dot(xn, wg_ref[:, D_MODEL:2 * D_MODEL])) * yb)
        xo_ref[...] = x_ref[...] + _dot(m.astype(BF16), wo_ref[...])


def _mixer_step(x, conv_all, s_all, s_out_prev, W, l):
    nseq = x.shape[0]
    assert nseq == LANES
    params = [W["w_main"], W["w_gate"], W["w_z"], W["w_a"], W["w_b"],
              W["w_o"], W["w_gk"], W["norm_mix"], W["conv_w"], W["b_gk"], W["gain"]]
    state_blk = (None, SUB, HEADS, DK, DV)
    in_specs = [
        _const_spec(x.shape),
        _layer_spec((conv_all, l)),
        pl.BlockSpec(state_blk, lambda i: (l, i, 0, 0, 0)),
    ] + [_layer_spec(p) for p in params]
    args = [x, conv_all, s_all, *[p[0] for p in params]]
    aliases = {}
    if s_out_prev is not None:
        in_specs.append(pl.BlockSpec(memory_space=pl.ANY))
        aliases = {len(args): 2}
        args.append(s_out_prev)
    out_shape = (
        jax.ShapeDtypeStruct(x.shape, F32),
        jax.ShapeDtypeStruct((nseq, 2 * D_CONV), F32),
        jax.ShapeDtypeStruct(s_all.shape, F32),
    )
    out_specs = (
        pl.BlockSpec(x.shape, lambda i: (0, 0)),
        pl.BlockSpec((nseq, 2 * D_CONV), lambda i: (0, 0)),
        pl.BlockSpec(state_blk, lambda i: (l, i, 0, 0, 0)),
    )
    scratch = [
        pltpu.VMEM((KDIM, nseq), F32),
        pltpu.VMEM((KDIM, nseq), F32),
        pltpu.VMEM((KDIM, nseq), F32),
        pltpu.VMEM((nseq, VDIM), F32),
        pltpu.VMEM((nseq, VDIM), F32),
        pltpu.VMEM((nseq, D_MODEL), F32),
    ]
    return pl.pallas_call(
        _mixer_step_kernel,
        grid=(nseq // SUB,),
        in_specs=in_specs,
        out_specs=out_specs,
        out_shape=out_shape,
        scratch_shapes=scratch,
        input_output_aliases=aliases,
        compiler_params=pltpu.CompilerParams(
            dimension_semantics=("arbitrary",),
            vmem_limit_bytes=VMEM_LIMIT),
        name="mixer_step",
    )(*args)


def _ffn_kernel(*refs, final, rb, n_cast):
    x_ref, small_ref, nffn_ref, wgu_ref, wdn_ref, fin_ref = refs[:6]
    cast_in = refs[6:6 + n_cast]
    o_ref, small_out_ref = refs[6 + n_cast:8 + n_cast]
    cast_out = refs[8 + n_cast:]
    _ride_along(cast_in, cast_out, (), ())

    def ffn_rows(x):
        xn = _rms(x, nffn_ref[...]).astype(BF16)
        g = _dot(xn, wgu_ref[:, 0:D_FF])
        u = _dot(xn, wgu_ref[:, D_FF:2 * D_FF])
        hid = (g * _sigmoid(g) * u).astype(BF16)
        y = x + _dot(hid, wdn_ref[...])
        return _rms(y, fin_ref[...]) if final else y

    y0 = ffn_rows(jnp.concatenate([x_ref[0:rb, :], small_ref[...]], axis=0))
    o_ref[0:rb, :] = y0[:rb]
    small_out_ref[...] = y0[rb:]
    for r0 in range(rb, x_ref.shape[0], rb):
        o_ref[r0:r0 + rb, :] = ffn_rows(x_ref[r0:r0 + rb, :])


def _ffn(x, small, W, final, tm, cast_jobs=()):
    n = x.shape[0]
    assert n % tm == 0
    steps = n // tm
    per_step = -(-small.shape[0] // (steps * FFN_SMALL_ROWS)) * FFN_SMALL_ROWS
    small_p = jnp.pad(small, ((0, steps * per_step - small.shape[0]), (0, 0)))
    params = [W["norm_ffn"], W["w_gu"], W["w_down"]]
    fin = W["final_norm"]
    small_spec = pl.BlockSpec((per_step, D_MODEL), lambda i: (i, 0))
    ride_in, ride_args, ride_out, ride_shape = _ride_specs(steps, lambda i: i, cast_jobs, ())
    outs = pl.pallas_call(
        functools.partial(_ffn_kernel, final=final, rb=min(tm, FFN_ROW_BLOCK),
                          n_cast=len(cast_jobs)),
        grid=(steps,),
        in_specs=[pl.BlockSpec((tm, D_MODEL), lambda i: (i, 0)), small_spec]
        + [_layer_spec(p) for p in params] + [_const_spec(fin.shape)] + ride_in,
        out_specs=[pl.BlockSpec((tm, D_MODEL), lambda i: (i, 0)), small_spec] + ride_out,
        out_shape=[jax.ShapeDtypeStruct((n, D_MODEL), F32),
                   jax.ShapeDtypeStruct(small_p.shape, F32)] + ride_shape,
        compiler_params=pltpu.CompilerParams(
            dimension_semantics=("arbitrary",),
            vmem_limit_bytes=VMEM_LIMIT),
        name="ffn",
    )(x, small_p, *[p[0] for p in params], fin, *ride_args)
    return outs[0], outs[1][:small.shape[0]], outs[2:]


def _transpose_cast_kernel(wt_ref, o_ref):
    o_ref[...] = wt_ref[0].T.astype(BF16)


def _transpose_cast(wt, layer, rb=512):
    d = wt.shape[2]
    n_main = D_MAIN // rb

    def row_of(i):
        return pl.multiple_of(jnp.where(i < n_main, i * rb, Z_HI + (i - n_main) * rb), SUB)

    return pl.pallas_call(
        _transpose_cast_kernel,
        grid=((D_MAIN + 2 * D_MODEL) // rb,),
        in_specs=[pl.BlockSpec((pl.Element(1), pl.Element(rb), pl.Element(d)),
                               lambda i: (layer, row_of(i), 0))],
        out_specs=pl.BlockSpec((None, d, rb), lambda i: (0, 0, i)),
        out_shape=jax.ShapeDtypeStruct((1, d, D_MAIN + 2 * D_MODEL), BF16),
        compiler_params=pltpu.CompilerParams(
            dimension_semantics=("arbitrary",),
            vmem_limit_bytes=VMEM_LIMIT),
        name="transpose_cast",
    )(wt)


def _layer_params(l, big, w_in, conv_w, w_gk2, b_gk2, gla_gain, norm_mix, norm_ffn, final_norm):
    depth = w_in.shape[0]
    w_z = jnp.pad(w_in[:, :, Z_LO:Z_HI], ((0, 0), (0, 0), (0, LANES - GATE_RANK))).astype(BF16)
    w_gk = jnp.pad(w_gk2, ((0, 0), (0, LANES - GATE_RANK), (0, 0))).astype(BF16)
    W = {name: (arr, 0) for name, arr in big.items() if name != "w_all"}
    W.update({
        "w_main": (big["w_all"], 0, D_MAIN, 0),
        "w_gate": (big["w_all"], 0, 2 * D_MODEL, D_MAIN // (2 * D_MODEL)),
        "w_z": (w_z, l),
        "w_gk": (w_gk, l),
        "b_gk": (b_gk2.reshape(depth, 1, KDIM), l),
        "gain": (gla_gain.reshape(depth, 1, DV), l),
        "conv_w": (conv_w, l),
        "norm_mix": (norm_mix.reshape(depth, 1, D_MODEL), l),
        "norm_ffn": (norm_ffn.reshape(depth, 1, D_MODEL), l),
    })
    W["final_norm"] = final_norm.reshape(1, D_MODEL)
    return W


def kernel(x_prompt, x_sample, state_conv, state_gla, meta_tokens, w_in, conv_w, w_gk2, b_gk2,
           gla_gain, w_a_out, w_b_out, w_o, norm_mix, norm_ffn, w_gu, w_down, final_norm):
    depth = w_in.shape[0]
    bsz, seq, _ = x_prompt.shape
    nsmp = x_sample.shape[0]
    w_in_t = jnp.swapaxes(w_in, 1, 2)
    big = {
        "w_all": _transpose_cast(w_in_t, 0),
        "w_a": w_a_out[:1].astype(BF16),
        "w_b": w_b_out[:1].astype(BF16),
        "w_o": w_o[:1].astype(BF16),
        "w_down": w_down[:1].astype(BF16),
    }
    small_params = (w_in, conv_w, w_gk2, b_gk2, gla_gain, norm_mix, norm_ffn, final_norm)

    xm = jnp.pad(meta_tokens.astype(F32), ((CHUNK - N_META, 0), (0, 0)))[None]
    xp = x_prompt
    xs = x_sample.reshape(nsmp, D_MODEL)
    conv_all = state_conv.reshape(depth, nsmp, (CONV_W - 1) * D_CONV)
    zero_conv = jnp.zeros((1, SUB, D_CONV), F32)
    zero_state = jnp.zeros((1, HEADS, DK, DV), F32)

    p_conv, p_gla, s_conv = [], [], []
    s_gla = None
    for l in range(depth):
        last = l == depth - 1
        W = _layer_params(l, big, *small_params)
        own_casts = [] if "w_gu" in big else [(w_gu, l)]
        mixer_casts = own_casts + ([] if last else [(w_gu, l + 1), (w_a_out, l + 1),
                                                    (w_b_out, l + 1), (w_o, l + 1)])
        mixer_turns = [] if last else [(w_in_t, l + 1)]
        ffn_casts = [] if last else [(w_down, l + 1)]
        xm, m_conv, m_state, _ = _mixer_seq(xm, zero_conv, zero_state, W, CHUNK)
        xp, pc, ps, ride_m = _mixer_seq(xp, m_conv, m_state, W, PROMPT_TILE,
                                        mixer_casts, mixer_turns)
        if own_casts:
            W["w_gu"], ride_m = (ride_m[0], 0), ride_m[1:]
        xs, sc, s_gla = _mixer_step(xs, conv_all, state_gla, s_gla, W, l)
        small = xs if last else jnp.concatenate([xs, xm[0]], axis=0)
        yp, ysmall, ride_f = _ffn(xp.reshape(bsz * seq, D_MODEL), small, W, last, FFN_TILE,
                                  ffn_casts)
        xp, xs = yp.reshape(bsz, seq, D_MODEL), ysmall[:nsmp]
        if not last:
            xm = ysmall[nsmp:][None]
            big = dict(zip(("w_gu", "w_a", "w_b", "w_o", "w_all"), ride_m), w_down=ride_f[0])
        p_conv.append(pc[:, SUB - (CONV_W - 1):, :])
        p_gla.append(ps)
        s_conv.append(sc.reshape(nsmp, CONV_W - 1, D_CONV))

    return (xp, xs.reshape(nsmp, 1, D_MODEL), jnp.stack(p_conv), jnp.stack(p_gla),
            jnp.stack(s_conv), s_gla)
```

```python
import functools

import jax
import jax.numpy as jnp
from jax import lax
from jax.experimental import pallas as pl
from jax.experimental.pallas import tpu as pltpu

F32 = jnp.float32
BF16 = jnp.bfloat16

D_MODEL = 1024
N_META = 16
D_CONV = D_MODEL
CONV_W = 3
HEADS = 4
DK = 128
DV = 256
KDIM = HEADS * DK
VDIM = HEADS * DV
GATE_RANK = 16
GATE_NORMALIZER = 16.0
D_FF = 2816
EPS = 1e-6
LOG2_E = 1.4426950408889634

CHUNK = 64
SUB = 8
NB = CHUNK // SUB
LANES = 128
COLB = 512
FFN_ROW_BLOCK = 256
FFN_SMALL_ROWS = 16
PROMPT_TILE = 256
SEQS_PER_STEP = 1
VMEM_LIMIT = 60 * 1024 * 1024

C_GB, C_GC, C_H = 0, 1024, 2048
C_Q, C_K, C_V, C_OG = 3072, 3584, 4096, 5120
D_MAIN = 6144
Z_LO = D_MAIN
Z_HI = Z_LO + GATE_RANK


def _rms(x, g):
    ms = jnp.mean(x * x, axis=-1, keepdims=True)
    return x * lax.rsqrt(ms + EPS) * g


def _log_sigmoid(x):
    return jnp.minimum(x, 0.0) - jnp.log(1.0 + jnp.exp(-jnp.abs(x)))


def _sigmoid(x):
    return 1.0 / (1.0 + jnp.exp(-x))


def _dot(a, b):
    return jnp.dot(a, b, preferred_element_type=F32)


def _dot_nt(a, b):
    return lax.dot_general(a, b, (((1,), (1,)), ((), ())), preferred_element_type=F32)


def _dot_tn(a, b):
    return lax.dot_general(a, b, (((0,), (0,)), ((), ())), preferred_element_type=F32)


def _split3(x):
    x1 = x.astype(BF16)
    r1 = x - x1.astype(F32)
    x2 = r1.astype(BF16)
    r2 = r1 - x2.astype(F32)
    return x1, x2, r2.astype(BF16)


def _head_norm(o, gain):
    return jnp.concatenate(
        [_rms(o[:, h * DV:(h + 1) * DV], gain) for h in range(HEADS)], axis=1)


def _interleave(slots, pieces):
    done = 0
    for i, slot_fn in enumerate(slots):
        slot_fn()
        upto = ((i + 1) * len(pieces) + len(slots) - 1) // len(slots)
        while done < upto:
            pieces[done]()
            done += 1


def _ride_along(step, cast_in, cast_out, turn_in, turn_out):
    for src, dst in zip(cast_in, cast_out):
        dst[...] = src[...].astype(BF16)
    for src, dst in zip(turn_in, turn_out):
        dst[...] = src[0].T.astype(BF16)


def _ride_specs(steps, step_of, cast_jobs, turn_jobs):
    in_specs, args, out_specs, out_shape = [], [], [], []
    for arr, layer in cast_jobs:
        _, rows, cols = arr.shape
        assert rows % (steps * 2 * SUB) == 0
        blk = (None, rows // steps, cols)
        in_specs.append(pl.BlockSpec(blk, lambda *g, layer=layer: (layer, step_of(*g), 0)))
        out_specs.append(pl.BlockSpec(blk, lambda *g: (0, step_of(*g), 0)))
        out_shape.append(jax.ShapeDtypeStruct((1, rows, cols), BF16))
        args.append(arr)
    turn_specs, turn_shape = [], []
    for wt, layer in turn_jobs:
        d = wt.shape[2]
        width = (D_MAIN + 2 * D_MODEL) // steps
        assert width % LANES == 0 and D_MAIN % width == 0
        n_main = D_MAIN // width

        def row_of(*g, n_main=n_main, width=width):
            s = step_of(*g)
            return pl.multiple_of(jnp.where(s < n_main, s * width, Z_HI + (s - n_main) * width), SUB)

        in_specs.append(pl.BlockSpec(
            (pl.Element(1), pl.Element(width), pl.Element(d)),
            lambda *g, layer=layer, row_of=row_of: (layer, row_of(*g), 0)))
        turn_specs.append(pl.BlockSpec((None, d, width), lambda *g: (0, 0, step_of(*g))))
        turn_shape.append(jax.ShapeDtypeStruct((1, d, D_MAIN + 2 * D_MODEL), BF16))
        args.append(wt)
    return in_specs, args, out_specs + turn_specs, out_shape + turn_shape


def _mixer_seq_kernel(*refs, n_cast, n_turn):
    (x_ref, xnext_ref, cprev_ref, s0_ref, wm_ref, wg_ref, wz_ref, wa_ref, wb_ref,
     wo_ref, wgk_ref, nmix_ref, convw_ref, bgk_ref, gain_ref) = refs[:15]
    n_in = 15 + n_cast + n_turn
    cast_in, turn_in = refs[15:15 + n_cast], refs[15 + n_cast:n_in]
    xo_ref, cnew_ref, sout_ref = refs[n_in:n_in + 3]
    cast_out = refs[n_in + 3:n_in + 3 + n_cast]
    turn_out = refs[n_in + 3 + n_cast:n_in + 3 + n_cast + n_turn]
    (xn_s, ubuf, st_s, q_s, k_s, b_s, v_s, o_s, ca_s, ya_s, og_s, ga_s,
     gb_s) = refs[n_in + 3 + n_cast + n_turn:]
    t = pl.program_id(1)
    nt = pl.num_programs(1)
    nseq, tm = x_ref.shape[0], x_ref.shape[1]
    xn_buf = lax.bitwise_and(t, 1)
    xn_next_buf = lax.bitwise_and(t + 1, 1)


    @pl.when(t == 0)
    def _init():
        for p in range(nseq):
            ubuf[p, 0:SUB, :] = cprev_ref[0]
            xn_s[p, 0] = _rms(x_ref[p], nmix_ref[...]).astype(BF16)
            for h in range(HEADS):
                st_s[p, h] = s0_ref[0, h].T

    ti = lax.broadcasted_iota(jnp.int32, (CHUNK, CHUNK), 0)
    si = lax.broadcasted_iota(jnp.int32, (CHUNK, CHUNK), 1)
    diag_sel = [(si == ((ti >> 3) << 3) + j) & ((ti & 7) >= j) for j in range(SUB)]
    lane_blk = lax.broadcasted_iota(jnp.int32, (SUB, CHUNK), 1) >> 3
    stack_off = [sum(CHUNK - SUB * (i + 1) for i in range(j)) for j in range(NB)]
    ncb = D_MODEL // COLB

    def cols_of(cb):
        return slice(cb * COLB, (cb + 1) * COLB)

    def make_stream(p):
        xn = xn_s[p, xn_buf]
        vals = {}

        def proj(w_ref, c0, n):
            return _dot(xn, w_ref[:, c0:c0 + n])

        def head_gate():
            z = proj(wz_ref, 0, LANES).astype(BF16)
            g = (_log_sigmoid(_dot(z, wgk_ref[...]) + bgk_ref[...])
                 * (LOG2_E / GATE_NORMALIZER))
            row = lax.broadcasted_iota(jnp.int32, g.shape, 0) & (CHUNK - 1)
            step = 1
            while step < CHUNK:
                g = g + jnp.where(row >= step, pltpu.roll(g, step, 0), 0.0)
                step *= 2
            b_s[p] = g

        def head_q():
            q_s[p] = proj(wm_ref, C_Q, KDIM) * (DK ** -0.5)

        def head_k():
            k_s[p] = proj(wm_ref, C_K, KDIM)

        def head_v():
            v_s[p] = proj(wm_ref, C_V, VDIM).astype(BF16)

        head = [head_gate, head_q, head_k, head_v]

        def mid_u():
            r = proj(wm_ref, C_GC, 2 * D_CONV)
            ubuf[p, SUB:SUB + tm, :] = r[:, :D_CONV] * r[:, D_CONV:]

        def mid_conv(cb):
            cols = cols_of(cb)
            cw = convw_ref[:, cols]
            yconv = (cw[0:1] * ubuf[p, SUB - 2:SUB - 2 + tm, cols]
                     + cw[1:2] * ubuf[p, SUB - 1:SUB - 1 + tm, cols]
                     + cw[2:3] * ubuf[p, SUB:SUB + tm, cols])
            ca_s[p, :, cols] = (proj(wm_ref, C_GB + cb * COLB, COLB) * yconv).astype(BF16)
            tail = ubuf[p, tm:tm + SUB, cols]
            ubuf[p, 0:SUB, cols] = tail
            cnew_ref[p, :, cols] = tail

        def mid_ya(cb):
            ya_s[p, :, cols_of(cb)] = _dot(ca_s[p], wa_ref[:, cols_of(cb)])

        def mid_norm_next():
            xn_s[p, xn_next_buf] = _rms(xnext_ref[p], nmix_ref[...]).astype(BF16)

        def mid_og(cb):
            og = proj(wm_ref, C_OG + cb * COLB, COLB)
            og_s[p, :, cols_of(cb)] = og * _sigmoid(og)

        def mid_ga(cb):
            ga_s[p, :, cols_of(cb)] = _sigmoid(proj(wg_ref, cb * COLB, COLB))

        def mid_gb(cb):
            gb_s[p, :, cols_of(cb)] = _sigmoid(proj(wg_ref, D_MODEL + cb * COLB, COLB))

        mid = [mid_norm_next, mid_u]
        mid += [functools.partial(mid_conv, cb) for cb in range(ncb)]
        for fn in (mid_ya, mid_og, mid_ga, mid_gb):
            mid += [functools.partial(fn, cb) for cb in range(ncb)]

        def gla_diag(c, h):
            r0 = c * CHUNK
            rows = slice(r0, r0 + CHUNK)
            kcols = slice(h * DK, (h + 1) * DK)
            qc = q_s[p, rows, kcols]
            bc = b_s[p, rows, kcols]

            def row_of_each_block(ref, j):
                return jnp.concatenate(
                    [jnp.broadcast_to(ref[p, r0 + SUB * i + j:r0 + SUB * i + j + 1, kcols],
                                      (SUB, DK)) for i in range(NB)], axis=0)

            acc = jnp.zeros((CHUNK, CHUNK), F32)
            for j in range(SUB):
                kj = row_of_each_block(k_s, j)
                bj = row_of_each_block(b_s, j)
                rj = jnp.sum(qc * kj * jnp.exp2(bc - bj), axis=-1, keepdims=True)
                acc = jnp.where(diag_sel[j], rj, acc)
            vals[("diag", c, h)] = acc

        def gla_rest(c, h):
            r0 = c * CHUNK
            rows = slice(r0, r0 + CHUNK)
            kcols = slice(h * DK, (h + 1) * DK)
            vcols = slice(h * DV, (h + 1) * DV)
            qc = q_s[p, rows, kcols]
            kc = k_s[p, rows, kcols]
            bc = b_s[p, rows, kcols]
            vc = v_s[p, rows, vcols]
            st = st_s[p, h]
            bend = [b_s[p, r0 + SUB * j + SUB - 1:r0 + SUB * (j + 1), kcols] for j in range(NB)]
            bend_rows = jnp.concatenate([jnp.broadcast_to(e, (SUB, DK)) for e in bend], axis=0)
            blast = bend[NB - 1]
            qhat = (qc * jnp.exp2(bc)).astype(BF16)
            o = _dot_nt(qhat, st.astype(BF16))
            kt = (kc * jnp.exp2(bend_rows - bc)).astype(BF16)
            qs = [qc[SUB * (j + 1):] * jnp.exp2(bc[SUB * (j + 1):] - bend[j])
                  for j in range(NB - 1)]
            rm = _dot_nt(jnp.concatenate(qs, axis=0).astype(BF16), kt)
            arows = [jnp.zeros((SUB, CHUNK), F32)]
            for i in range(1, NB):
                acc = jnp.zeros((SUB, CHUNK), F32)
                for j in range(i):
                    s0 = stack_off[j] + SUB * (i - j - 1)
                    acc = jnp.where(lane_blk == j, rm[s0:s0 + SUB], acc)
                arows.append(acc)
            a = jnp.concatenate(arows, axis=0) + vals.pop(("diag", c, h))
            o_s[p, rows, vcols] = o + _dot(a.astype(BF16), vc)
            khat = (kc * jnp.exp2(blast - bc)).astype(BF16)
            st_s[p, h] = st * jnp.exp2(blast) + _dot_tn(vc, khat)

        gla = []
        for c in range(tm // CHUNK):
            for h in range(HEADS):
                gla += [functools.partial(gla_diag, c, h), functools.partial(gla_rest, c, h)]

        def tail_yb(rs):
            yb_in = (_head_norm(o_s[p, rs, :], gain_ref[...]) * og_s[p, rs, :]).astype(BF16)
            vals[("yb", rs.start)] = _dot(yb_in, wb_ref[...])

        def tail_out(rs):
            m = ga_s[p, rs, :] * ya_s[p, rs, :] + gb_s[p, rs, :] * vals.pop(("yb", rs.start))
            xo_ref[p, rs, :] = x_ref[p, rs, :] + _dot(m.astype(BF16), wo_ref[...])

        def tail(rs):
            return [functools.partial(tail_yb, rs), functools.partial(tail_out, rs)]

        return head, mid, gla, tail

    for p in range(nseq):
        head, mid, gla, tail = make_stream(p)
        for fn in head:
            fn()
        if p == 0:
            _ride_along(pl.program_id(0) * nt + t, cast_in, cast_out, turn_in, turn_out)
        _interleave(gla, mid)
        for fn in tail(slice(0, tm)):
            fn()

    @pl.when(t == nt - 1)
    def _fin():
        for p in range(nseq):
            for h in range(HEADS):
                sout_ref[p, h] = st_s[p, h].T


def _const_spec(shape):
    nd = len(shape)
    return pl.BlockSpec(shape, lambda *_: (0,) * nd, pipeline_mode=pl.Buffered(1))


def _layer_spec(entry):
    arr, idx = entry[:2]
    nd = arr.ndim
    if len(entry) == 4:
        ncols, cblk = entry[2:]
        return pl.BlockSpec((None, arr.shape[1], ncols), lambda *_: (idx, 0, cblk),
                            pipeline_mode=pl.Buffered(1))
    return pl.BlockSpec((None,) + arr.shape[1:], lambda *_: (idx,) + (0,) * (nd - 1),
                        pipeline_mode=pl.Buffered(1))


def _mixer_seq(x, cprev8, s0, W, tm, cast_jobs=(), turn_jobs=()):
    bsz, T, _ = x.shape
    assert T % tm == 0 and tm % CHUNK == 0
    ns = SEQS_PER_STEP if bsz % SEQS_PER_STEP == 0 else 1
    params = [W["w_main"], W["w_gate"], W["w_z"], W["w_a"], W["w_b"], W["w_o"], W["w_gk"],
              W["norm_mix"], W["conv_w"], W["b_gk"], W["gain"]]
    nt = T // tm
    ride_in, ride_args, ride_out, ride_shape = _ride_specs(
        (bsz // ns) * nt, lambda b, t: b * nt + t, cast_jobs, turn_jobs)
    in_specs = [
        pl.BlockSpec((ns, tm, D_MODEL), lambda b, t: (b, t, 0)),
        pl.BlockSpec((ns, tm, D_MODEL), lambda b, t: (b, jnp.minimum(t + 1, nt - 1), 0)),
        pl.BlockSpec((1, SUB, D_CONV), lambda b, t: (0, 0, 0)),
        pl.BlockSpec((1, HEADS, DK, DV), lambda b, t: (0, 0, 0, 0)),
    ] + [_layer_spec(p) for p in params] + ride_in
    out_shape = [
        jax.ShapeDtypeStruct((bsz, T, D_MODEL), F32),
        jax.ShapeDtypeStruct((bsz, SUB, D_CONV), F32),
        jax.ShapeDtypeStruct((bsz, HEADS, DK, DV), F32),
    ] + ride_shape
    out_specs = [
        pl.BlockSpec((ns, tm, D_MODEL), lambda b, t: (b, t, 0)),
        pl.BlockSpec((ns, SUB, D_CONV), lambda b, t: (b, 0, 0)),
        pl.BlockSpec((ns, HEADS, DK, DV), lambda b, t: (b, 0, 0, 0)),
    ] + ride_out
    scratch = [
        pltpu.VMEM((ns, 2, tm, D_MODEL), BF16),
        pltpu.VMEM((ns, tm + SUB, D_CONV), F32),
        pltpu.VMEM((ns, HEADS, DV, DK), F32),
        pltpu.VMEM((ns, tm, KDIM), F32),
        pltpu.VMEM((ns, tm, KDIM), F32),
        pltpu.VMEM((ns, tm, KDIM), F32),
        pltpu.VMEM((ns, tm, VDIM), BF16),
        pltpu.VMEM((ns, tm, VDIM), F32),
        pltpu.VMEM((ns, tm, D_CONV), BF16),
        pltpu.VMEM((ns, tm, D_MODEL), F32),
        pltpu.VMEM((ns, tm, VDIM), F32),
        pltpu.VMEM((ns, tm, D_MODEL), F32),
        pltpu.VMEM((ns, tm, D_MODEL), F32),
    ]
    outs = pl.pallas_call(
        functools.partial(_mixer_seq_kernel, n_cast=len(cast_jobs), n_turn=len(turn_jobs)),
        grid=(bsz // ns, nt),
        in_specs=in_specs,
        out_specs=out_specs,
        out_shape=out_shape,
        scratch_shapes=scratch,
        compiler_params=pltpu.CompilerParams(
            dimension_semantics=("arbitrary", "arbitrary"),
            vmem_limit_bytes=VMEM_LIMIT),
        name="mixer_seq",
    )(x, x, cprev8, s0, *[p[0] for p in params], *ride_args)
    return outs[0], outs[1], outs[2], outs[3:]


def _mixer_step_kernel(x_ref, cp_ref, s_ref, wm_ref, wg_ref, wz_ref,
                       wa_ref, wb_ref, wo_ref, wgk_ref,
                       nmix_ref, convw_ref, bgk_ref, gain_ref, *rest):
    xo_ref, cnew_ref, sout_ref, qt_s, kt_s, at_s, v_s, o_s, ya_s = rest[-9:]
    i = pl.program_id(0)
    n = pl.num_programs(0)
    nseq = x_ref.shape[0]

    def xn_bf16():
        return _rms(x_ref[...], nmix_ref[...]).astype(BF16)

    @pl.when(i == 0)
    def _dense_in():
        xn = xn_bf16()
        u = _dot(xn, wm_ref[:, C_GC:C_GC + D_CONV]) * _dot(xn, wm_ref[:, C_H:C_H + D_CONV])
        cw = convw_ref[...]
        p1 = cp_ref[:, D_CONV:2 * D_CONV]
        yconv = cw[0:1] * cp_ref[:, 0:D_CONV] + cw[1:2] * p1 + cw[2:3] * u
        cnew_ref[:, 0:D_CONV] = p1
        cnew_ref[:, D_CONV:2 * D_CONV] = u
        ca = (_dot(xn, wm_ref[:, C_GB:C_GB + D_CONV]) * yconv).astype(BF16)
        ya_s[...] = _dot(ca, wa_ref[...])
        v_s[...] = _dot(xn, wm_ref[:, C_V:C_V + VDIM])
        qt_s[...] = (_dot(xn, wm_ref[:, C_Q:C_Q + KDIM]) * (DK ** -0.5)).T
        kt_s[...] = _dot(xn, wm_ref[:, C_K:C_K + KDIM]).T
        z = _dot(xn, wz_ref[...]).astype(BF16)
        g = _log_sigmoid(_dot(z, wgk_ref[...]) + bgk_ref[...]) * (1.0 / GATE_NORMALIZER)
        at_s[...] = jnp.exp(g).T

    shift = (nseq - i * SUB) % nseq
    lead = slice(0, 2 * SUB)
    qt = pltpu.roll(qt_s[...], shift, 1)[:, lead].astype(BF16)
    kt = pltpu.roll(kt_s[...], shift, 1)[:, lead].astype(BF16)
    at = jnp.concatenate(_split3(pltpu.roll(at_s[...], shift, 1)[:, lead]), axis=1)
    rid = lax.broadcasted_iota(jnp.int32, (2 * SUB, SUB * LANES), 0)
    cid = lax.broadcasted_iota(jnp.int32, (2 * SUB, SUB * LANES), 1)
    sel = (rid == (cid >> 7)).astype(BF16)
    sel3 = jnp.concatenate([sel, sel, sel], axis=0)
    for h in range(HEADS):
        rk = slice(h * DK, (h + 1) * DK)
        cv = slice(h * DV, (h + 1) * DV)
        qb = _dot(qt[rk], sel)
        kb = _dot(kt[rk], sel)
        ab = _dot(at[rk], sel3)
        for j in range(SUB):
            row = pl.ds(i * SUB + j, 1)
            grp = slice(j * LANES, (j + 1) * LANES)
            both = lambda c: jnp.concatenate([c[:, grp], c[:, grp]], axis=1)
            sn = s_ref[j, h] * both(ab) + both(kb) * v_s[row, cv]
            sout_ref[j, h] = sn
            o_s[row, cv] = jnp.sum(sn * both(qb), axis=0, keepdims=True)

    @pl.when(i == n - 1)
    def _dense_out():
        xn = xn_bf16()
        og = _dot(xn, wm_ref[:, C_OG:C_OG + VDIM])
        yb_in = (_head_norm(o_s[...], gain_ref[...]) * (og * _sigmoid(og))).astype(BF16)
        yb = _dot(yb_in, wb_ref[...])
        m = (_sigmoid(_dot(xn, wg_ref[:, 0:D_MODEL])) * ya_s[...]
             + _sigmoid(_dot(xn, wg_ref[:, D_MODEL:2 * D_MODEL])) * yb)
        xo_ref[...] = x_ref[...] + _dot(m.astype(BF16), wo_ref[...])


def _mixer_step(x, conv_all, s_all, s_out_prev, W, l):
    nseq = x.shape[0]
    assert nseq == LANES
    params = [W["w_main"], W["w_gate"], W["w_z"], W["w_a"], W["w_b"],
              W["w_o"], W["w_gk"], W["norm_mix"], W["conv_w"], W["b_gk"], W["gain"]]
    state_blk = (None, SUB, HEADS, DK, DV)
    in_specs = [
        _const_spec(x.shape),
        _layer_spec((conv_all, l)),
        pl.BlockSpec(state_blk, lambda i: (l, i, 0, 0, 0)),
    ] + [_layer_spec(p) for p in params]
    args = [x, conv_all, s_all, *[p[0] for p in params]]
    aliases = {}
    if s_out_prev is not None:
        in_specs.append(pl.BlockSpec(memory_space=pl.ANY))
        aliases = {len(args): 2}
        args.append(s_out_prev)
    out_shape = (
        jax.ShapeDtypeStruct(x.shape, F32),
        jax.ShapeDtypeStruct((nseq, 2 * D_CONV), F32),
        jax.ShapeDtypeStruct(s_all.shape, F32),
    )
    out_specs = (
        pl.BlockSpec(x.shape, lambda i: (0, 0)),
        pl.BlockSpec((nseq, 2 * D_CONV), lambda i: (0, 0)),
        pl.BlockSpec(state_blk, lambda i: (l, i, 0, 0, 0)),
    )
    scratch = [
        pltpu.VMEM((KDIM, nseq), F32),
        pltpu.VMEM((KDIM, nseq), F32),
        pltpu.VMEM((KDIM, nseq), F32),
        pltpu.VMEM((nseq, VDIM), F32),
        pltpu.VMEM((nseq, VDIM), F32),
        pltpu.VMEM((nseq, D_MODEL), F32),
    ]
    return pl.pallas_call(
        _mixer_step_kernel,
        grid=(nseq // SUB,),
        in_specs=in_specs,
        out_specs=out_specs,
        out_shape=out_shape,
        scratch_shapes=scratch,
        input_output_aliases=aliases,
        compiler_params=pltpu.CompilerParams(
            dimension_semantics=("arbitrary",),
            vmem_limit_bytes=VMEM_LIMIT),
        name="mixer_step",
    )(*args)


def _ffn_kernel(*refs, final, rb, n_cast):
    x_ref, small_ref, nffn_ref, wgu_ref, wdn_ref, fin_ref = refs[:6]
    cast_in = refs[6:6 + n_cast]
    o_ref, small_out_ref = refs[6 + n_cast:8 + n_cast]
    cast_out = refs[8 + n_cast:]
    _ride_along(pl.program_id(0), cast_in, cast_out, (), ())

    def ffn_rows(x):
        xn = _rms(x, nffn_ref[...]).astype(BF16)
        g = _dot(xn, wgu_ref[:, 0:D_FF])
        u = _dot(xn, wgu_ref[:, D_FF:2 * D_FF])
        hid = (g * _sigmoid(g) * u).astype(BF16)
        y = x + _dot(hid, wdn_ref[...])
        return _rms(y, fin_ref[...]) if final else y

    y0 = ffn_rows(jnp.concatenate([x_ref[0:rb, :], small_ref[...]], axis=0))
    o_ref[0:rb, :] = y0[:rb]
    small_out_ref[...] = y0[rb:]
    for r0 in range(rb, x_ref.shape[0], rb):
        o_ref[r0:r0 + rb, :] = ffn_rows(x_ref[r0:r0 + rb, :])


def _ffn(x, small, W, final, tm, cast_jobs=()):
    n = x.shape[0]
    assert n % tm == 0
    steps = n // tm
    per_step = -(-small.shape[0] // (steps * FFN_SMALL_ROWS)) * FFN_SMALL_ROWS
    small_p = jnp.pad(small, ((0, steps * per_step - small.shape[0]), (0, 0)))
    params = [W["norm_ffn"], W["w_gu"], W["w_down"]]
    fin = W["final_norm"]
    small_spec = pl.BlockSpec((per_step, D_MODEL), lambda i: (i, 0))
    ride_in, ride_args, ride_out, ride_shape = _ride_specs(steps, lambda i: i, cast_jobs, ())
    outs = pl.pallas_call(
        functools.partial(_ffn_kernel, final=final, rb=min(tm, FFN_ROW_BLOCK),
                          n_cast=len(cast_jobs)),
        grid=(steps,),
        in_specs=[pl.BlockSpec((tm, D_MODEL), lambda i: (i, 0)), small_spec]
        + [_layer_spec(p) for p in params] + [_const_spec(fin.shape)] + ride_in,
        out_specs=[pl.BlockSpec((tm, D_MODEL), lambda i: (i, 0)), small_spec] + ride_out,
        out_shape=[jax.ShapeDtypeStruct((n, D_MODEL), F32),
                   jax.ShapeDtypeStruct(small_p.shape, F32)] + ride_shape,
        compiler_params=pltpu.CompilerParams(
            dimension_semantics=("arbitrary",),
            vmem_limit_bytes=VMEM_LIMIT),
        name="ffn",
    )(x, small_p, *[p[0] for p in params], fin, *ride_args)
    return outs[0], outs[1][:small.shape[0]], outs[2:]


def _transpose_cast_kernel(wt_ref, o_ref):
    o_ref[...] = wt_ref[0].T.astype(BF16)


def _transpose_cast(wt, layer, rb=512):
    d = wt.shape[2]
    n_main = D_MAIN // rb

    def row_of(i):
        return pl.multiple_of(jnp.where(i < n_main, i * rb, Z_HI + (i - n_main) * rb), SUB)

    return pl.pallas_call(
        _transpose_cast_kernel,
        grid=((D_MAIN + 2 * D_MODEL) // rb,),
        in_specs=[pl.BlockSpec((pl.Element(1), pl.Element(rb), pl.Element(d)),
                               lambda i: (layer, row_of(i), 0))],
        out_specs=pl.BlockSpec((None, d, rb), lambda i: (0, 0, i)),
        out_shape=jax.ShapeDtypeStruct((1, d, D_MAIN + 2 * D_MODEL), BF16),
        compiler_params=pltpu.CompilerParams(
            dimension_semantics=("arbitrary",),
            vmem_limit_bytes=VMEM_LIMIT),
        name="transpose_cast",
    )(wt)


def _layer_params(l, big, w_in, conv_w, w_gk2, b_gk2, gla_gain, norm_mix, norm_ffn, final_norm):
    depth = w_in.shape[0]
    w_z = jnp.pad(w_in[:, :, Z_LO:Z_HI], ((0, 0), (0, 0), (0, LANES - GATE_RANK))).astype(BF16)
    w_gk = jnp.pad(w_gk2, ((0, 0), (0, LANES - GATE_RANK), (0, 0))).astype(BF16)
    W = {name: (arr, 0) for name, arr in big.items() if name != "w_all"}
    W.update({
        "w_main": (big["w_all"], 0, D_MAIN, 0),
        "w_gate": (big["w_all"], 0, 2 * D_MODEL, D_MAIN // (2 * D_MODEL)),
        "w_z": (w_z, l),
        "w_gk": (w_gk, l),
        "b_gk": (b_gk2.reshape(depth, 1, KDIM), l),
        "gain": (gla_gain.reshape(depth, 1, DV), l),
        "conv_w": (conv_w, l),
        "norm_mix": (norm_mix.reshape(depth, 1, D_MODEL), l),
        "norm_ffn": (norm_ffn.reshape(depth, 1, D_MODEL), l),
    })
    W["final_norm"] = final_norm.reshape(1, D_MODEL)
    return W


def kernel(x_prompt, x_sample, state_conv, state_gla, meta_tokens, w_in, conv_w, w_gk2, b_gk2,
           gla_gain, w_a_out, w_b_out, w_o, norm_mix, norm_ffn, w_gu, w_down, final_norm):
    depth = w_in.shape[0]
    bsz, seq, _ = x_prompt.shape
    nsmp = x_sample.shape[0]
    w_in_t = jnp.swapaxes(w_in, 1, 2)
    big = {
        "w_all": _transpose_cast(w_in_t, 0),
        "w_a": w_a_out[:1].astype(BF16),
        "w_b": w_b_out[:1].astype(BF16),
        "w_o": w_o[:1].astype(BF16),
        "w_down": w_down[:1].astype(BF16),
    }
    small_params = (w_in, conv_w, w_gk2, b_gk2, gla_gain, norm_mix, norm_ffn, final_norm)

    xm = jnp.pad(meta_tokens.astype(F32), ((CHUNK - N_META, 0), (0, 0)))[None]
    xp = x_prompt
    xs = x_sample.reshape(nsmp, D_MODEL)
    conv_all = state_conv.reshape(depth, nsmp, (CONV_W - 1) * D_CONV)
    zero_conv = jnp.zeros((1, SUB, D_CONV), F32)
    zero_state = jnp.zeros((1, HEADS, DK, DV), F32)

    p_conv, p_gla, s_conv = [], [], []
    s_gla = None
    for l in range(depth):
        last = l == depth - 1
        W = _layer_params(l, big, *small_params)
        own_casts = [] if "w_gu" in big else [(w_gu, l)]
        mixer_casts = own_casts + ([] if last else [(w_gu, l + 1), (w_a_out, l + 1),
                                                    (w_b_out, l + 1), (w_o, l + 1)])
        mixer_turns = [] if last else [(w_in_t, l + 1)]
        ffn_casts = [] if last else [(w_down, l + 1)]
        xm, m_conv, m_state, _ = _mixer_seq(xm, zero_conv, zero_state, W, CHUNK)
        xp, pc, ps, ride_m = _mixer_seq(xp, m_conv, m_state, W, PROMPT_TILE,
                                        mixer_casts, mixer_turns)
        if own_casts:
            W["w_gu"], ride_m = (ride_m[0], 0), ride_m[1:]
        xs, sc, s_gla = _mixer_step(xs, conv_all, state_gla, s_gla, W, l)
        small = xs if last else jnp.concatenate([xs, xm[0]], axis=0)
        yp, ysmall, ride_f = _ffn(xp.reshape(bsz * seq, D_MODEL), small, W, last, 1024,
                                  ffn_casts)
        xp, xs = yp.reshape(bsz, seq, D_MODEL), ysmall[:nsmp]
        if not last:
            xm = ysmall[nsmp:][None]
            big = dict(zip(("w_gu", "w_a", "w_b", "w_o", "w_all"), ride_m), w_down=ride_f[0])
        p_conv.append(pc[:, SUB - (CONV_W - 1):, :])
        p_gla.append(ps)
        s_conv.append(sc.reshape(nsmp, CONV_W - 1, D_CONV))

    return (xp, xs.reshape(nsmp, 1, D_MODEL), jnp.stack(p_conv), jnp.stack(p_gla),
            jnp.stack(s_conv), s_gla)
```

```python
import functools

import jax
import jax.numpy as jnp
from jax import lax
from jax.experimental import pallas as pl
from jax.experimental.pallas import tpu as pltpu

F32 = jnp.float32
BF16 = jnp.bfloat16

D_MODEL = 1024
N_META = 16
D_CONV = D_MODEL
CONV_W = 3
HEADS = 4
DK = 128
DV = 256
KDIM = HEADS * DK
VDIM = HEADS * DV
GATE_RANK = 16
GATE_NORMALIZER = 16.0
D_FF = 2816
EPS = 1e-6
LOG2_E = 1.4426950408889634

CHUNK = 64
SUB = 8
NB = CHUNK // SUB
LANES = 128
COLB = 512
FFN_ROW_BLOCK = 256
FFN_SMALL_ROWS = 16
PROMPT_TILE = 256
SEQS_PER_STEP = 1
VMEM_LIMIT = 60 * 1024 * 1024

C_GB, C_GC, C_H = 0, 1024, 2048
C_Q, C_K, C_V, C_OG = 3072, 3584, 4096, 5120
D_MAIN = 6144
Z_LO = D_MAIN
Z_HI = Z_LO + GATE_RANK


def _rms(x, g):
    ms = jnp.mean(x * x, axis=-1, keepdims=True)
    return x * lax.rsqrt(ms + EPS) * g


def _log_sigmoid(x):
    return jnp.minimum(x, 0.0) - jnp.log(1.0 + jnp.exp(-jnp.abs(x)))


def _sigmoid(x):
    return 1.0 / (1.0 + jnp.exp(-x))


def _dot(a, b):
    return jnp.dot(a, b, preferred_element_type=F32)


def _dot_nt(a, b):
    return lax.dot_general(a, b, (((1,), (1,)), ((), ())), preferred_element_type=F32)


def _dot_tn(a, b):
    return lax.dot_general(a, b, (((0,), (0,)), ((), ())), preferred_element_type=F32)


def _split3(x):
    x1 = x.astype(BF16)
    r1 = x - x1.astype(F32)
    x2 = r1.astype(BF16)
    r2 = r1 - x2.astype(F32)
    return x1, x2, r2.astype(BF16)


def _head_norm(o, gain):
    return jnp.concatenate(
        [_rms(o[:, h * DV:(h + 1) * DV], gain) for h in range(HEADS)], axis=1)


def _interleave(slots, pieces):
    done = 0
    for i, slot_fn in enumerate(slots):
        slot_fn()
        upto = ((i + 1) * len(pieces) + len(slots) - 1) // len(slots)
        while done < upto:
            pieces[done]()
            done += 1


def _ride_along(step, cast_in, cast_out, turn_in, turn_out):
    for src, dst in zip(cast_in, cast_out):
        dst[...] = src[...].astype(BF16)
    for src, dst in zip(turn_in, turn_out):
        dst[...] = src[0].T.astype(BF16)


def _ride_specs(steps, step_of, cast_jobs, turn_jobs):
    in_specs, args, out_specs, out_shape = [], [], [], []
    for arr, layer in cast_jobs:
        _, rows, cols = arr.shape
        assert rows % (steps * 2 * SUB) == 0
        blk = (None, rows // steps, cols)
        in_specs.append(pl.BlockSpec(blk, lambda *g, layer=layer: (layer, step_of(*g), 0)))
        out_specs.append(pl.BlockSpec(blk, lambda *g: (0, step_of(*g), 0)))
        out_shape.append(jax.ShapeDtypeStruct((1, rows, cols), BF16))
        args.append(arr)
    turn_specs, turn_shape = [], []
    for wt, layer in turn_jobs:
        d = wt.shape[2]
        width = (D_MAIN + 2 * D_MODEL) // steps
        assert width % LANES == 0 and D_MAIN % width == 0
        n_main = D_MAIN // width

        def row_of(*g, n_main=n_main, width=width):
            s = step_of(*g)
            return pl.multiple_of(jnp.where(s < n_main, s * width, Z_HI + (s - n_main) * width), SUB)

        in_specs.append(pl.BlockSpec(
            (pl.Element(1), pl.Element(width), pl.Element(d)),
            lambda *g, layer=layer, row_of=row_of: (layer, row_of(*g), 0)))
        turn_specs.append(pl.BlockSpec((None, d, width), lambda *g: (0, 0, step_of(*g))))
        turn_shape.append(jax.ShapeDtypeStruct((1, d, D_MAIN + 2 * D_MODEL), BF16))
        args.append(wt)
    return in_specs, args, out_specs + turn_specs, out_shape + turn_shape


def _mixer_seq_kernel(*refs, n_cast, n_turn):
    (x_ref, xnext_ref, cprev_ref, s0_ref, wm_ref, wg_ref, wz_ref, wa_ref, wb_ref,
     wo_ref, wgk_ref, nmix_ref, convw_ref, bgk_ref, gain_ref) = refs[:15]
    n_in = 15 + n_cast + n_turn
    cast_in, turn_in = refs[15:15 + n_cast], refs[15 + n_cast:n_in]
    xo_ref, cnew_ref, sout_ref = refs[n_in:n_in + 3]
    cast_out = refs[n_in + 3:n_in + 3 + n_cast]
    turn_out = refs[n_in + 3 + n_cast:n_in + 3 + n_cast + n_turn]
    (xn_s, ubuf, st_s, q_s, k_s, b_s, v_s, o_s, ca_s, ya_s, og_s, ga_s,
     gb_s) = refs[n_in + 3 + n_cast + n_turn:]
    t = pl.program_id(1)
    nt = pl.num_programs(1)
    nseq, tm = x_ref.shape[0], x_ref.shape[1]
    xn_buf = lax.bitwise_and(t, 1)
    xn_next_buf = lax.bitwise_and(t + 1, 1)


    @pl.when(t == 0)
    def _init():
        for p in range(nseq):
            ubuf[p, 0:SUB, :] = cprev_ref[0]
            xn_s[p, 0] = _rms(x_ref[p], nmix_ref[...]).astype(BF16)
            for h in range(HEADS):
                st_s[p, h] = s0_ref[0, h].T

    ti = lax.broadcasted_iota(jnp.int32, (CHUNK, CHUNK), 0)
    si = lax.broadcasted_iota(jnp.int32, (CHUNK, CHUNK), 1)
    diag_sel = [(si == ((ti >> 3) << 3) + j) & ((ti & 7) >= j) for j in range(SUB)]
    lane_blk = lax.broadcasted_iota(jnp.int32, (SUB, CHUNK), 1) >> 3
    stack_off = [sum(CHUNK - SUB * (i + 1) for i in range(j)) for j in range(NB)]
    ncb = D_MODEL // COLB

    def cols_of(cb):
        return slice(cb * COLB, (cb + 1) * COLB)

    def make_stream(p):
        xn = xn_s[p, xn_buf]
        vals = {}

        def proj(w_ref, c0, n):
            return _dot(xn, w_ref[:, c0:c0 + n])

        def head_gate():
            z = proj(wz_ref, 0, LANES).astype(BF16)
            g = (_log_sigmoid(_dot(z, wgk_ref[...]) + bgk_ref[...])
                 * (LOG2_E / GATE_NORMALIZER))
            row = lax.broadcasted_iota(jnp.int32, g.shape, 0) & (CHUNK - 1)
            step = 1
            while step < CHUNK:
                g = g + jnp.where(row >= step, pltpu.roll(g, step, 0), 0.0)
                step *= 2
            b_s[p] = g

        def head_q():
            q_s[p] = proj(wm_ref, C_Q, KDIM) * (DK ** -0.5)

        def head_k():
            k_s[p] = proj(wm_ref, C_K, KDIM)

        def head_v():
            v_s[p] = proj(wm_ref, C_V, VDIM).astype(BF16)

        head = [head_gate, head_q, head_k, head_v]

        def mid_u():
            r = proj(wm_ref, C_GC, 2 * D_CONV)
            ubuf[p, SUB:SUB + tm, :] = r[:, :D_CONV] * r[:, D_CONV:]

        def mid_conv(cb):
            cols = cols_of(cb)
            cw = convw_ref[:, cols]
            yconv = (cw[0:1] * ubuf[p, SUB - 2:SUB - 2 + tm, cols]
                     + cw[1:2] * ubuf[p, SUB - 1:SUB - 1 + tm, cols]
                     + cw[2:3] * ubuf[p, SUB:SUB + tm, cols])
            ca_s[p, :, cols] = (proj(wm_ref, C_GB + cb * COLB, COLB) * yconv).astype(BF16)
            tail = ubuf[p, tm:tm + SUB, cols]
            ubuf[p, 0:SUB, cols] = tail
            cnew_ref[p, :, cols] = tail

        def mid_ya(cb):
            ya_s[p, :, cols_of(cb)] = _dot(ca_s[p], wa_ref[:, cols_of(cb)])

        def mid_norm_next():
            xn_s[p, xn_next_buf] = _rms(xnext_ref[p], nmix_ref[...]).astype(BF16)

        def mid_og(cb):
            og = proj(wm_ref, C_OG + cb * COLB, COLB)
            og_s[p, :, cols_of(cb)] = og * _sigmoid(og)

        def mid_ga(cb):
            ga_s[p, :, cols_of(cb)] = _sigmoid(proj(wg_ref, cb * COLB, COLB))

        def mid_gb(cb):
            gb_s[p, :, cols_of(cb)] = _sigmoid(proj(wg_ref, D_MODEL + cb * COLB, COLB))

        mid = [mid_norm_next, mid_u]
        mid += [functools.partial(mid_conv, cb) for cb in range(ncb)]
        for fn in (mid_ya, mid_og, mid_ga, mid_gb):
            mid += [functools.partial(fn, cb) for cb in range(ncb)]

        def gla_diag(c, h):
            r0 = c * CHUNK
            rows = slice(r0, r0 + CHUNK)
            kcols = slice(h * DK, (h + 1) * DK)
            qc = q_s[p, rows, kcols]
            bc = b_s[p, rows, kcols]

            def row_of_each_block(ref, j):
                return jnp.concatenate(
                    [jnp.broadcast_to(ref[p, r0 + SUB * i + j:r0 + SUB * i + j + 1, kcols],
                                      (SUB, DK)) for i in range(NB)], axis=0)

            acc = jnp.zeros((CHUNK, CHUNK), F32)
            for j in range(SUB):
                kj = row_of_each_block(k_s, j)
                bj = row_of_each_block(b_s, j)
                rj = jnp.sum(qc * kj * jnp.exp2(bc - bj), axis=-1, keepdims=True)
                acc = jnp.where(diag_sel[j], rj, acc)
            vals[("diag", c, h)] = acc

        def gla_rest(c, h):
            r0 = c * CHUNK
            rows = slice(r0, r0 + CHUNK)
            kcols = slice(h * DK, (h + 1) * DK)
            vcols = slice(h * DV, (h + 1) * DV)
            qc = q_s[p, rows, kcols]
            kc = k_s[p, rows, kcols]
            bc = b_s[p, rows, kcols]
            vc = v_s[p, rows, vcols]
            st = st_s[p, h]
            bend = [b_s[p, r0 + SUB * j + SUB - 1:r0 + SUB * (j + 1), kcols] for j in range(NB)]
            bend_rows = jnp.concatenate([jnp.broadcast_to(e, (SUB, DK)) for e in bend], axis=0)
            blast = bend[NB - 1]
            qhat = (qc * jnp.exp2(bc)).astype(BF16)
            o = _dot_nt(qhat, st.astype(BF16))
            kt = (kc * jnp.exp2(bend_rows - bc)).astype(BF16)
            qs = [qc[SUB * (j + 1):] * jnp.exp2(bc[SUB * (j + 1):] - bend[j])
                  for j in range(NB - 1)]
            rm = _dot_nt(jnp.concatenate(qs, axis=0).astype(BF16), kt)
            arows = [jnp.zeros((SUB, CHUNK), F32)]
            for i in range(1, NB):
                acc = jnp.zeros((SUB, CHUNK), F32)
                for j in range(i):
                    s0 = stack_off[j] + SUB * (i - j - 1)
                    acc = jnp.where(lane_blk == j, rm[s0:s0 + SUB], acc)
                arows.append(acc)
            a = jnp.concatenate(arows, axis=0) + vals.pop(("diag", c, h))
            o_s[p, rows, vcols] = o + _dot(a.astype(BF16), vc)
            khat = (kc * jnp.exp2(blast - bc)).astype(BF16)
            st_s[p, h] = st * jnp.exp2(blast) + _dot_tn(vc, khat)

        gla = []
        for c in range(tm // CHUNK):
            for h in range(HEADS):
                gla += [functools.partial(gla_diag, c, h), functools.partial(gla_rest, c, h)]

        def tail_yb(rs):
            yb_in = jnp.concatenate(
                [(_rms(o_s[p, rs, h * DV:(h + 1) * DV], gain_ref[...])
                  * og_s[p, rs, h * DV:(h + 1) * DV]).astype(BF16) for h in range(HEADS)], axis=1)
            vals[("yb", rs.start)] = _dot(yb_in, wb_ref[...])

        def tail_out(rs):
            yb = vals.pop(("yb", rs.start))
            m = jnp.concatenate(
                [(ga_s[p, rs, cols_of(cb)] * ya_s[p, rs, cols_of(cb)]
                  + gb_s[p, rs, cols_of(cb)] * yb[:, cols_of(cb)]).astype(BF16)
                 for cb in range(ncb)], axis=1)
            xo_ref[p, rs, :] = x_ref[p, rs, :] + _dot(m, wo_ref[...])

        def tail(rs):
            return [functools.partial(tail_yb, rs), functools.partial(tail_out, rs)]

        return head, mid, gla, tail

    for p in range(nseq):
        head, mid, gla, tail = make_stream(p)
        for fn in head:
            fn()
        if p == 0:
            _ride_along(pl.program_id(0) * nt + t, cast_in, cast_out, turn_in, turn_out)
        _interleave(gla, mid)
        for fn in tail(slice(0, tm)):
            fn()

    @pl.when(t == nt - 1)
    def _fin():
        for p in range(nseq):
            for h in range(HEADS):
                sout_ref[p, h] = st_s[p, h].T


def _const_spec(shape):
    nd = len(shape)
    return pl.BlockSpec(shape, lambda *_: (0,) * nd, pipeline_mode=pl.Buffered(1))


def _layer_spec(entry):
    arr, idx = entry[:2]
    nd = arr.ndim
    if len(entry) == 4:
        ncols, cblk = entry[2:]
        return pl.BlockSpec((None, arr.shape[1], ncols), lambda *_: (idx, 0, cblk),
                            pipeline_mode=pl.Buffered(1))
    return pl.BlockSpec((None,) + arr.shape[1:], lambda *_: (idx,) + (0,) * (nd - 1),
                        pipeline_mode=pl.Buffered(1))


def _mixer_seq(x, cprev8, s0, W, tm, cast_jobs=(), turn_jobs=()):
    bsz, T, _ = x.shape
    assert T % tm == 0 and tm % CHUNK == 0
    ns = SEQS_PER_STEP if bsz % SEQS_PER_STEP == 0 else 1
    params = [W["w_main"], W["w_gate"], W["w_z"], W["w_a"], W["w_b"], W["w_o"], W["w_gk"],
              W["norm_mix"], W["conv_w"], W["b_gk"], W["gain"]]
    nt = T // tm
    ride_in, ride_args, ride_out, ride_shape = _ride_specs(
        (bsz // ns) * nt, lambda b, t: b * nt + t, cast_jobs, turn_jobs)
    in_specs = [
        pl.BlockSpec((ns, tm, D_MODEL), lambda b, t: (b, t, 0)),
        pl.BlockSpec((ns, tm, D_MODEL), lambda b, t: (b, jnp.minimum(t + 1, nt - 1), 0)),
        pl.BlockSpec((1, SUB, D_CONV), lambda b, t: (0, 0, 0)),
        pl.BlockSpec((1, HEADS, DK, DV), lambda b, t: (0, 0, 0, 0)),
    ] + [_layer_spec(p) for p in params] + ride_in
    out_shape = [
        jax.ShapeDtypeStruct((bsz, T, D_MODEL), F32),
        jax.ShapeDtypeStruct((bsz, SUB, D_CONV), F32),
        jax.ShapeDtypeStruct((bsz, HEADS, DK, DV), F32),
    ] + ride_shape
    out_specs = [
        pl.BlockSpec((ns, tm, D_MODEL), lambda b, t: (b, t, 0)),
        pl.BlockSpec((ns, SUB, D_CONV), lambda b, t: (b, 0, 0)),
        pl.BlockSpec((ns, HEADS, DK, DV), lambda b, t: (b, 0, 0, 0)),
    ] + ride_out
    scratch = [
        pltpu.VMEM((ns, 2, tm, D_MODEL), BF16),
        pltpu.VMEM((ns, tm + SUB, D_CONV), F32),
        pltpu.VMEM((ns, HEADS, DV, DK), F32),
        pltpu.VMEM((ns, tm, KDIM), F32),
        pltpu.VMEM((ns, tm, KDIM), F32),
        pltpu.VMEM((ns, tm, KDIM), F32),
        pltpu.VMEM((ns, tm, VDIM), BF16),
        pltpu.VMEM((ns, tm, VDIM), F32),
        pltpu.VMEM((ns, tm, D_CONV), BF16),
        pltpu.VMEM((ns, tm, D_MODEL), F32),
        pltpu.VMEM((ns, tm, VDIM), F32),
        pltpu.VMEM((ns, tm, D_MODEL), F32),
        pltpu.VMEM((ns, tm, D_MODEL), F32),
    ]
    outs = pl.pallas_call(
        functools.partial(_mixer_seq_kernel, n_cast=len(cast_jobs), n_turn=len(turn_jobs)),
        grid=(bsz // ns, nt),
        in_specs=in_specs,
        out_specs=out_specs,
        out_shape=out_shape,
        scratch_shapes=scratch,
        compiler_params=pltpu.CompilerParams(
            dimension_semantics=("arbitrary", "arbitrary"),
            vmem_limit_bytes=VMEM_LIMIT),
        name="mixer_seq",
    )(x, x, cprev8, s0, *[p[0] for p in params], *ride_args)
    return outs[0], outs[1], outs[2], outs[3:]


def _mixer_step_kernel(x_ref, cp_ref, s_ref, wm_ref, wg_ref, wz_ref,
                       wa_ref, wb_ref, wo_ref, wgk_ref,
                       nmix_ref, convw_ref, bgk_ref, gain_ref, *rest):
    xo_ref, cnew_ref, sout_ref, qt_s, kt_s, at_s, v_s, o_s, ya_s = rest[-9:]
    i = pl.program_id(0)
    n = pl.num_programs(0)
    nseq = x_ref.shape[0]

    def xn_bf16():
        return _rms(x_ref[...], nmix_ref[...]).astype(BF16)

    @pl.when(i == 0)
    def _dense_in():
        xn = xn_bf16()
        u = _dot(xn, wm_ref[:, C_GC:C_GC + D_CONV]) * _dot(xn, wm_ref[:, C_H:C_H + D_CONV])
        cw = convw_ref[...]
        p1 = cp_ref[:, D_CONV:2 * D_CONV]
        yconv = cw[0:1] * cp_ref[:, 0:D_CONV] + cw[1:2] * p1 + cw[2:3] * u
        cnew_ref[:, 0:D_CONV] = p1
        cnew_ref[:, D_CONV:2 * D_CONV] = u
        ca = (_dot(xn, wm_ref[:, C_GB:C_GB + D_CONV]) * yconv).astype(BF16)
        ya_s[...] = _dot(ca, wa_ref[...])
        v_s[...] = _dot(xn, wm_ref[:, C_V:C_V + VDIM])
        qt_s[...] = (_dot(xn, wm_ref[:, C_Q:C_Q + KDIM]) * (DK ** -0.5)).T
        kt_s[...] = _dot(xn, wm_ref[:, C_K:C_K + KDIM]).T
        z = _dot(xn, wz_ref[...]).astype(BF16)
        g = _log_sigmoid(_dot(z, wgk_ref[...]) + bgk_ref[...]) * (1.0 / GATE_NORMALIZER)
        at_s[...] = jnp.exp(g).T

    shift = (nseq - i * SUB) % nseq
    lead = slice(0, 2 * SUB)
    qt = pltpu.roll(qt_s[...], shift, 1)[:, lead].astype(BF16)
    kt = pltpu.roll(kt_s[...], shift, 1)[:, lead].astype(BF16)
    at = jnp.concatenate(_split3(pltpu.roll(at_s[...], shift, 1)[:, lead]), axis=1)
    rid = lax.broadcasted_iota(jnp.int32, (2 * SUB, SUB * LANES), 0)
    cid = lax.broadcasted_iota(jnp.int32, (2 * SUB, SUB * LANES), 1)
    sel = (rid == (cid >> 7)).astype(BF16)
    sel3 = jnp.concatenate([sel, sel, sel], axis=0)
    for h in range(HEADS):
        rk = slice(h * DK, (h + 1) * DK)
        cv = slice(h * DV, (h + 1) * DV)
        qb = _dot(qt[rk], sel)
        kb = _dot(kt[rk], sel)
        ab = _dot(at[rk], sel3)
        for j in range(SUB):
            row = pl.ds(i * SUB + j, 1)
            grp = slice(j * LANES, (j + 1) * LANES)
            both = lambda c: jnp.concatenate([c[:, grp], c[:, grp]], axis=1)
            sn = s_ref[j, h] * both(ab) + both(kb) * v_s[row, cv]
            sout_ref[j, h] = sn
            o_s[row, cv] = jnp.sum(sn * both(qb), axis=0, keepdims=True)

    @pl.when(i == n - 1)
    def _dense_out():
        xn = xn_bf16()
        og = _dot(xn, wm_ref[:, C_OG:C_OG + VDIM])
        yb_in = (_head_norm(o_s[...], gain_ref[...]) * (og * _sigmoid(og))).astype(BF16)
        yb = _dot(yb_in, wb_ref[...])
        m = (_sigmoid(_dot(xn, wg_ref[:, 0:D_MODEL])) * ya_s[...]
             + _sigmoid(_dot(xn, wg_ref[:, D_MODEL:2 * D_MODEL])) * yb)
        xo_ref[...] = x_ref[...] + _dot(m.astype(BF16), wo_ref[...])


def _mixer_step(x, conv_all, s_all, s_out_prev, W, l):
    nseq = x.shape[0]
    assert nseq == LANES
    params = [W["w_main"], W["w_gate"], W["w_z"], W["w_a"], W["w_b"],
              W["w_o"], W["w_gk"], W["norm_mix"], W["conv_w"], W["b_gk"], W["gain"]]
    state_blk = (None, SUB, HEADS, DK, DV)
    in_specs = [
        _const_spec(x.shape),
        _layer_spec((conv_all, l)),
        pl.BlockSpec(state_blk, lambda i: (l, i, 0, 0, 0)),
    ] + [_layer_spec(p) for p in params]
    args = [x, conv_all, s_all, *[p[0] for p in params]]
    aliases = {}
    if s_out_prev is not None:
        in_specs.append(pl.BlockSpec(memory_space=pl.ANY))
        aliases = {len(args): 2}
        args.append(s_out_prev)
    out_shape = (
        jax.ShapeDtypeStruct(x.shape, F32),
        jax.ShapeDtypeStruct((nseq, 2 * D_CONV), F32),
        jax.ShapeDtypeStruct(s_all.shape, F32),
    )
    out_specs = (
        pl.BlockSpec(x.shape, lambda i: (0, 0)),
        pl.BlockSpec((nseq, 2 * D_CONV), lambda i: (0, 0)),
        pl.BlockSpec(state_blk, lambda i: (l, i, 0, 0, 0)),
    )
    scratch = [
        pltpu.VMEM((KDIM, nseq), F32),
        pltpu.VMEM((KDIM, nseq), F32),
        pltpu.VMEM((KDIM, nseq), F32),
        pltpu.VMEM((nseq, VDIM), F32),
        pltpu.VMEM((nseq, VDIM), F32),
        pltpu.VMEM((nseq, D_MODEL), F32),
    ]
    return pl.pallas_call(
        _mixer_step_kernel,
        grid=(nseq // SUB,),
        in_specs=in_specs,
        out_specs=out_specs,
        out_shape=out_shape,
        scratch_shapes=scratch,
        input_output_aliases=aliases,
        compiler_params=pltpu.CompilerParams(
            dimension_semantics=("arbitrary",),
            vmem_limit_bytes=VMEM_LIMIT),
        name="mixer_step",
    )(*args)


def _ffn_kernel(*refs, final, rb, n_cast):
    x_ref, small_ref, nffn_ref, wgu_ref, wdn_ref, fin_ref = refs[:6]
    cast_in = refs[6:6 + n_cast]
    o_ref, small_out_ref = refs[6 + n_cast:8 + n_cast]
    cast_out = refs[8 + n_cast:]
    _ride_along(pl.program_id(0), cast_in, cast_out, (), ())

    def ffn_rows(x):
        xn = _rms(x, nffn_ref[...]).astype(BF16)
        g = _dot(xn, wgu_ref[:, 0:D_FF])
        u = _dot(xn, wgu_ref[:, D_FF:2 * D_FF])
        hid = (g * _sigmoid(g) * u).astype(BF16)
        y = x + _dot(hid, wdn_ref[...])
        return _rms(y, fin_ref[...]) if final else y

    y0 = ffn_rows(jnp.concatenate([x_ref[0:rb, :], small_ref[...]], axis=0))
    o_ref[0:rb, :] = y0[:rb]
    small_out_ref[...] = y0[rb:]
    for r0 in range(rb, x_ref.shape[0], rb):
        o_ref[r0:r0 + rb, :] = ffn_rows(x_ref[r0:r0 + rb, :])


def _ffn(x, small, W, final, tm, cast_jobs=()):
    n = x.shape[0]
    assert n % tm == 0
    steps = n // tm
    per_step = -(-small.shape[0] // (steps * FFN_SMALL_ROWS)) * FFN_SMALL_ROWS
    small_p = jnp.pad(small, ((0, steps * per_step - small.shape[0]), (0, 0)))
    params = [W["norm_ffn"], W["w_gu"], W["w_down"]]
    fin = W["final_norm"]
    small_spec = pl.BlockSpec((per_step, D_MODEL), lambda i: (i, 0))
    ride_in, ride_args, ride_out, ride_shape = _ride_specs(steps, lambda i: i, cast_jobs, ())
    outs = pl.pallas_call(
        functools.partial(_ffn_kernel, final=final, rb=min(tm, FFN_ROW_BLOCK),
                          n_cast=len(cast_jobs)),
        grid=(steps,),
        in_specs=[pl.BlockSpec((tm, D_MODEL), lambda i: (i, 0)), small_spec]
        + [_layer_spec(p) for p in params] + [_const_spec(fin.shape)] + ride_in,
        out_specs=[pl.BlockSpec((tm, D_MODEL), lambda i: (i, 0)), small_spec] + ride_out,
        out_shape=[jax.ShapeDtypeStruct((n, D_MODEL), F32),
                   jax.ShapeDtypeStruct(small_p.shape, F32)] + ride_shape,
        compiler_params=pltpu.CompilerParams(
            dimension_semantics=("arbitrary",),
            vmem_limit_bytes=VMEM_LIMIT),
        name="ffn",
    )(x, small_p, *[p[0] for p in params], fin, *ride_args)
    return outs[0], outs[1][:small.shape[0]], outs[2:]


def _transpose_cast_kernel(wt_ref, o_ref):
    o_ref[...] = wt_ref[0].T.astype(BF16)


def _transpose_cast(wt, layer, rb=512):
    d = wt.shape[2]
    n_main = D_MAIN // rb

    def row_of(i):
        return pl.multiple_of(jnp.where(i < n_main, i * rb, Z_HI + (i - n_main) * rb), SUB)

    return pl.pallas_call(
        _transpose_cast_kernel,
        grid=((D_MAIN + 2 * D_MODEL) // rb,),
        in_specs=[pl.BlockSpec((pl.Element(1), pl.Element(rb), pl.Element(d)),
                               lambda i: (layer, row_of(i), 0))],
        out_specs=pl.BlockSpec((None, d, rb), lambda i: (0, 0, i)),
        out_shape=jax.ShapeDtypeStruct((1, d, D_MAIN + 2 * D_MODEL), BF16),
        compiler_params=pltpu.CompilerParams(
            dimension_semantics=("arbitrary",),
            vmem_limit_bytes=VMEM_LIMIT),
        name="transpose_cast",
    )(wt)


def _layer_params(l, big, w_in, conv_w, w_gk2, b_gk2, gla_gain, norm_mix, norm_ffn, final_norm):
    depth = w_in.shape[0]
    w_z = jnp.pad(w_in[:, :, Z_LO:Z_HI], ((0, 0), (0, 0), (0, LANES - GATE_RANK))).astype(BF16)
    w_gk = jnp.pad(w_gk2, ((0, 0), (0, LANES - GATE_RANK), (0, 0))).astype(BF16)
    W = {name: (arr, 0) for name, arr in big.items() if name != "w_all"}
    W.update({
        "w_main": (big["w_all"], 0, D_MAIN, 0),
        "w_gate": (big["w_all"], 0, 2 * D_MODEL, D_MAIN // (2 * D_MODEL)),
        "w_z": (w_z, l),
        "w_gk": (w_gk, l),
        "b_gk": (b_gk2.reshape(depth, 1, KDIM), l),
        "gain": (gla_gain.reshape(depth, 1, DV), l),
        "conv_w": (conv_w, l),
        "norm_mix": (norm_mix.reshape(depth, 1, D_MODEL), l),
        "norm_ffn": (norm_ffn.reshape(depth, 1, D_MODEL), l),
    })
    W["final_norm"] = final_norm.reshape(1, D_MODEL)
    return W


def kernel(x_prompt, x_sample, state_conv, state_gla, meta_tokens, w_in, conv_w, w_gk2, b_gk2,
           gla_gain, w_a_out, w_b_out, w_o, norm_mix, norm_ffn, w_gu, w_down, final_norm):
    depth = w_in.shape[0]
    bsz, seq, _ = x_prompt.shape
    nsmp = x_sample.shape[0]
    w_in_t = jnp.swapaxes(w_in, 1, 2)
    big = {
        "w_all": _transpose_cast(w_in_t, 0),
        "w_a": w_a_out[:1].astype(BF16),
        "w_b": w_b_out[:1].astype(BF16),
        "w_o": w_o[:1].astype(BF16),
        "w_down": w_down[:1].astype(BF16),
    }
    small_params = (w_in, conv_w, w_gk2, b_gk2, gla_gain, norm_mix, norm_ffn, final_norm)

    xm = jnp.pad(meta_tokens.astype(F32), ((CHUNK - N_META, 0), (0, 0)))[None]
    xp = x_prompt
    xs = x_sample.reshape(nsmp, D_MODEL)
    conv_all = state_conv.reshape(depth, nsmp, (CONV_W - 1) * D_CONV)
    zero_conv = jnp.zeros((1, SUB, D_CONV), F32)
    zero_state = jnp.zeros((1, HEADS, DK, DV), F32)

    p_conv, p_gla, s_conv = [], [], []
    s_gla = None
    for l in range(depth):
        last = l == depth - 1
        W = _layer_params(l, big, *small_params)
        own_casts = [] if "w_gu" in big else [(w_gu, l)]
        mixer_casts = own_casts + ([] if last else [(w_gu, l + 1), (w_a_out, l + 1),
                                                    (w_b_out, l + 1), (w_o, l + 1)])
        mixer_turns = [] if last else [(w_in_t, l + 1)]
        ffn_casts = [] if last else [(w_down, l + 1)]
        xm, m_conv, m_state, _ = _mixer_seq(xm, zero_conv, zero_state, W, CHUNK)
        xp, pc, ps, ride_m = _mixer_seq(xp, m_conv, m_state, W, PROMPT_TILE,
                                        mixer_casts, mixer_turns)
        if own_casts:
            W["w_gu"], ride_m = (ride_m[0], 0), ride_m[1:]
        xs, sc, s_gla = _mixer_step(xs, conv_all, state_gla, s_gla, W, l)
        small = xs if last else jnp.concatenate([xs, xm[0]], axis=0)
        yp, ysmall, ride_f = _ffn(xp.reshape(bsz * seq, D_MODEL), small, W, last, 1024,
                                  ffn_casts)
        xp, xs = yp.reshape(bsz, seq, D_MODEL), ysmall[:nsmp]
        if not last:
            xm = ysmall[nsmp:][None]
            big = dict(zip(("w_gu", "w_a", "w_b", "w_o", "w_all"), ride_m), w_down=ride_f[0])
        p_conv.append(pc[:, SUB - (CONV_W - 1):, :])
        p_gla.append(ps)
        s_conv.append(sc.reshape(nsmp, CONV_W - 1, D_CONV))

    return (xp, xs.reshape(nsmp, 1, D_MODEL), jnp.stack(p_conv), jnp.stack(p_gla),
            jnp.stack(s_conv), s_gla)
```
